```python
import math
import jax, jax.numpy as jnp
from jax import lax
import numpy as np

D_MODEL = 4096
BATCH = 1
SEQ = 16384
DEPTH = 1
DEC_BATCH = 1
DEC_SEQ = 8192
PAST_LEN = 128

HEAD_DIM = 128
HEADS_PER_GROUP = 8
DILATED_GROUPS = ((128, 1), (512, 4), (2048, 16))
N_GROUPS = len(DILATED_GROUPS)
N_ATTN_HEADS = HEADS_PER_GROUP * N_GROUPS
ATTN_WIDTH = N_ATTN_HEADS * HEAD_DIM
ATTN_OUT_WIDTH = HEADS_PER_GROUP * HEAD_DIM
FOURIER_WIDTH = D_MODEL // 4
FOURIER_GROUPS = 4
FOURIER_GROUP_DIM = FOURIER_WIDTH // FOURIER_GROUPS
N_BRANCHES = 2
IN_WIDTH = 3 * ATTN_WIDTH + FOURIER_WIDTH + N_BRANCHES * D_MODEL
SPLIT_POINTS = (ATTN_WIDTH, 2 * ATTN_WIDTH, 3 * ATTN_WIDTH,
                3 * ATTN_WIDTH + FOURIER_WIDTH, 3 * ATTN_WIDTH + FOURIER_WIDTH + D_MODEL)
D_FF = -(-(8 * D_MODEL) // (3 * 256)) * 256
N_MOD = 6
ROPE_THETA = 10000.0
LN_EPS = 1e-5
NEG_INF = -1e30
DEEPNORM_ALPHA = (2.0 * DEPTH) ** 0.25
DEEPNORM_BETA = (8.0 * DEPTH) ** -0.25

kernel_name = 'dilated_fourier_hybrid_encoder'


def layer_norm(x, g=None, b=None):
    x32 = x.astype(jnp.float32)
    mu = jnp.mean(x32, axis=-1, keepdims=True)
    xc = x32 - mu
    var = jnp.mean(xc * xc, axis=-1, keepdims=True)
    y = xc * lax.rsqrt(var + LN_EPS)
    if g is not None:
        y = y * g.astype(jnp.float32) + b.astype(jnp.float32)
    return y.astype(x.dtype)


def rope(t, pos):
    half = HEAD_DIM // 2
    inv = ROPE_THETA ** (-jnp.arange(half, dtype=jnp.float32) / half)
    ang = pos.astype(jnp.float32)[:, None] * inv[None, :]
    cos = jnp.cos(ang)[None, :, None, :]
    sin = jnp.sin(ang)[None, :, None, :]
    t32 = t.astype(jnp.float32)
    t1, t2 = t32[..., :half], t32[..., half:]
    return jnp.concatenate([t1 * cos - t2 * sin, t2 * cos + t1 * sin], axis=-1).astype(t.dtype)


def banded_attention(q, k, v, radius):
    n, L, h, dh = q.shape
    blk = radius
    nb = -(-L // blk)
    lp = nb * blk
    qb = jnp.pad(q, ((0, 0), (0, lp - L), (0, 0), (0, 0))).reshape(n, nb, blk, h, dh)

    def kv_blocks(t):
        tp = jnp.pad(t, ((0, 0), (blk, lp - L + blk), (0, 0), (0, 0))).reshape(n, nb + 2, blk, h, dh)
        return jnp.concatenate([tp[:, :-2], tp[:, 1:-1], tp[:, 2:]], axis=2)

    kb = kv_blocks(k)
    vb = kv_blocks(v)
    s = jnp.einsum('nbqhd,nbkhd->nbhqk', qb, kb, preferred_element_type=jnp.float32) * (dh ** -0.5)
    qi = jnp.arange(blk)[:, None]
    kj = jnp.arange(3 * blk)[None, :]
    rel = kj - qi
    band = (rel >= 0) & (rel <= 2 * blk)
    kpos = (jnp.arange(nb)[:, None] - 1) * blk + jnp.arange(3 * blk)[None, :]
    valid = (kpos >= 0) & (kpos < L)
    mask = band[None, :, :] & valid[:, None, :]
    s = jnp.where(mask[None, :, None, :, :], s, NEG_INF)
    m = jnp.max(s, axis=-1, keepdims=True)
    p = jnp.exp(s - m)
    den = jnp.sum(p, axis=-1, keepdims=True)
    o = jnp.einsum('nbhqk,nbkhd->nbqhd', p, vb.astype(jnp.float32))
    o = o / jnp.swapaxes(den, 2, 3)
    lse = jnp.swapaxes((m + jnp.log(den))[..., 0], 2, 3)
    o = o.reshape(n, lp, h, dh)[:, :L]
    lse = lse.reshape(n, lp, h)[:, :L]
    return o, lse


def dilated_attention(q, k, v, dilation, radius):
    b, s, h, dh = q.shape
    L = s // dilation

    def split(t):
        return t.reshape(b, L, dilation, h, dh).transpose(0, 2, 1, 3, 4).reshape(b * dilation, L, h, dh)

    o, lse = banded_attention(split(q), split(k), split(v), radius)
    o = o.reshape(b, dilation, L, h, dh).transpose(0, 2, 1, 3, 4).reshape(b, s, h, dh)
    lse = lse.reshape(b, dilation, L, h).transpose(0, 2, 1, 3).reshape(b, s, h)
    return o, lse


def fourier_mix(f):
    b, s, _ = f.shape
    fg = f.astype(jnp.float32).reshape(b, s, FOURIER_GROUPS, FOURIER_GROUP_DIM)
    out = jnp.fft.fft2(fg, axes=(1, 3), norm='ortho').real
    return out.reshape(b, s, FOURIER_WIDTH).astype(f.dtype)


def encoder_layer(x, c, w_ada, b_ada, w_in, w_attn_up, w_fourier_up, w_mix_out,
                  ln1_g, ln1_b, w_gate, w_up, w_down, ln2_g, ln2_b):
    b, s, d = x.shape
    mod = jnp.dot(jax.nn.silu(c), w_ada) + b_ada
    shift_m, scale_m, gate_m, shift_f, scale_f, gate_f = [t[:, None, :] for t in jnp.split(mod, N_MOD, axis=-1)]

    h = layer_norm(x) * (1 + scale_m) + shift_m
    proj = jnp.dot(h, w_in)
    q, k, v, f, g_attn, g_four = jnp.split(proj, SPLIT_POINTS, axis=-1)
    pos = jnp.arange(s)
    q = rope(q.reshape(b, s, N_ATTN_HEADS, HEAD_DIM), pos)
    k = rope(k.reshape(b, s, N_ATTN_HEADS, HEAD_DIM), pos)
    v = v.reshape(b, s, N_ATTN_HEADS, HEAD_DIM)
    outs, lses = [], []
    for gi, (win, dil) in enumerate(DILATED_GROUPS):
        sl = slice(gi * HEADS_PER_GROUP, (gi + 1) * HEADS_PER_GROUP)
        o, l = dilated_attention(q[:, :, sl], k[:, :, sl], v[:, :, sl], dil, win // (2 * dil))
        outs.append(o)
        lses.append(l)
    wts = jax.nn.softmax(jnp.stack(lses), axis=0)
    attn = jnp.einsum('gbsh,gbshd->bshd', wts, jnp.stack(outs)).astype(x.dtype)
    a_branch = jnp.dot(attn.reshape(b, s, ATTN_OUT_WIDTH), w_attn_up)
    f_branch = jnp.dot(fourier_mix(f), w_fourier_up)
    merged = jax.nn.sigmoid(g_attn) * a_branch + jax.nn.sigmoid(g_four) * f_branch
    mix = jnp.dot(merged, w_mix_out)
    x = layer_norm(DEEPNORM_ALPHA * x + gate_m * mix, ln1_g, ln1_b)

    h2 = layer_norm(x) * (1 + scale_f) + shift_f
    ffn = jnp.dot(jax.nn.silu(jnp.dot(h2, w_gate)) * jnp.dot(h2, w_up), w_down)
    x = layer_norm(DEEPNORM_ALPHA * x + gate_f * ffn, ln2_g, ln2_b)
    return x


def trunk(x, c, w_ada, b_ada, w_in, w_attn_up, w_fourier_up, w_mix_out,
          ln1_g, ln1_b, w_gate, w_up, w_down, ln2_g, ln2_b):
    for l in range(DEPTH):
        x = encoder_layer(x, c, w_ada[l], b_ada[l], w_in[l], w_attn_up[l], w_fourier_up[l], w_mix_out[l],
                          ln1_g[l], ln1_b[l], w_gate[l], w_up[l], w_down[l], ln2_g[l], ln2_b[l])
    return x


def setup_inputs(seed: int = 0) -> dict:
    key = jax.random.key(seed)
    ks = jax.random.split(key, 20)
    f32 = jnp.float32
    nrm = lambda k, shape, scale: jax.random.normal(k, shape, f32) * scale
    beta = DEEPNORM_BETA
    col_scale = jnp.ones((IN_WIDTH,), f32).at[2 * ATTN_WIDTH:3 * ATTN_WIDTH].set(beta)
    return {
        'x_prompt': nrm(ks[0], (BATCH, SEQ, D_MODEL), 1.0),
        'x_sample': nrm(ks[1], (DEC_BATCH, DEC_SEQ, D_MODEL), 1.0),
        'c_prompt': nrm(ks[2], (BATCH, D_MODEL), 1.0),
        'c_sample': nrm(ks[3], (DEC_BATCH, D_MODEL), 1.0),
        'w_ada': nrm(ks[4], (DEPTH, D_MODEL, N_MOD * D_MODEL), 0.5 * D_MODEL ** -0.5),
        'b_ada': nrm(ks[5], (DEPTH, N_MOD * D_MODEL), 0.02),
        'w_in': nrm(ks[6], (DEPTH, D_MODEL, IN_WIDTH), D_MODEL ** -0.5) * col_scale,
        'w_attn_up': nrm(ks[7], (DEPTH, ATTN_OUT_WIDTH, D_MODEL), beta * ATTN_OUT_WIDTH ** -0.5),
        'w_fourier_up': nrm(ks[8], (DEPTH, FOURIER_WIDTH, D_MODEL), beta * FOURIER_WIDTH ** -0.5),
        'w_mix_out': nrm(ks[9], (DEPTH, D_MODEL, D_MODEL), beta * D_MODEL ** -0.5),
        'ln1_g': 1.0 + nrm(ks[10], (DEPTH, D_MODEL), 0.02),
        'ln1_b': nrm(ks[11], (DEPTH, D_MODEL), 0.02),
        'w_gate': nrm(ks[12], (DEPTH, D_MODEL, D_FF), beta * D_MODEL ** -0.5),
        'w_up': nrm(ks[13], (DEPTH, D_MODEL, D_FF), beta * D_MODEL ** -0.5),
        'w_down': nrm(ks[14], (DEPTH, D_FF, D_MODEL), beta * D_FF ** -0.5),
        'ln2_g': 1.0 + nrm(ks[15], (DEPTH, D_MODEL), 0.02),
        'ln2_b': nrm(ks[16], (DEPTH, D_MODEL), 0.02),
    }


def reference(x_prompt, x_sample, c_prompt, c_sample, w_ada, b_ada, w_in, w_attn_up, w_fourier_up,
              w_mix_out, ln1_g, ln1_b, w_gate, w_up, w_down, ln2_g, ln2_b):
    y_prompt = trunk(x_prompt, c_prompt, w_ada, b_ada, w_in, w_attn_up, w_fourier_up, w_mix_out,
                     ln1_g, ln1_b, w_gate, w_up, w_down, ln2_g, ln2_b)
    y_sample = trunk(x_sample, c_sample, w_ada, b_ada, w_in, w_attn_up, w_fourier_up, w_mix_out,
                     ln1_g, ln1_b, w_gate, w_up, w_down, ln2_g, ln2_b)
    return (y_prompt, y_sample)
```

```python
import functools
import math

import numpy as np
import jax
import jax.numpy as jnp
from jax import lax
from jax.experimental import pallas as pl
from jax.experimental.pallas import tpu as pltpu

HEAD_DIM = 128
HEADS_PER_GROUP = 8
GROUP_WIDTH = HEADS_PER_GROUP * HEAD_DIM
DILATIONS = (1, 4, 16)
ATTN_RADIUS = 64
ATTN_WIDTH = len(DILATIONS) * GROUP_WIDTH
FOURIER_GROUPS = 4
DFT_ROWS = 128
N_MOD = 6
ROPE_THETA = 10000.0
LN_EPS = 1e-5
NEG_INF = -1e30
Q_SUB = 128
LSE_LANES = HEAD_DIM // HEADS_PER_GROUP

VMEM_LIMIT_BYTES = 56 * 1024 * 1024

F32 = jnp.float32
BF16 = jnp.bfloat16


def _params(*sem):
    return pltpu.CompilerParams(dimension_semantics=sem, vmem_limit_bytes=VMEM_LIMIT_BYTES)


def _tile(n, want):
    t = min(n, want)
    while n % t:
        t //= 2
    return t


def _ada_kernel(ct_ref, w_ref, b_ref, o_ref, sb_ref):
    kdim, tn = w_ref.shape
    nt = tn // 128

    @pl.when(pl.program_id(0) == 0)
    def _():
        c = ct_ref[...]
        s = c * jax.nn.sigmoid(c)
        sb_ref[0] = jnp.broadcast_to(s[:, 0:1], (kdim, 128))
        sb_ref[1] = jnp.broadcast_to(s[:, 1:2], (kdim, 128))

    def body(kc, acc):
        r0 = pl.multiple_of(kc * 8, 8)
        s0 = sb_ref[0, pl.ds(r0, 8), :]
        s1 = sb_ref[1, pl.ds(r0, 8), :]
        out = []
        for t in range(nt):
            w = w_ref[pl.ds(r0, 8), t * 128:(t + 1) * 128]
            out.append(acc[2 * t] + w * s0)
            out.append(acc[2 * t + 1] + w * s1)
        return tuple(out)

    zero = jnp.zeros((8, 128), F32)
    acc = lax.fori_loop(0, kdim // 8, body, (zero,) * (2 * nt), unroll=8)
    for t in range(nt):
        cols = slice(t * 128, (t + 1) * 128)
        o_ref[0:1, cols] = jnp.sum(acc[2 * t], axis=0, keepdims=True) + b_ref[:, cols]
        o_ref[1:2, cols] = jnp.sum(acc[2 * t + 1], axis=0, keepdims=True) + b_ref[:, cols]


def _ada_mod(c2, w, b):
    kdim, n = w.shape
    tn = _tile(n, 512)
    return pl.pallas_call(
        _ada_kernel,
        out_shape=jax.ShapeDtypeStruct((2, n), F32),
        grid=(n // tn,),
        in_specs=[pl.BlockSpec((kdim, 2), lambda j: (0, 0)),
                  pl.BlockSpec((kdim, tn), lambda j: (0, j)),
                  pl.BlockSpec((1, tn), lambda j: (0, j))],
        out_specs=pl.BlockSpec((2, tn), lambda j: (0, j)),
        scratch_shapes=[pltpu.VMEM((2, kdim, 128), F32)],
        compiler_params=_params("arbitrary"),
        name="ada_mod",
    )(c2.T, w, b.reshape(1, n))


def _normalize(x):
    mu = jnp.mean(x, axis=-1, keepdims=True)
    xc = x - mu
    var = jnp.mean(xc * xc, axis=-1, keepdims=True)
    return xc * lax.rsqrt(var + LN_EPS)


def _ln_mod_kernel(x_ref, mod_ref, o_ref):
    y = _normalize(x_ref[...])
    o_ref[...] = (y * (1.0 + mod_ref[1:2, :]) + mod_ref[0:1, :]).astype(o_ref.dtype)


def _ln_mod(x, mod3):
    s, d = x.shape
    tm = _tile(s, 256)
    return pl.pallas_call(
        _ln_mod_kernel,
        out_shape=jax.ShapeDtypeStruct((s, d), BF16),
        grid=(s // tm,),
        in_specs=[pl.BlockSpec((tm, d), lambda i: (i, 0)),
                  pl.BlockSpec((3, d), lambda i: (0, 0))],
        out_specs=pl.BlockSpec((tm, d), lambda i: (i, 0)),
        compiler_params=_params("parallel"),
        name="ln_mod",
    )(x, mod3)


def _proj_cast_kernel(x_ref, w_ref, o_ref):
    acc = jnp.dot(x_ref[...], w_ref[...], preferred_element_type=F32)
    o_ref[...] = acc.astype(o_ref.dtype)


def _proj_sigmoid_kernel(x_ref, w_ref, o_ref):
    acc = jnp.dot(x_ref[...], w_ref[...], preferred_element_type=F32)
    o_ref[...] = jax.nn.sigmoid(acc).astype(o_ref.dtype)


def _proj_rope_kernel(x_ref, w_ref, cos_ref, sin_ref, o_ref, *, scale):
    acc = jnp.dot(x_ref[...], w_ref[...], preferred_element_type=F32)
    cos = cos_ref[...] * scale
    sin = sin_ref[...] * scale
    for h in range(acc.shape[1] // HEAD_DIM):
        t = acc[:, h * HEAD_DIM:(h + 1) * HEAD_DIM]
        rot = pltpu.roll(t, HEAD_DIM // 2, 1)
        o_ref[:, h * HEAD_DIM:(h + 1) * HEAD_DIM] = (t * cos + rot * sin).astype(o_ref.dtype)


def _proj(h, w, body, extra=(), extra_specs=(), tm_want=512, tn_want=1024):
    s, kdim = h.shape
    n = w.shape[1]
    tm, tn = _tile(s, tm_want), _tile(n, tn_want)
    return pl.pallas_call(
        body,
        out_shape=jax.ShapeDtypeStruct((s, n), BF16),
        grid=(s // tm, n // tn),
        in_specs=[pl.BlockSpec((tm, kdim), lambda i, j: (i, 0)),
                  pl.BlockSpec((kdim, tn), lambda i, j: (0, j))] + [f(tm) for f in extra_specs],
        out_specs=pl.BlockSpec((tm, tn), lambda i, j: (i, j)),
        compiler_params=_params("parallel", "arbitrary"),
        name="in_proj",
    )(h, w, *extra)


def _rope_tables(s):
    half = HEAD_DIM // 2
    inv = ROPE_THETA ** (-jnp.arange(half, dtype=F32) / half)
    ang = jnp.arange(s).astype(F32)[:, None] * inv[None, :]
    cos, sin = jnp.cos(ang), jnp.sin(ang)
    return jnp.concatenate([cos, cos], axis=-1), jnp.concatenate([-sin, sin], axis=-1)


def _attn_kernel(*refs, tq, seq, has_prev, is_last):
    q_ref, kp_ref, kc_ref, kn_ref, vp_ref, vc_ref, vn_ref = refs[:7]
    pos = 7
    if has_prev:
        op_ref, lp_ref = refs[pos:pos + 2]
        pos += 2
    o_ref = refs[pos]
    pos += 1
    if not is_last:
        l_ref = refs[pos]
        pos += 1
    kw_ref, vw_ref = refs[pos:pos + 2]

    r = ATTN_RADIUS
    kw_ref[0:r, :] = kp_ref[...]
    kw_ref[r:r + tq, :] = kc_ref[...]
    kw_ref[r + tq:, :] = kn_ref[...]
    vw_ref[0:r, :] = vp_ref[...]
    vw_ref[r:r + tq, :] = vc_ref[...]
    vw_ref[r + tq:, :] = vn_ref[...]

    base = pl.program_id(1) * tq
    nkeys = Q_SUB + 2 * r
    qi = lax.broadcasted_iota(jnp.int32, (Q_SUB, nkeys), 0)
    kj = lax.broadcasted_iota(jnp.int32, (Q_SUB, nkeys), 1)
    lane_head = lax.broadcasted_iota(jnp.int32, (Q_SUB, HEAD_DIM), 1) // LSE_LANES

    for sb in range(tq // Q_SUB):
        rows = slice(sb * Q_SUB, (sb + 1) * Q_SUB)
        first = base + sb * Q_SUB - r
        lo = jnp.maximum(qi, -first)
        hi = jnp.minimum(qi + 2 * r, seq - 1 - first)
        keep = (kj >= lo) & (kj <= hi)
        lse_tile = jnp.zeros((Q_SUB, HEAD_DIM), F32)
        for h in range(HEADS_PER_GROUP):
            cols = slice(h * HEAD_DIM, (h + 1) * HEAD_DIM)
            qs = q_ref[rows, cols]
            ks = kw_ref[sb * Q_SUB:sb * Q_SUB + nkeys, cols]
            vs = vw_ref[sb * Q_SUB:sb * Q_SUB + nkeys, cols]
            sc = lax.dot_general(qs, ks, (((1,), (1,)), ((), ())), preferred_element_type=F32)
            sc = jnp.where(keep, sc, NEG_INF)
            m = jnp.max(sc, axis=-1, keepdims=True)
            p = jnp.exp(sc - m)
            den = jnp.sum(p, axis=-1, keepdims=True)
            o = jnp.dot(p.astype(BF16), vs, preferred_element_type=F32) / den
            lse = m + jnp.log(den)
            if has_prev:
                lse_p = lp_ref[rows, h * LSE_LANES:h * LSE_LANES + 1]
                top = jnp.maximum(lse_p, lse)
                tot = top + jnp.log(jnp.exp(lse_p - top) + jnp.exp(lse - top))
                o = op_ref[rows, cols] * jnp.exp(lse_p - tot) + o * jnp.exp(lse - tot)
                lse = tot
            o_ref[rows, cols] = o.astype(o_ref.dtype)
            if not is_last:
                lse_tile = jnp.where(lane_head == h, lse, lse_tile)
        if not is_last:
            l_ref[rows, :] = lse_tile


def _attn_group(q, k, v, gi, prev, is_last):
    s = q.shape[0]
    d = DILATIONS[gi]
    seq = s // d
    tq = _tile(seq, 512)
    r = ATTN_RADIUS
    ncol = ATTN_WIDTH // GROUP_WIDTH
    halo_per_tile = tq // r
    n_halo = seq // r

    def view(a, width):
        return a.reshape(seq, d * width)

    main = pl.BlockSpec((tq, GROUP_WIDTH), lambda rr, lb: (lb, rr * ncol + gi))
    before = pl.BlockSpec((r, GROUP_WIDTH),
                          lambda rr, lb: (jnp.maximum(lb * halo_per_tile - 1, 0), rr * ncol + gi))
    after = pl.BlockSpec((r, GROUP_WIDTH),
                         lambda rr, lb: (jnp.minimum((lb + 1) * halo_per_tile, n_halo - 1), rr * ncol + gi))
    o_spec = pl.BlockSpec((tq, GROUP_WIDTH), lambda rr, lb: (lb, rr))
    l_spec = pl.BlockSpec((tq, HEAD_DIM), lambda rr, lb: (lb, rr))

    qv, kv, vv = view(q, ATTN_WIDTH), view(k, ATTN_WIDTH), view(v, ATTN_WIDTH)
    args = [qv, kv, kv, kv, vv, vv, vv]
    in_specs = [main, before, main, after, before, main, after]
    if prev is not None:
        args += [view(prev[0], GROUP_WIDTH), view(prev[1], HEAD_DIM)]
        in_specs += [o_spec, l_spec]
    if is_last:
        out_shape = jax.ShapeDtypeStruct((seq, d * GROUP_WIDTH), BF16)
        out_specs = o_spec
    else:
        out_shape = (jax.ShapeDtypeStruct((seq, d * GROUP_WIDTH), F32),
                     jax.ShapeDtypeStruct((seq, d * HEAD_DIM), F32))
        out_specs = (o_spec, l_spec)
    out = pl.pallas_call(
        functools.partial(_attn_kernel, tq=tq, seq=seq, has_prev=prev is not None, is_last=is_last),
        out_shape=out_shape,
        grid=(d, seq // tq),
        in_specs=in_specs,
        out_specs=out_specs,
        scratch_shapes=[pltpu.VMEM((tq + 2 * r, GROUP_WIDTH), BF16),
                        pltpu.VMEM((tq + 2 * r, GROUP_WIDTH), BF16)],
        compiler_params=_params("parallel", "arbitrary"),
        name=f"dilated_attn_{d}",
    )(*args)
    if is_last:
        return out.reshape(s, GROUP_WIDTH)
    return out[0].reshape(s, GROUP_WIDTH), out[1].reshape(s, HEAD_DIM)


def _dft_tables(s, cg):
    n1, n2 = DFT_ROWS, s // DFT_ROWS

    def cs(rows, cols, period):
        ang = 2.0 * np.pi * ((np.arange(rows)[:, None] * np.arange(cols)[None, :]) % period) / period
        return np.cos(ang), np.sin(ang)

    c1, s1 = cs(n1, n1, n1)
    c2, s2 = cs(n2, n2, n2)
    ct, st = cs(n1, n2, s)
    cc, sc = cs(cg, cg, cg)
    norm = 1.0 / math.sqrt(s * cg)
    as_bf16 = lambda a: jnp.asarray(a, F32).astype(BF16)
    return dict(
        w1=as_bf16(np.concatenate([c1, -s1], axis=0)),
        tw_cos=jnp.asarray(ct, F32), tw_sin=jnp.asarray(st, F32),
        w2_re=as_bf16(np.concatenate([c2, -s2], axis=0)),
        w2_im=as_bf16(np.concatenate([s2, c2], axis=0)),
        wc_re=as_bf16(cc * norm), wc_im=as_bf16(sc * norm))


def _dft_stage1_kernel(x_ref, w1_ref, twc_ref, tws_ref, br_ref, bi_ref):
    n1 = x_ref.shape[0]
    n2 = pl.program_id(0)
    a = jnp.dot(w1_ref[...], x_ref[...], preferred_element_type=F32)
    ar, ai = a[:n1], a[n1:]
    lane = lax.broadcasted_iota(jnp.int32, twc_ref.shape, 1)
    c = jnp.sum(jnp.where(lane == n2, twc_ref[...], 0.0), axis=1, keepdims=True)
    sn = jnp.sum(jnp.where(lane == n2, tws_ref[...], 0.0), axis=1, keepdims=True)
    br_ref[...] = (ar * c + ai * sn).astype(br_ref.dtype)
    bi_ref[...] = (ai * c - ar * sn).astype(bi_ref.dtype)


def _dft_stage2_kernel(br_ref, bi_ref, w2r_ref, w2i_ref, wcr_ref, wci_ref, o_ref):
    n2 = br_ref.shape[1]
    z = (jnp.dot(w2r_ref[...], br_ref[0], preferred_element_type=F32)
         + jnp.dot(w2i_ref[...], bi_ref[0], preferred_element_type=F32))
    zr, zi = z[:n2].astype(BF16), z[n2:].astype(BF16)
    cg = wcr_ref.shape[0]
    for g in range(FOURIER_GROUPS):
        cols = slice(g * cg, (g + 1) * cg)
        out = (jnp.dot(zr[:, cols], wcr_ref[...], preferred_element_type=F32)
               + jnp.dot(zi[:, cols], wci_ref[...], preferred_element_type=F32))
        o_ref[:, cols] = out.astype(o_ref.dtype)


def _fourier_mix(f):
    s, width = f.shape
    n1, n2 = DFT_ROWS, s // DFT_ROWS
    cg = width // FOURIER_GROUPS
    t = _dft_tables(s, cg)
    full = lambda a: pl.BlockSpec(a.shape, lambda i: (0,) * a.ndim)
    br, bi = pl.pallas_call(
        _dft_stage1_kernel,
        out_shape=(jax.ShapeDtypeStruct((n1, n2 * width), BF16),) * 2,
        grid=(n2,),
        in_specs=[pl.BlockSpec((n1, width), lambda i: (0, i)),
                  full(t["w1"]), full(t["tw_cos"]), full(t["tw_sin"])],
        out_specs=(pl.BlockSpec((n1, width), lambda i: (0, i)),) * 2,
        compiler_params=_params("parallel"),
        name="dft_stage1",
    )(f.reshape(n1, n2 * width), t["w1"], t["tw_cos"], t["tw_sin"])
    out = pl.pallas_call(
        _dft_stage2_kernel,
        out_shape=jax.ShapeDtypeStruct((n2, n1 * width), BF16),
        grid=(n1,),
        in_specs=[pl.BlockSpec((1, n2, width), lambda i: (i, 0, 0)),
                  pl.BlockSpec((1, n2, width), lambda i: (i, 0, 0)),
                  full(t["w2_re"]), full(t["w2_im"]), full(t["wc_re"]), full(t["wc_im"])],
        out_specs=pl.BlockSpec((n2, width), lambda i: (0, i)),
        compiler_params=_params("parallel"),
        name="dft_stage2",
    )(br.reshape(n1, n2, width), bi.reshape(n1, n2, width),
      t["w2_re"], t["w2_im"], t["wc_re"], t["wc_im"])
    return out.reshape(s, width)


def _merge_kernel(a_ref, f_ref, wa_ref, wf_ref, ga_ref, gf_ref, o_ref):
    ab = jnp.dot(a_ref[...], wa_ref[...], preferred_element_type=F32)
    fb = jnp.dot(f_ref[...], wf_ref[...], preferred_element_type=F32)
    o_ref[...] = (ga_ref[...].astype(F32) * ab + gf_ref[...].astype(F32) * fb).astype(o_ref.dtype)


def _merge(attn, four, wa, wf, ga, gf):
    s, ka = attn.shape
    kf = four.shape[1]
    n = wa.shape[1]
    tm, tn = _tile(s, 512), _tile(n, 1024)
    return pl.pallas_call(
        _merge_kernel,
        out_shape=jax.ShapeDtypeStruct((s, n), BF16),
        grid=(s // tm, n // tn),
        in_specs=[pl.BlockSpec((tm, ka), lambda i, j: (i, 0)),
                  pl.BlockSpec((tm, kf), lambda i, j: (i, 0)),
                  pl.BlockSpec((ka, tn), lambda i, j: (0, j)),
                  pl.BlockSpec((kf, tn), lambda i, j: (0, j)),
                  pl.BlockSpec((tm, tn), lambda i, j: (i, j)),
                  pl.BlockSpec((tm, tn), lambda i, j: (i, j))],
        out_specs=pl.BlockSpec((tm, tn), lambda i, j: (i, j)),
        compiler_params=_params("parallel", "arbitrary"),
        name="branch_merge",
    )(attn, four, wa, wf, ga, gf)


LN_ROWS = 64
DOWN_COLS = 1024


def _down_ln_kernel(a_ref, w_ref, res_ref, mod_ref, g_ref, b_ref, y_ref, *h_ref, alpha, nk, gate_row):
    k = pl.program_id(1)
    n = y_ref.shape[1]
    nc = min(n, DOWN_COLS)
    for c0 in range(0, n, nc):
        cols = slice(c0, c0 + nc)
        part = jnp.dot(a_ref[...], w_ref[:, cols], preferred_element_type=F32)

        @pl.when(k == 0)
        def _():
            y_ref[:, cols] = part

        @pl.when(k > 0)
        def _():
            y_ref[:, cols] += part

    @pl.when(k == nk - 1)
    def _():
        gate = mod_ref[gate_row:gate_row + 1, :]

        def chunk(c, carry):
            rows = pl.ds(pl.multiple_of(c * LN_ROWS, LN_ROWS), LN_ROWS)
            t = alpha * res_ref[rows, :] + gate * y_ref[rows, :]
            y = _normalize(t) * g_ref[...] + b_ref[...]
            y_ref[rows, :] = y
            if h_ref:
                h2 = _normalize(y) * (1.0 + mod_ref[4:5, :]) + mod_ref[3:4, :]
                h_ref[0][rows, :] = h2.astype(BF16)
            return carry

        lax.fori_loop(0, y_ref.shape[0] // LN_ROWS, chunk, 0)


def _down_ln(a, w, res, mod6, g, b, alpha, gate_row, with_h2, tm_want=512, tk_want=512):
    s, kdim = a.shape
    n = w.shape[1]
    tm = _tile(s, tm_want)
    tk = _tile(kdim, tk_want)
    nk = kdim // tk
    row = pl.BlockSpec((tm, n), lambda i, k: (i, 0))
    vec = lambda rows: pl.BlockSpec((rows, n), lambda i, k: (0, 0))
    out_shape = [jax.ShapeDtypeStruct((s, n), F32)]
    out_specs = [row]
    if with_h2:
        out_shape.append(jax.ShapeDtypeStruct((s, n), BF16))
        out_specs.append(row)
    return pl.pallas_call(
        functools.partial(_down_ln_kernel, alpha=alpha, nk=nk, gate_row=gate_row),
        out_shape=tuple(out_shape),
        grid=(s // tm, nk),
        in_specs=[pl.BlockSpec((tm, tk), lambda i, k: (i, k)),
                  pl.BlockSpec((tk, n), lambda i, k: (k, 0)),
                  row, vec(N_MOD), vec(1), vec(1)],
        out_specs=tuple(out_specs),
        compiler_params=_params("parallel", "arbitrary"),
        name="down_ln",
    )(a, w, res, mod6, g.reshape(1, n), b.reshape(1, n))


def _swiglu_kernel(x_ref, w_ref, o_ref):
    acc = jnp.dot(x_ref[...], w_ref[...], preferred_element_type=F32)
    tn = o_ref.shape[1]
    gpre, up = acc[:, :tn], acc[:, tn:]
    o_ref[...] = (gpre * jax.nn.sigmoid(gpre) * up).astype(o_ref.dtype)


def _swiglu(h, wgu, tn):
    s, kdim = h.shape
    n = wgu.shape[1] // 2
    tm = _tile(s, 1024)
    return pl.pallas_call(
        _swiglu_kernel,
        out_shape=jax.ShapeDtypeStruct((s, n), BF16),
        grid=(s // tm, n // tn),
        in_specs=[pl.BlockSpec((tm, kdim), lambda i, j: (i, 0)),
                  pl.BlockSpec((kdim, 2 * tn), lambda i, j: (0, j))],
        out_specs=pl.BlockSpec((tm, tn), lambda i, j: (i, j)),
        compiler_params=_params("parallel", "arbitrary"),
        name="swiglu",
    )(h, wgu)


def _interleave_blocks(wg, wu, tn):
    kdim, n = wg.shape
    both = jnp.stack([wg.reshape(kdim, n // tn, tn), wu.reshape(kdim, n // tn, tn)], axis=2)
    return both.reshape(kdim, 2 * n).astype(BF16)


def _layer(x, mod6, wts, alpha):
    s, d = x.shape
    h = _ln_mod(x, mod6[0:3])

    cos, sin = _rope_tables(s)
    rope_specs = (lambda tm: pl.BlockSpec((tm, HEAD_DIM), lambda i, j: (i, 0)),) * 2
    rope = functools.partial(_proj_rope_kernel, scale=HEAD_DIM ** -0.5)
    q = _proj(h, wts["wq"], rope, (cos, sin), rope_specs)
    k = _proj(h, wts["wk"], functools.partial(_proj_rope_kernel, scale=1.0), (cos, sin), rope_specs)
    v = _proj(h, wts["wv"], _proj_cast_kernel)
    f = _proj(h, wts["wf"], _proj_cast_kernel)
    ga = _proj(h, wts["wga"], _proj_sigmoid_kernel)
    gf = _proj(h, wts["wgf"], _proj_sigmoid_kernel)

    state = None
    for gi in range(len(DILATIONS)):
        state = _attn_group(q, k, v, gi, state, gi == len(DILATIONS) - 1)
    attn = state

    four = _fourier_mix(f)
    merged = _merge(attn, four, wts["w_attn_up"], wts["w_fourier_up"], ga, gf)
    x1, h2 = _down_ln(merged, wts["w_mix_out"], x, mod6, wts["ln1_g"], wts["ln1_b"], alpha, 2, True)
    u = _swiglu(h2, wts["wgu"], wts["ff_tile"])
    (y,) = _down_ln(u, wts["w_down"], x1, mod6, wts["ln2_g"], wts["ln2_b"], alpha, 5, False)
    return y


def kernel(x_prompt, x_sample, c_prompt, c_sample, w_ada, b_ada, w_in, w_attn_up, w_fourier_up,
           w_mix_out, ln1_g, ln1_b, w_gate, w_up, w_down, ln2_g, ln2_b):
    depth = w_ada.shape[0]
    d = x_prompt.shape[-1]
    alpha = (2.0 * depth) ** 0.25
    fw = d // 4
    bounds = (0, ATTN_WIDTH, 2 * ATTN_WIDTH, 3 * ATTN_WIDTH, 3 * ATTN_WIDTH + fw,
              3 * ATTN_WIDTH + fw + d, 3 * ATTN_WIDTH + fw + 2 * d)
    xs = [x_prompt[0], x_sample[0]]
    c2 = jnp.concatenate([c_prompt, c_sample], axis=0)
    for l in range(depth):
        mod = _ada_mod(c2, w_ada[l], b_ada[l]).reshape(2, N_MOD, d)
        ff_tile = _tile(w_gate.shape[-1], 256)
        sect = [w_in[l][:, bounds[i]:bounds[i + 1]].astype(BF16) for i in range(6)]
        wts = dict(wq=sect[0], wk=sect[1], wv=sect[2], wf=sect[3], wga=sect[4], wgf=sect[5],
                   w_attn_up=w_attn_up[l].astype(BF16), w_fourier_up=w_fourier_up[l].astype(BF16),
                   w_mix_out=w_mix_out[l].astype(BF16), ln1_g=ln1_g[l], ln1_b=ln1_b[l],
                   wgu=_interleave_blocks(w_gate[l], w_up[l], ff_tile), ff_tile=ff_tile,
                   w_down=w_down[l].astype(BF16), ln2_g=ln2_g[l], ln2_b=ln2_b[l])
        xs = [_layer(xs[g], mod[g], wts, alpha) for g in range(2)]
    return (xs[0][None], xs[1][None])
```

```python
import functools
import math

import numpy as np
import jax
import jax.numpy as jnp
from jax import lax
from jax.experimental import pallas as pl
from jax.experimental.pallas import tpu as pltpu

HEAD_DIM = 128
HEADS_PER_GROUP = 8
GROUP_WIDTH = HEADS_PER_GROUP * HEAD_DIM
DILATIONS = (1, 4, 16)
ATTN_RADIUS = 64
ATTN_WIDTH = len(DILATIONS) * GROUP_WIDTH
FOURIER_GROUPS = 4
DFT_ROWS = 128
N_MOD = 6
ROPE_THETA = 10000.0
LN_EPS = 1e-5
NEG_INF = -1e30
Q_SUB = 128
LSE_LANES = HEAD_DIM // HEADS_PER_GROUP

VMEM_LIMIT_BYTES = 56 * 1024 * 1024

F32 = jnp.float32
BF16 = jnp.bfloat16


def _params(*sem):
    return pltpu.CompilerParams(dimension_semantics=sem, vmem_limit_bytes=VMEM_LIMIT_BYTES)


def _tile(n, want):
    t = min(n, want)
    while n % t:
        t //= 2
    return t


def _ada_kernel(ct_ref, w_ref, b_ref, o_ref, sb_ref):
    kdim, tn = w_ref.shape
    nt = tn // 128

    @pl.when(pl.program_id(0) == 0)
    def _():
        c = ct_ref[...]
        s = c * jax.nn.sigmoid(c)
        sb_ref[0] = jnp.broadcast_to(s[:, 0:1], (kdim, 128))
        sb_ref[1] = jnp.broadcast_to(s[:, 1:2], (kdim, 128))

    def body(kc, acc):
        r0 = pl.multiple_of(kc * 8, 8)
        s0 = sb_ref[0, pl.ds(r0, 8), :]
        s1 = sb_ref[1, pl.ds(r0, 8), :]
        out = []
        for t in range(nt):
            w = w_ref[pl.ds(r0, 8), t * 128:(t + 1) * 128]
            out.append(acc[2 * t] + w * s0)
            out.append(acc[2 * t + 1] + w * s1)
        return tuple(out)

    zero = jnp.zeros((8, 128), F32)
    acc = lax.fori_loop(0, kdim // 8, body, (zero,) * (2 * nt), unroll=8)
    for t in range(nt):
        cols = slice(t * 128, (t + 1) * 128)
        o_ref[0:1, cols] = jnp.sum(acc[2 * t], axis=0, keepdims=True) + b_ref[:, cols]
        o_ref[1:2, cols] = jnp.sum(acc[2 * t + 1], axis=0, keepdims=True) + b_ref[:, cols]


def _ada_mod(c2, w, b):
    kdim, n = w.shape
    tn = _tile(n, 512)
    return pl.pallas_call(
        _ada_kernel,
        out_shape=jax.ShapeDtypeStruct((2, n), F32),
        grid=(n // tn,),
        in_specs=[pl.BlockSpec((kdim, 2), lambda j: (0, 0)),
                  pl.BlockSpec((kdim, tn), lambda j: (0, j)),
                  pl.BlockSpec((1, tn), lambda j: (0, j))],
        out_specs=pl.BlockSpec((2, tn), lambda j: (0, j)),
        scratch_shapes=[pltpu.VMEM((2, kdim, 128), F32)],
        compiler_params=_params("arbitrary"),
        name="ada_mod",
    )(c2.T, w, b.reshape(1, n))


def _normalize(x):
    mu = jnp.mean(x, axis=-1, keepdims=True)
    xc = x - mu
    var = jnp.mean(xc * xc, axis=-1, keepdims=True)
    return xc * lax.rsqrt(var + LN_EPS)


def _ln_mod_kernel(x_ref, mod_ref, o_ref):
    y = _normalize(x_ref[...])
    o_ref[...] = (y * (1.0 + mod_ref[1:2, :]) + mod_ref[0:1, :]).astype(o_ref.dtype)


def _ln_mod(x, mod3):
    s, d = x.shape
    tm = _tile(s, 256)
    return pl.pallas_call(
        _ln_mod_kernel,
        out_shape=jax.ShapeDtypeStruct((s, d), BF16),
        grid=(s // tm,),
        in_specs=[pl.BlockSpec((tm, d), lambda i: (i, 0)),
                  pl.BlockSpec((3, d), lambda i: (0, 0))],
        out_specs=pl.BlockSpec((tm, d), lambda i: (i, 0)),
        compiler_params=_params("parallel"),
        name="ln_mod",
    )(x, mod3)


def _proj_cast_kernel(x_ref, w_ref, o_ref):
    acc = jnp.dot(x_ref[...], w_ref[...], preferred_element_type=F32)
    o_ref[...] = acc.astype(o_ref.dtype)


def _proj_sigmoid_kernel(x_ref, w_ref, o_ref):
    acc = jnp.dot(x_ref[...], w_ref[...], preferred_element_type=F32)
    o_ref[...] = jax.nn.sigmoid(acc).astype(o_ref.dtype)


def _proj_rope_kernel(x_ref, w_ref, cos_ref, sin_ref, o_ref, *, scale):
    acc = jnp.dot(x_ref[...], w_ref[...], preferred_element_type=F32)
    cos = cos_ref[...] * scale
    sin = sin_ref[...] * scale
    for h in range(acc.shape[1] // HEAD_DIM):
        t = acc[:, h * HEAD_DIM:(h + 1) * HEAD_DIM]
        rot = pltpu.roll(t, HEAD_DIM // 2, 1)
        o_ref[:, h * HEAD_DIM:(h + 1) * HEAD_DIM] = (t * cos + rot * sin).astype(o_ref.dtype)


def _proj(h, w, body, extra=(), extra_specs=(), tm_want=512, tn_want=1024):
    s, kdim = h.shape
    n = w.shape[1]
    tm, tn = _tile(s, tm_want), _tile(n, tn_want)
    return pl.pallas_call(
        body,
        out_shape=jax.ShapeDtypeStruct((s, n), BF16),
        grid=(s // tm, n // tn),
        in_specs=[pl.BlockSpec((tm, kdim), lambda i, j: (i, 0)),
                  pl.BlockSpec((kdim, tn), lambda i, j: (0, j))] + [f(tm) for f in extra_specs],
        out_specs=pl.BlockSpec((tm, tn), lambda i, j: (i, j)),
        compiler_params=_params("parallel", "arbitrary"),
        name="in_proj",
    )(h, w, *extra)


def _rope_tables(s):
    half = HEAD_DIM // 2
    inv = ROPE_THETA ** (-jnp.arange(half, dtype=F32) / half)
    ang = jnp.arange(s).astype(F32)[:, None] * inv[None, :]
    cos, sin = jnp.cos(ang), jnp.sin(ang)
    return jnp.concatenate([cos, cos], axis=-1), jnp.concatenate([-sin, sin], axis=-1)


def _attn_kernel(*refs, tq, seq, has_prev, is_last):
    q_ref, kp_ref, kc_ref, kn_ref, vp_ref, vc_ref, vn_ref = refs[:7]
    pos = 7
    if has_prev:
        op_ref, lp_ref = refs[pos:pos + 2]
        pos += 2
    o_ref = refs[pos]
    pos += 1
    if not is_last:
        l_ref = refs[pos]
        pos += 1
    kw_ref, vw_ref = refs[pos:pos + 2]

    r = ATTN_RADIUS
    kw_ref[0:r, :] = kp_ref[...]
    kw_ref[r:r + tq, :] = kc_ref[...]
    kw_ref[r + tq:, :] = kn_ref[...]
    vw_ref[0:r, :] = vp_ref[...]
    vw_ref[r:r + tq, :] = vc_ref[...]
    vw_ref[r + tq:, :] = vn_ref[...]

    base = pl.program_id(1) * tq
    nkeys = Q_SUB + 2 * r
    qi = lax.broadcasted_iota(jnp.int32, (Q_SUB, nkeys), 0)
    kj = lax.broadcasted_iota(jnp.int32, (Q_SUB, nkeys), 1)
    lane_head = lax.broadcasted_iota(jnp.int32, (Q_SUB, HEAD_DIM), 1) // LSE_LANES

    for sb in range(tq // Q_SUB):
        rows = slice(sb * Q_SUB, (sb + 1) * Q_SUB)
        first = base + sb * Q_SUB - r
        lo = jnp.maximum(qi, -first)
        hi = jnp.minimum(qi + 2 * r, seq - 1 - first)
        keep = (kj >= lo) & (kj <= hi)
        lse_tile = jnp.zeros((Q_SUB, HEAD_DIM), F32)
        for h in range(HEADS_PER_GROUP):
            cols = slice(h * HEAD_DIM, (h + 1) * HEAD_DIM)
            qs = q_ref[rows, cols]
            ks = kw_ref[sb * Q_SUB:sb * Q_SUB + nkeys, cols]
            vs = vw_ref[sb * Q_SUB:sb * Q_SUB + nkeys, cols]
            sc = lax.dot_general(qs, ks, (((1,), (1,)), ((), ())), preferred_element_type=F32)
            sc = jnp.where(keep, sc, NEG_INF)
            m = jnp.max(sc, axis=-1, keepdims=True)
            p = jnp.exp(sc - m)
            den = jnp.sum(p, axis=-1, keepdims=True)
            o = jnp.dot(p.astype(BF16), vs, preferred_element_type=F32) / den
            lse = m + jnp.log(den)
            if has_prev:
                lse_p = lp_ref[rows, h * LSE_LANES:h * LSE_LANES + 1]
                top = jnp.maximum(lse_p, lse)
                tot = top + jnp.log(jnp.exp(lse_p - top) + jnp.exp(lse - top))
                o = op_ref[rows, cols] * jnp.exp(lse_p - tot) + o * jnp.exp(lse - tot)
                lse = tot
            o_ref[rows, cols] = o.astype(o_ref.dtype)
            if not is_last:
                lse_tile = jnp.where(lane_head == h, lse, lse_tile)
        if not is_last:
            l_ref[rows, :] = lse_tile


def _attn_group(q, k, v, gi, prev, is_last):
    s = q.shape[0]
    d = DILATIONS[gi]
    seq = s // d
    tq = _tile(seq, 512)
    r = ATTN_RADIUS
    ncol = ATTN_WIDTH // GROUP_WIDTH
    halo_per_tile = tq // r
    n_halo = seq // r

    def view(a, width):
        return a.reshape(seq, d * width)

    main = pl.BlockSpec((tq, GROUP_WIDTH), lambda rr, lb: (lb, rr * ncol + gi))
    before = pl.BlockSpec((r, GROUP_WIDTH),
                          lambda rr, lb: (jnp.maximum(lb * halo_per_tile - 1, 0), rr * ncol + gi))
    after = pl.BlockSpec((r, GROUP_WIDTH),
                         lambda rr, lb: (jnp.minimum((lb + 1) * halo_per_tile, n_halo - 1), rr * ncol + gi))
    o_spec = pl.BlockSpec((tq, GROUP_WIDTH), lambda rr, lb: (lb, rr))
    l_spec = pl.BlockSpec((tq, HEAD_DIM), lambda rr, lb: (lb, rr))

    qv, kv, vv = view(q, ATTN_WIDTH), view(k, ATTN_WIDTH), view(v, ATTN_WIDTH)
    args = [qv, kv, kv, kv, vv, vv, vv]
    in_specs = [main, before, main, after, before, main, after]
    if prev is not None:
        args += [view(prev[0], GROUP_WIDTH), view(prev[1], HEAD_DIM)]
        in_specs += [o_spec, l_spec]
    if is_last:
        out_shape = jax.ShapeDtypeStruct((seq, d * GROUP_WIDTH), BF16)
        out_specs = o_spec
    else:
        out_shape = (jax.ShapeDtypeStruct((seq, d * GROUP_WIDTH), F32),
                     jax.ShapeDtypeStruct((seq, d * HEAD_DIM), F32))
        out_specs = (o_spec, l_spec)
    out = pl.pallas_call(
        functools.partial(_attn_kernel, tq=tq, seq=seq, has_prev=prev is not None, is_last=is_last),
        out_shape=out_shape,
        grid=(d, seq // tq),
        in_specs=in_specs,
        out_specs=out_specs,
        scratch_shapes=[pltpu.VMEM((tq + 2 * r, GROUP_WIDTH), BF16),
                        pltpu.VMEM((tq + 2 * r, GROUP_WIDTH), BF16)],
        compiler_params=_params("parallel", "arbitrary"),
        name=f"dilated_attn_{d}",
    )(*args)
    if is_last:
        return out.reshape(s, GROUP_WIDTH)
    return out[0].reshape(s, GROUP_WIDTH), out[1].reshape(s, HEAD_DIM)


def _dft_tables(s, cg):
    n1, n2 = DFT_ROWS, s // DFT_ROWS

    def cs(rows, cols, period):
        ang = 2.0 * np.pi * ((np.arange(rows)[:, None] * np.arange(cols)[None, :]) % period) / period
        return np.cos(ang), np.sin(ang)

    c1, s1 = cs(n1, n1, n1)
    c2, s2 = cs(n2, n2, n2)
    ct, st = cs(n1, n2, s)
    cc, sc = cs(cg, cg, cg)
    norm = 1.0 / math.sqrt(s * cg)
    as_bf16 = lambda a: jnp.asarray(a, F32).astype(BF16)
    return dict(
        w1=as_bf16(np.concatenate([c1, -s1], axis=0)),
        tw_cos=jnp.asarray(ct, F32), tw_sin=jnp.asarray(st, F32),
        w2_re=as_bf16(np.concatenate([c2, -s2], axis=0)),
        w2_im=as_bf16(np.concatenate([s2, c2], axis=0)),
        wc_re=as_bf16(cc * norm), wc_im=as_bf16(sc * norm))


def _dft_stage1_kernel(x_ref, w1_ref, twc_ref, tws_ref, br_ref, bi_ref):
    n1 = x_ref.shape[0]
    n2 = pl.program_id(0)
    a = jnp.dot(w1_ref[...], x_ref[...], preferred_element_type=F32)
    ar, ai = a[:n1], a[n1:]
    lane = lax.broadcasted_iota(jnp.int32, twc_ref.shape, 1)
    c = jnp.sum(jnp.where(lane == n2, twc_ref[...], 0.0), axis=1, keepdims=True)
    sn = jnp.sum(jnp.where(lane == n2, tws_ref[...], 0.0), axis=1, keepdims=True)
    br_ref[...] = (ar * c + ai * sn).astype(br_ref.dtype)
    bi_ref[...] = (ai * c - ar * sn).astype(bi_ref.dtype)


def _dft_stage2_kernel(br_ref, bi_ref, w2r_ref, w2i_ref, wcr_ref, wci_ref, o_ref):
    n2 = br_ref.shape[1]
    z = (jnp.dot(w2r_ref[...], br_ref[0], preferred_element_type=F32)
         + jnp.dot(w2i_ref[...], bi_ref[0], preferred_element_type=F32))
    zr, zi = z[:n2].astype(BF16), z[n2:].astype(BF16)
    cg = wcr_ref.shape[0]
    for g in range(FOURIER_GROUPS):
        cols = slice(g * cg, (g + 1) * cg)
        out = (jnp.dot(zr[:, cols], wcr_ref[...], preferred_element_type=F32)
               + jnp.dot(zi[:, cols], wci_ref[...], preferred_element_type=F32))
        o_ref[:, cols] = out.astype(o_ref.dtype)


def _fourier_mix(f):
    s, width = f.shape
    n1, n2 = DFT_ROWS, s // DFT_ROWS
    cg = width // FOURIER_GROUPS
    t = _dft_tables(s, cg)
    full = lambda a: pl.BlockSpec(a.shape, lambda i: (0,) * a.ndim)
    br, bi = pl.pallas_call(
        _dft_stage1_kernel,
        out_shape=(jax.ShapeDtypeStruct((n1, n2 * width), BF16),) * 2,
        grid=(n2,),
        in_specs=[pl.BlockSpec((n1, width), lambda i: (0, i)),
                  full(t["w1"]), full(t["tw_cos"]), full(t["tw_sin"])],
        out_specs=(pl.BlockSpec((n1, width), lambda i: (0, i)),) * 2,
        compiler_params=_params("parallel"),
        name="dft_stage1",
    )(f.reshape(n1, n2 * width), t["w1"], t["tw_cos"], t["tw_sin"])
    out = pl.pallas_call(
        _dft_stage2_kernel,
        out_shape=jax.ShapeDtypeStruct((n2, n1 * width), BF16),
        grid=(n1,),
        in_specs=[pl.BlockSpec((1, n2, width), lambda i: (i, 0, 0)),
                  pl.BlockSpec((1, n2, width), lambda i: (i, 0, 0)),
                  full(t["w2_re"]), full(t["w2_im"]), full(t["wc_re"]), full(t["wc_im"])],
        out_specs=pl.BlockSpec((n2, width), lambda i: (0, i)),
        compiler_params=_params("parallel"),
        name="dft_stage2",
    )(br.reshape(n1, n2, width), bi.reshape(n1, n2, width),
      t["w2_re"], t["w2_im"], t["wc_re"], t["wc_im"])
    return out.reshape(s, width)


def _merge_kernel(a_ref, f_ref, wa_ref, wf_ref, ga_ref, gf_ref, o_ref):
    ab = jnp.dot(a_ref[...], wa_ref[...], preferred_element_type=F32)
    fb = jnp.dot(f_ref[...], wf_ref[...], preferred_element_type=F32)
    o_ref[...] = (ga_ref[...].astype(F32) * ab + gf_ref[...].astype(F32) * fb).astype(o_ref.dtype)


def _merge(attn, four, wa, wf, ga, gf):
    s, ka = attn.shape
    kf = four.shape[1]
    n = wa.shape[1]
    tm, tn = _tile(s, 512), _tile(n, 1024)
    return pl.pallas_call(
        _merge_kernel,
        out_shape=jax.ShapeDtypeStruct((s, n), BF16),
        grid=(s // tm, n // tn),
        in_specs=[pl.BlockSpec((tm, ka), lambda i, j: (i, 0)),
                  pl.BlockSpec((tm, kf), lambda i, j: (i, 0)),
                  pl.BlockSpec((ka, tn), lambda i, j: (0, j)),
                  pl.BlockSpec((kf, tn), lambda i, j: (0, j)),
                  pl.BlockSpec((tm, tn), lambda i, j: (i, j)),
                  pl.BlockSpec((tm, tn), lambda i, j: (i, j))],
        out_specs=pl.BlockSpec((tm, tn), lambda i, j: (i, j)),
        compiler_params=_params("parallel", "arbitrary"),
        name="branch_merge",
    )(attn, four, wa, wf, ga, gf)


def _proj_res_kernel(a_ref, w_ref, res_ref, gate_ref, o_ref, *, alpha):
    acc = jnp.dot(a_ref[...], w_ref[...], preferred_element_type=F32)
    o_ref[...] = alpha * res_ref[...] + gate_ref[...] * acc


def _proj_res(a, w, res, gate, alpha, tm_want, tn_want):
    s, kdim = a.shape
    n = w.shape[1]
    tm, tn = _tile(s, tm_want), _tile(n, tn_want)
    return pl.pallas_call(
        functools.partial(_proj_res_kernel, alpha=alpha),
        out_shape=jax.ShapeDtypeStruct((s, n), F32),
        grid=(s // tm, n // tn),
        in_specs=[pl.BlockSpec((tm, kdim), lambda i, j: (i, 0)),
                  pl.BlockSpec((kdim, tn), lambda i, j: (0, j)),
                  pl.BlockSpec((tm, tn), lambda i, j: (i, j)),
                  pl.BlockSpec((1, tn), lambda i, j: (0, j))],
        out_specs=pl.BlockSpec((tm, tn), lambda i, j: (i, j)),
        compiler_params=_params("parallel", "arbitrary"),
        name="proj_res",
    )(a, w, res, gate)


def _ln_out_kernel(t_ref, g_ref, b_ref, *refs):
    y = _normalize(t_ref[...]) * g_ref[...] + b_ref[...]
    if len(refs) == 1:
        refs[0][...] = y
    else:
        mod_ref, y_ref, h_ref = refs
        y_ref[...] = y
        h_ref[...] = (_normalize(y) * (1.0 + mod_ref[1:2, :]) + mod_ref[0:1, :]).astype(h_ref.dtype)


def _ln_out(t, g, b, mod3=None):
    s, d = t.shape
    tm = _tile(s, 256)
    row = pl.BlockSpec((tm, d), lambda i: (i, 0))
    vec = lambda rows: pl.BlockSpec((rows, d), lambda i: (0, 0))
    args, in_specs = [t, g.reshape(1, d), b.reshape(1, d)], [row, vec(1), vec(1)]
    out_shape, out_specs = [jax.ShapeDtypeStruct((s, d), F32)], [row]
    if mod3 is not None:
        args.append(mod3)
        in_specs.append(vec(3))
        out_shape.append(jax.ShapeDtypeStruct((s, d), BF16))
        out_specs.append(row)
    return pl.pallas_call(
        _ln_out_kernel,
        out_shape=tuple(out_shape),
        grid=(s // tm,),
        in_specs=in_specs,
        out_specs=tuple(out_specs),
        compiler_params=_params("parallel"),
        name="ln_out",
    )(*args)


def _swiglu_kernel(x_ref, wg_ref, wu_ref, o_ref):
    x = x_ref[...]
    gpre = jnp.dot(x, wg_ref[...], preferred_element_type=F32)
    up = jnp.dot(x, wu_ref[...], preferred_element_type=F32)
    o_ref[...] = (gpre * jax.nn.sigmoid(gpre) * up).astype(o_ref.dtype)


def _swiglu(h, wg, wu):
    s, kdim = h.shape
    n = wg.shape[1]
    tm, tn = _tile(s, 1024), _tile(n, 256)
    return pl.pallas_call(
        _swiglu_kernel,
        out_shape=jax.ShapeDtypeStruct((s, n), BF16),
        grid=(s // tm, n // tn),
        in_specs=[pl.BlockSpec((tm, kdim), lambda i, j: (i, 0)),
                  pl.BlockSpec((kdim, tn), lambda i, j: (0, j)),
                  pl.BlockSpec((kdim, tn), lambda i, j: (0, j))],
        out_specs=pl.BlockSpec((tm, tn), lambda i, j: (i, j)),
        compiler_params=_params("parallel", "arbitrary"),
        name="swiglu",
    )(h, wg, wu)


def _layer(x, mod6, wts, alpha):
    s, d = x.shape
    h = _ln_mod(x, mod6[0:3])

    cos, sin = _rope_tables(s)
    rope_specs = (lambda tm: pl.BlockSpec((tm, HEAD_DIM), lambda i, j: (i, 0)),) * 2
    rope = functools.partial(_proj_rope_kernel, scale=HEAD_DIM ** -0.5)
    q = _proj(h, wts["wq"], rope, (cos, sin), rope_specs)
    k = _proj(h, wts["wk"], functools.partial(_proj_rope_kernel, scale=1.0), (cos, sin), rope_specs)
    v = _proj(h, wts["wv"], _proj_cast_kernel)
    f = _proj(h, wts["wf"], _proj_cast_kernel)
    ga = _proj(h, wts["wga"], _proj_sigmoid_kernel)
    gf = _proj(h, wts["wgf"], _proj_sigmoid_kernel)

    state = None
    for gi in range(len(DILATIONS)):
        state = _attn_group(q, k, v, gi, state, gi == len(DILATIONS) - 1)
    attn = state

    four = _fourier_mix(f)
    merged = _merge(attn, four, wts["w_attn_up"], wts["w_fourier_up"], ga, gf)
    t1 = _proj_res(merged, wts["w_mix_out"], x, mod6[2:3], alpha, 1024, 512)
    x1, h2 = _ln_out(t1, wts["ln1_g"], wts["ln1_b"], mod6[3:6])
    u = _swiglu(h2, wts["w_gate"], wts["w_up"])
    t2 = _proj_res(u, wts["w_down"], x1, mod6[5:6], alpha, 512, 512)
    (y,) = _ln_out(t2, wts["ln2_g"], wts["ln2_b"])
    return y


def kernel(x_prompt, x_sample, c_prompt, c_sample, w_ada, b_ada, w_in, w_attn_up, w_fourier_up,
           w_mix_out, ln1_g, ln1_b, w_gate, w_up, w_down, ln2_g, ln2_b):
    depth = w_ada.shape[0]
    d = x_prompt.shape[-1]
    alpha = (2.0 * depth) ** 0.25
    fw = d // 4
    bounds = (0, ATTN_WIDTH, 2 * ATTN_WIDTH, 3 * ATTN_WIDTH, 3 * ATTN_WIDTH + fw,
              3 * ATTN_WIDTH + fw + d, 3 * ATTN_WIDTH + fw + 2 * d)
    xs = [x_prompt[0], x_sample[0]]
    c2 = jnp.concatenate([c_prompt, c_sample], axis=0)
    for l in range(depth):
        mod = _ada_mod(c2, w_ada[l], b_ada[l]).reshape(2, N_MOD, d)
        sect = [w_in[l][:, bounds[i]:bounds[i + 1]].astype(BF16) for i in range(6)]
        wts = dict(wq=sect[0], wk=sect[1], wv=sect[2], wf=sect[3], wga=sect[4], wgf=sect[5],
                   w_attn_up=w_attn_up[l].astype(BF16), w_fourier_up=w_fourier_up[l].astype(BF16),
                   w_mix_out=w_mix_out[l].astype(BF16), ln1_g=ln1_g[l], ln1_b=ln1_b[l],
                   w_gate=w_gate[l].astype(BF16), w_up=w_up[l].astype(BF16),
                   w_down=w_down[l].astype(BF16), ln2_g=ln2_g[l], ln2_b=ln2_b[l])
        xs = [_layer(xs[g], mod[g], wts, alpha) for g in range(2)]
    return (xs[0][None], xs[1][None])
```

```python
import functools
import math

import numpy as np
import jax
import jax.numpy as jnp
from jax import lax
from jax.experimental import pallas as pl
from jax.experimental.pallas import tpu as pltpu

HEAD_DIM = 128
HEADS_PER_GROUP = 8
GROUP_WIDTH = HEADS_PER_GROUP * HEAD_DIM
DILATIONS = (1, 4, 16)
ATTN_RADIUS = 64
ATTN_WIDTH = len(DILATIONS) * GROUP_WIDTH
FOURIER_GROUPS = 4
DFT_ROWS = 128
DFT_SLAB = 8
N_MOD = 6
ROPE_THETA = 10000.0
LN_EPS = 1e-5
NEG_INF = -1e30
Q_SUB = 128
LSE_LANES = HEAD_DIM // HEADS_PER_GROUP

VMEM_LIMIT_BYTES = 56 * 1024 * 1024

F32 = jnp.float32
BF16 = jnp.bfloat16


def _params(*sem):
    return pltpu.CompilerParams(dimension_semantics=sem, vmem_limit_bytes=VMEM_LIMIT_BYTES)


def _tile(n, want):
    t = min(n, want)
    while n % t:
        t //= 2
    return t


def _head_cols(h):
    return slice(h * HEAD_DIM, (h + 1) * HEAD_DIM)


def _ada_kernel(ct_ref, w_ref, b_ref, o_ref, sb_ref):
    kdim, tn = w_ref.shape
    nt = tn // 128

    @pl.when(pl.program_id(0) == 0)
    def _():
        c = ct_ref[...]
        s = c * jax.nn.sigmoid(c)
        sb_ref[0] = jnp.broadcast_to(s[:, 0:1], (kdim, 128))
        sb_ref[1] = jnp.broadcast_to(s[:, 1:2], (kdim, 128))

    def body(kc, acc):
        r0 = pl.multiple_of(kc * 8, 8)
        s0 = sb_ref[0, pl.ds(r0, 8), :]
        s1 = sb_ref[1, pl.ds(r0, 8), :]
        out = []
        for t in range(nt):
            w = w_ref[pl.ds(r0, 8), t * 128:(t + 1) * 128]
            out.append(acc[2 * t] + w * s0)
            out.append(acc[2 * t + 1] + w * s1)
        return tuple(out)

    zero = jnp.zeros((8, 128), F32)
    acc = lax.fori_loop(0, kdim // 8, body, (zero,) * (2 * nt), unroll=8)
    for t in range(nt):
        cols = slice(t * 128, (t + 1) * 128)
        o_ref[0:1, cols] = jnp.sum(acc[2 * t], axis=0, keepdims=True) + b_ref[:, cols]
        o_ref[1:2, cols] = jnp.sum(acc[2 * t + 1], axis=0, keepdims=True) + b_ref[:, cols]


def _ada_mod(c2, w, b):
    kdim, n = w.shape
    tn = _tile(n, 512)
    return pl.pallas_call(
        _ada_kernel,
        out_shape=jax.ShapeDtypeStruct((2, n), F32),
        grid=(n // tn,),
        in_specs=[pl.BlockSpec((kdim, 2), lambda j: (0, 0)),
                  pl.BlockSpec((kdim, tn), lambda j: (0, j)),
                  pl.BlockSpec((1, tn), lambda j: (0, j))],
        out_specs=pl.BlockSpec((2, tn), lambda j: (0, j)),
        scratch_shapes=[pltpu.VMEM((2, kdim, 128), F32)],
        compiler_params=_params("arbitrary"),
        name="ada_mod",
    )(c2.T, w, b.reshape(1, n))


def _normalize(x):
    mu = jnp.mean(x, axis=-1, keepdims=True)
    xc = x - mu
    var = jnp.mean(xc * xc, axis=-1, keepdims=True)
    return xc * lax.rsqrt(var + LN_EPS)


def _ln_mod_kernel(x_ref, mod_ref, o_ref):
    y = _normalize(x_ref[...])
    o_ref[...] = (y * (1.0 + mod_ref[1:2, :]) + mod_ref[0:1, :]).astype(o_ref.dtype)


def _ln_mod(x, mod3):
    s, d = x.shape
    tm = _tile(s, 256)
    return pl.pallas_call(
        _ln_mod_kernel,
        out_shape=jax.ShapeDtypeStruct((s, d), BF16),
        grid=(s // tm,),
        in_specs=[pl.BlockSpec((tm, d), lambda i: (i, 0)),
                  pl.BlockSpec((3, d), lambda i: (0, 0))],
        out_specs=pl.BlockSpec((tm, d), lambda i: (i, 0)),
        compiler_params=_params("parallel"),
        name="ln_mod",
    )(x, mod3)


def _proj_cast_kernel(x_ref, w_ref, o_ref):
    acc = jnp.dot(x_ref[...], w_ref[...], preferred_element_type=F32)
    o_ref[...] = acc.astype(o_ref.dtype)


def _proj_sigmoid_kernel(x_ref, w_ref, o_ref):
    acc = jnp.dot(x_ref[...], w_ref[...], preferred_element_type=F32)
    o_ref[...] = jax.nn.sigmoid(acc).astype(o_ref.dtype)


def _proj(h, w, body, out_dtype, tm_want=512, tn_want=1024):
    s, kdim = h.shape
    n = w.shape[1]
    tm, tn = _tile(s, tm_want), _tile(n, tn_want)
    return pl.pallas_call(
        body,
        out_shape=jax.ShapeDtypeStruct((s, n), out_dtype),
        grid=(s // tm, n // tn),
        in_specs=[pl.BlockSpec((tm, kdim), lambda i, j: (i, 0)),
                  pl.BlockSpec((kdim, tn), lambda i, j: (0, j))],
        out_specs=pl.BlockSpec((tm, tn), lambda i, j: (i, j)),
        compiler_params=_params("parallel", "arbitrary"),
        name="in_proj",
    )(h, w)


def _qkv_kernel(x_ref, w_ref, cos_ref, sin_ref, o_ref, *scratch, dil, q_scale):
    j = pl.program_id(1)
    acc = jnp.dot(x_ref[...], w_ref[...], preferred_element_type=F32)
    tm = acc.shape[0]

    def put(h, t):
        if dil == 1:
            o_ref[0, :, _head_cols(h)] = t.astype(o_ref.dtype)
        else:
            scratch[0][h] = t

    @pl.when(j < 2)
    def _():
        scale = jnp.where(j == 0, q_scale, 1.0)
        cos = cos_ref[...] * scale
        sin = sin_ref[...] * scale
        for h in range(HEADS_PER_GROUP):
            t = acc[:, _head_cols(h)]
            put(h, t * cos + pltpu.roll(t, HEAD_DIM // 2, 1) * sin)

    @pl.when(j == 2)
    def _():
        for h in range(HEADS_PER_GROUP):
            put(h, acc[:, _head_cols(h)])

    if dil > 1:
        for r in range(dil):
            for h in range(HEADS_PER_GROUP):
                rows = scratch[0][h, pl.ds(r, tm // dil, stride=dil), :]
                o_ref[r, :, _head_cols(h)] = rows.astype(o_ref.dtype)


def _qkv_proj(h, w_qkv, gi, cos, sin):
    s, kdim = h.shape
    dil = DILATIONS[gi]
    tm = _tile(s, 512)
    ncol = ATTN_WIDTH // GROUP_WIDTH
    scratch = [] if dil == 1 else [pltpu.VMEM((HEADS_PER_GROUP, tm, HEAD_DIM), F32)]
    return pl.pallas_call(
        functools.partial(_qkv_kernel, dil=dil, q_scale=HEAD_DIM ** -0.5),
        out_shape=jax.ShapeDtypeStruct((3, dil, s // dil, GROUP_WIDTH), BF16),
        grid=(s // tm, 3),
        in_specs=[pl.BlockSpec((tm, kdim), lambda i, j: (i, 0)),
                  pl.BlockSpec((kdim, GROUP_WIDTH), lambda i, j: (0, ncol * j + gi)),
                  pl.BlockSpec((tm, HEAD_DIM), lambda i, j: (i, 0)),
                  pl.BlockSpec((tm, HEAD_DIM), lambda i, j: (i, 0))],
        out_specs=pl.BlockSpec((None, dil, tm // dil, GROUP_WIDTH), lambda i, j: (j, 0, i, 0)),
        scratch_shapes=scratch,
        compiler_params=_params("parallel", "arbitrary"),
        name=f"qkv_proj_{dil}",
    )(h, w_qkv, cos, sin)


def _rope_tables(s):
    half = HEAD_DIM // 2
    inv = ROPE_THETA ** (-jnp.arange(half, dtype=F32) / half)
    ang = jnp.arange(s).astype(F32)[:, None] * inv[None, :]
    cos, sin = jnp.cos(ang), jnp.sin(ang)
    return jnp.concatenate([cos, cos], axis=-1), jnp.concatenate([-sin, sin], axis=-1)


def _attn_kernel(q_ref, kp_ref, kc_ref, kn_ref, vp_ref, vc_ref, vn_ref, o_ref, l_ref, kw_ref, vw_ref,
                 *, tq, seq):
    r = ATTN_RADIUS
    kw_ref[0:r, :] = kp_ref[...]
    kw_ref[r:r + tq, :] = kc_ref[...]
    kw_ref[r + tq:, :] = kn_ref[...]
    vw_ref[0:r, :] = vp_ref[...]
    vw_ref[r:r + tq, :] = vc_ref[...]
    vw_ref[r + tq:, :] = vn_ref[...]

    base = pl.program_id(1) * tq
    nkeys = Q_SUB + 2 * r
    qi = lax.broadcasted_iota(jnp.int32, (Q_SUB, nkeys), 0)
    kj = lax.broadcasted_iota(jnp.int32, (Q_SUB, nkeys), 1)
    lane_head = lax.broadcasted_iota(jnp.int32, (Q_SUB, HEAD_DIM), 1) // LSE_LANES

    for sb in range(tq // Q_SUB):
        rows = slice(sb * Q_SUB, (sb + 1) * Q_SUB)
        first = base + sb * Q_SUB - r
        lo = jnp.maximum(qi, -first)
        hi = jnp.minimum(qi + 2 * r, seq - 1 - first)
        keep = (kj >= lo) & (kj <= hi)
        lse_tile = jnp.zeros((Q_SUB, HEAD_DIM), F32)
        for h in range(HEADS_PER_GROUP):
            cols = _head_cols(h)
            qs = q_ref[rows, cols]
            ks = kw_ref[sb * Q_SUB:sb * Q_SUB + nkeys, cols]
            vs = vw_ref[sb * Q_SUB:sb * Q_SUB + nkeys, cols]
            sc = lax.dot_general(qs, ks, (((1,), (1,)), ((), ())), preferred_element_type=F32)
            sc = jnp.where(keep, sc, NEG_INF)
            m = jnp.max(sc, axis=-1, keepdims=True)
            p = jnp.exp(sc - m)
            den = jnp.sum(p, axis=-1, keepdims=True)
            o = jnp.dot(p.astype(BF16), vs, preferred_element_type=F32) / den
            o_ref[rows, cols] = o.astype(o_ref.dtype)
            lse_tile = jnp.where(lane_head == h, m + jnp.log(den), lse_tile)
        l_ref[rows, :] = lse_tile


def _attn_group(qkv):
    _, dil, seq, _ = qkv.shape
    tq = _tile(seq, 512)
    r = ATTN_RADIUS
    halo_per_tile = tq // r
    n_halo = seq // r

    def main(which):
        return pl.BlockSpec((None, None, tq, GROUP_WIDTH), lambda rr, lb: (which, rr, lb, 0))

    def before(which):
        return pl.BlockSpec((None, None, r, GROUP_WIDTH),
                            lambda rr, lb: (which, rr, jnp.maximum(lb * halo_per_tile - 1, 0), 0))

    def after(which):
        return pl.BlockSpec((None, None, r, GROUP_WIDTH),
                            lambda rr, lb: (which, rr, jnp.minimum((lb + 1) * halo_per_tile, n_halo - 1), 0))

    return pl.pallas_call(
        functools.partial(_attn_kernel, tq=tq, seq=seq),
        out_shape=(jax.ShapeDtypeStruct((dil, seq, GROUP_WIDTH), BF16),
                   jax.ShapeDtypeStruct((dil, seq, HEAD_DIM), F32)),
        grid=(dil, seq // tq),
        in_specs=[main(0), before(1), main(1), after(1), before(2), main(2), after(2)],
        out_specs=(pl.BlockSpec((None, tq, GROUP_WIDTH), lambda rr, lb: (rr, lb, 0)),
                   pl.BlockSpec((None, tq, HEAD_DIM), lambda rr, lb: (rr, lb, 0))),
        scratch_shapes=[pltpu.VMEM((tq + 2 * r, GROUP_WIDTH), BF16),
                        pltpu.VMEM((tq + 2 * r, GROUP_WIDTH), BF16)],
        compiler_params=_params("parallel", "arbitrary"),
        name=f"banded_attn_{dil}",
    )(qkv, qkv, qkv, qkv, qkv, qkv, qkv)


def _combine_kernel(*refs):
    ng = len(DILATIONS)
    o_refs, l_refs = refs[0:2 * ng:2], refs[1:2 * ng:2]
    out_ref, os_ref, ls_ref = refs[2 * ng:]
    t = out_ref.shape[0]
    for g, dil in enumerate(DILATIONS):
        if dil == 1:
            continue
        for r in range(dil):
            dst = pl.ds(r, t // dil, stride=dil)
            ls_ref[g, dst, :] = l_refs[g][r]
            for h in range(HEADS_PER_GROUP):
                os_ref[g, h, dst, :] = o_refs[g][r, :, _head_cols(h)].astype(F32)

    def lse_of(g):
        return l_refs[g][0] if DILATIONS[g] == 1 else ls_ref[g]

    top = functools.reduce(jnp.maximum, [lse_of(g) for g in range(ng)])
    e = [jnp.exp(lse_of(g) - top) for g in range(ng)]
    inv = 1.0 / functools.reduce(lambda a, b: a + b, e)
    w = [eg * inv for eg in e]
    for h in range(HEADS_PER_GROUP):
        acc = None
        for g, dil in enumerate(DILATIONS):
            og = o_refs[g][0, :, _head_cols(h)].astype(F32) if dil == 1 else os_ref[g, h]
            term = w[g][:, h * LSE_LANES:h * LSE_LANES + 1] * og
            acc = term if acc is None else acc + term
        out_ref[:, _head_cols(h)] = acc.astype(out_ref.dtype)


def _combine(outs):
    ng = len(DILATIONS)
    s = outs[0][0].shape[0] * outs[0][0].shape[1]
    t = _tile(s, 512)
    args, in_specs = [], []
    for (o, l), dil in zip(outs, DILATIONS):
        args += [o, l]
        in_specs += [pl.BlockSpec((dil, t // dil, GROUP_WIDTH), lambda i: (0, i, 0)),
                     pl.BlockSpec((dil, t // dil, HEAD_DIM), lambda i: (0, i, 0))]
    return pl.pallas_call(
        _combine_kernel,
        out_shape=jax.ShapeDtypeStruct((s, GROUP_WIDTH), BF16),
        grid=(s // t,),
        in_specs=in_specs,
        out_specs=pl.BlockSpec((t, GROUP_WIDTH), lambda i: (i, 0)),
        scratch_shapes=[pltpu.VMEM((ng, HEADS_PER_GROUP, t, HEAD_DIM), F32),
                        pltpu.VMEM((ng, t, HEAD_DIM), F32)],
        compiler_params=_params("parallel"),
        name="attn_combine",
    )(*args)


def _dft_tables(s, cg):
    n1, n2 = DFT_ROWS, s // DFT_ROWS

    def cs(rows, cols, period):
        ang = 2.0 * np.pi * ((np.arange(rows)[:, None] * np.arange(cols)[None, :]) % period) / period
        return np.cos(ang), np.sin(ang)

    c1, s1 = cs(n1, n1, n1)
    c2, s2 = cs(n2, n2, n2)
    ct, st = cs(n1, n2, s)
    cc, sc = cs(cg, cg, cg)
    norm = 1.0 / math.sqrt(s * cg)
    as_bf16 = lambda a: jnp.asarray(a, F32).astype(BF16)
    return dict(
        w1=as_bf16(np.concatenate([c1, -s1], axis=0)),
        tw_cos=jnp.asarray(ct, F32), tw_sin=jnp.asarray(st, F32),
        w2_re=as_bf16(np.concatenate([c2, -s2], axis=0)),
        w2_im=as_bf16(np.concatenate([s2, c2], axis=0)),
        wc_re=as_bf16(cc * norm), wc_im=as_bf16(sc * norm))


def _dft_stage1_kernel(x_ref, w1_ref, twc_ref, tws_ref, br_ref, bi_ref):
    n1, slab, _ = x_ref.shape
    lane = lax.broadcasted_iota(jnp.int32, twc_ref.shape, 1)
    for b in range(slab):
        n2 = pl.program_id(0) * slab + b
        a = jnp.dot(w1_ref[...], x_ref[:, b, :].astype(BF16), preferred_element_type=F32)
        ar, ai = a[:n1], a[n1:]
        c = jnp.sum(jnp.where(lane == n2, twc_ref[...], 0.0), axis=1, keepdims=True)
        sn = jnp.sum(jnp.where(lane == n2, tws_ref[...], 0.0), axis=1, keepdims=True)
        br_ref[:, b, :] = ar * c + ai * sn
        bi_ref[:, b, :] = ai * c - ar * sn


def _dft_stage2_kernel(br_ref, bi_ref, w2r_ref, w2i_ref, wcr_ref, wci_ref, o_ref):
    slab, n2, _ = br_ref.shape
    cg = wcr_ref.shape[0]
    for kk in range(slab):
        z = (jnp.dot(w2r_ref[...], br_ref[kk].astype(BF16), preferred_element_type=F32)
             + jnp.dot(w2i_ref[...], bi_ref[kk].astype(BF16), preferred_element_type=F32))
        zr, zi = z[:n2].astype(BF16), z[n2:].astype(BF16)
        for g in range(FOURIER_GROUPS):
            cols = slice(g * cg, (g + 1) * cg)
            o_ref[:, kk, cols] = (jnp.dot(zr[:, cols], wcr_ref[...], preferred_element_type=F32)
                                  + jnp.dot(zi[:, cols], wci_ref[...], preferred_element_type=F32))


def _fourier_mix(f):
    s, width = f.shape
    n1, n2 = DFT_ROWS, s // DFT_ROWS
    cg = width // FOURIER_GROUPS
    t = _dft_tables(s, cg)
    slab = DFT_SLAB
    full = lambda a: pl.BlockSpec(a.shape, lambda i: (0,) * a.ndim)
    br, bi = pl.pallas_call(
        _dft_stage1_kernel,
        out_shape=(jax.ShapeDtypeStruct((n1, n2, width), F32),) * 2,
        grid=(n2 // slab,),
        in_specs=[pl.BlockSpec((n1, slab, width), lambda i: (0, i, 0)),
                  full(t["w1"]), full(t["tw_cos"]), full(t["tw_sin"])],
        out_specs=(pl.BlockSpec((n1, slab, width), lambda i: (0, i, 0)),) * 2,
        compiler_params=_params("parallel"),
        name="dft_stage1",
    )(f.reshape(n1, n2, width), t["w1"], t["tw_cos"], t["tw_sin"])
    out = pl.pallas_call(
        _dft_stage2_kernel,
        out_shape=jax.ShapeDtypeStruct((n2, n1, width), F32),
        grid=(n1 // slab,),
        in_specs=[pl.BlockSpec((slab, n2, width), lambda i: (i, 0, 0)),
                  pl.BlockSpec((slab, n2, width), lambda i: (i, 0, 0)),
                  full(t["w2_re"]), full(t["w2_im"]), full(t["wc_re"]), full(t["wc_im"])],
        out_specs=pl.BlockSpec((n2, slab, width), lambda i: (0, i, 0)),
        compiler_params=_params("parallel"),
        name="dft_stage2",
    )(br, bi, t["w2_re"], t["w2_im"], t["wc_re"], t["wc_im"])
    return out.reshape(s, width)


def _merge_kernel(a_ref, f_ref, wa_ref, wf_ref, ga_ref, gf_ref, o_ref):
    ab = jnp.dot(a_ref[...], wa_ref[...], preferred_element_type=F32)
    fb = jnp.dot(f_ref[...].astype(BF16), wf_ref[...], preferred_element_type=F32)
    o_ref[...] = (ga_ref[...].astype(F32) * ab + gf_ref[...].astype(F32) * fb).astype(o_ref.dtype)


def _merge(attn, four, wa, wf, ga, gf):
    s, ka = attn.shape
    kf = four.shape[1]
    n = wa.shape[1]
    tm, tn = _tile(s, 512), _tile(n, 1024)
    return pl.pallas_call(
        _merge_kernel,
        out_shape=jax.ShapeDtypeStruct((s, n), BF16),
        grid=(s // tm, n // tn),
        in_specs=[pl.BlockSpec((tm, ka), lambda i, j: (i, 0)),
                  pl.BlockSpec((tm, kf), lambda i, j: (i, 0)),
                  pl.BlockSpec((ka, tn), lambda i, j: (0, j)),
                  pl.BlockSpec((kf, tn), lambda i, j: (0, j)),
                  pl.BlockSpec((tm, tn), lambda i, j: (i, j)),
                  pl.BlockSpec((tm, tn), lambda i, j: (i, j))],
        out_specs=pl.BlockSpec((tm, tn), lambda i, j: (i, j)),
        compiler_params=_params("parallel", "arbitrary"),
        name="branch_merge",
    )(attn, four, wa, wf, ga, gf)


def _proj_res_kernel(a_ref, w_ref, res_ref, gate_ref, o_ref, *, alpha):
    acc = jnp.dot(a_ref[...], w_ref[...], preferred_element_type=F32)
    o_ref[...] = alpha * res_ref[...] + gate_ref[...] * acc


def _proj_res(a, w, res, gate, alpha, tm_want, tn_want):
    s, kdim = a.shape
    n = w.shape[1]
    tm, tn = _tile(s, tm_want), _tile(n, tn_want)
    return pl.pallas_call(
        functools.partial(_proj_res_kernel, alpha=alpha),
        out_shape=jax.ShapeDtypeStruct((s, n), F32),
        grid=(s // tm, n // tn),
        in_specs=[pl.BlockSpec((tm, kdim), lambda i, j: (i, 0)),
                  pl.BlockSpec((kdim, tn), lambda i, j: (0, j)),
                  pl.BlockSpec((tm, tn), lambda i, j: (i, j)),
                  pl.BlockSpec((1, tn), lambda i, j: (0, j))],
        out_specs=pl.BlockSpec((tm, tn), lambda i, j: (i, j)),
        compiler_params=_params("parallel", "arbitrary"),
        name="proj_res",
    )(a, w, res, gate)


def _ln_out_kernel(t_ref, g_ref, b_ref, *refs):
    y = _normalize(t_ref[...]) * g_ref[...] + b_ref[...]
    if len(refs) == 1:
        refs[0][...] = y
    else:
        mod_ref, y_ref, h_ref = refs
        y_ref[...] = y
        h_ref[...] = (_normalize(y) * (1.0 + mod_ref[1:2, :]) + mod_ref[0:1, :]).astype(h_ref.dtype)


def _ln_out(t, g, b, mod3=None):
    s, d = t.shape
    tm = _tile(s, 256)
    row = pl.BlockSpec((tm, d), lambda i: (i, 0))
    vec = lambda rows: pl.BlockSpec((rows, d), lambda i: (0, 0))
    args, in_specs = [t, g.reshape(1, d), b.reshape(1, d)], [row, vec(1), vec(1)]
    out_shape, out_specs = [jax.ShapeDtypeStruct((s, d), F32)], [row]
    if mod3 is not None:
        args.append(mod3)
        in_specs.append(vec(3))
        out_shape.append(jax.ShapeDtypeStruct((s, d), BF16))
        out_specs.append(row)
    return pl.pallas_call(
        _ln_out_kernel,
        out_shape=tuple(out_shape),
        grid=(s // tm,),
        in_specs=in_specs,
        out_specs=tuple(out_specs),
        compiler_params=_params("parallel"),
        name="ln_out",
    )(*args)


def _swiglu_kernel(x_ref, wg_ref, wu_ref, o_ref):
    x = x_ref[...]
    gpre = jnp.dot(x, wg_ref[...], preferred_element_type=F32)
    up = jnp.dot(x, wu_ref[...], preferred_element_type=F32)
    o_ref[...] = (gpre * jax.nn.sigmoid(gpre) * up).astype(o_ref.dtype)


def _swiglu(h, wg, wu):
    s, kdim = h.shape
    n = wg.shape[1]
    tm, tn = _tile(s, 1024), _tile(n, 256)
    return pl.pallas_call(
        _swiglu_kernel,
        out_shape=jax.ShapeDtypeStruct((s, n), BF16),
        grid=(s // tm, n // tn),
        in_specs=[pl.BlockSpec((tm, kdim), lambda i, j: (i, 0)),
                  pl.BlockSpec((kdim, tn), lambda i, j: (0, j)),
                  pl.BlockSpec((kdim, tn), lambda i, j: (0, j))],
        out_specs=pl.BlockSpec((tm, tn), lambda i, j: (i, j)),
        compiler_params=_params("parallel", "arbitrary"),
        name="swiglu",
    )(h, wg, wu)


def _layer(x, mod6, wts, alpha):
    s, d = x.shape
    h = _ln_mod(x, mod6[0:3])

    cos, sin = _rope_tables(s)
    outs = [_attn_group(_qkv_proj(h, wts["w_qkv"], gi, cos, sin)) for gi in range(len(DILATIONS))]
    attn = _combine(outs)

    f = _proj(h, wts["wf"], _proj_cast_kernel, F32)
    ga = _proj(h, wts["wga"], _proj_sigmoid_kernel, BF16)
    gf = _proj(h, wts["wgf"], _proj_sigmoid_kernel, BF16)
    four = _fourier_mix(f)

    merged = _merge(attn, four, wts["w_attn_up"], wts["w_fourier_up"], ga, gf)
    t1 = _proj_res(merged, wts["w_mix_out"], x, mod6[2:3], alpha, 1024, 512)
    x1, h2 = _ln_out(t1, wts["ln1_g"], wts["ln1_b"], mod6[3:6])
    u = _swiglu(h2, wts["w_gate"], wts["w_up"])
    t2 = _proj_res(u, wts["w_down"], x1, mod6[5:6], alpha, 512, 512)
    (y,) = _ln_out(t2, wts["ln2_g"], wts["ln2_b"])
    return y


def kernel(x_prompt, x_sample, c_prompt, c_sample, w_ada, b_ada, w_in, w_attn_up, w_fourier_up,
           w_mix_out, ln1_g, ln1_b, w_gate, w_up, w_down, ln2_g, ln2_b):
    depth = w_ada.shape[0]
    d = x_prompt.shape[-1]
    alpha = (2.0 * depth) ** 0.25
    fw = d // 4
    bounds = (0, 3 * ATTN_WIDTH, 3 * ATTN_WIDTH + fw, 3 * ATTN_WIDTH + fw + d, 3 * ATTN_WIDTH + fw + 2 * d)
    xs = [x_prompt[0], x_sample[0]]
    c2 = jnp.concatenate([c_prompt, c_sample], axis=0)
    for l in range(depth):
        mod = _ada_mod(c2, w_ada[l], b_ada[l]).reshape(2, N_MOD, d)
        sect = [w_in[l][:, bounds[i]:bounds[i + 1]].astype(BF16) for i in range(4)]
        wts = dict(w_qkv=sect[0], wf=sect[1], wga=sect[2], wgf=sect[3],
                   w_attn_up=w_attn_up[l].astype(BF16), w_fourier_up=w_fourier_up[l].astype(BF16),
                   w_mix_out=w_mix_out[l].astype(BF16), ln1_g=ln1_g[l], ln1_b=ln1_b[l],
                   w_gate=w_gate[l].astype(BF16), w_up=w_up[l].astype(BF16),
                   w_down=w_down[l].astype(BF16), ln2_g=ln2_g[l], ln2_b=ln2_b[l])
        xs = [_layer(xs[g], mod[g], wts, alpha) for g in range(2)]
    return (xs[0][None], xs[1][None])
```

```python
import functools
import math

import numpy as np
import jax
import jax.numpy as jnp
from jax import lax
from jax.experimental import pallas as pl
from jax.experimental.pallas import tpu as pltpu

HEAD_DIM = 128
HEADS_PER_GROUP = 8
GROUP_WIDTH = HEADS_PER_GROUP * HEAD_DIM
DILATIONS = (1, 4, 16)
ATTN_RADIUS = 64
ATTN_WIDTH = len(DILATIONS) * GROUP_WIDTH
FOURIER_WIDTH_DIVISOR = 4
FOURIER_GROUPS = 4
DFT_ROWS = 128
DFT_SLAB = 8
N_MOD = 6
ROPE_THETA = 10000.0
LN_EPS = 1e-5
NEG_INF = -1e30
Q_SUB = 128
LSE_LANES = HEAD_DIM // HEADS_PER_GROUP

VMEM_LIMIT_BYTES = 56 * 1024 * 1024

F32 = jnp.float32
BF16 = jnp.bfloat16


def _params(*sem):
    return pltpu.CompilerParams(dimension_semantics=sem, vmem_limit_bytes=VMEM_LIMIT_BYTES)


def _tile(n, want):
    t = min(n, want)
    while n % t:
        t //= 2
    return t


def _head_cols(h):
    return slice(h * HEAD_DIM, (h + 1) * HEAD_DIM)


def _ada_kernel(ct_ref, w_ref, b_ref, o_ref, sb_ref):
    kdim, tn = w_ref.shape
    nt = tn // 128

    @pl.when(pl.program_id(0) == 0)
    def _():
        c = ct_ref[...]
        s = c * jax.nn.sigmoid(c)
        sb_ref[0] = jnp.broadcast_to(s[:, 0:1], (kdim, 128))
        sb_ref[1] = jnp.broadcast_to(s[:, 1:2], (kdim, 128))

    def body(kc, acc):
        r0 = pl.multiple_of(kc * 8, 8)
        s0 = sb_ref[0, pl.ds(r0, 8), :]
        s1 = sb_ref[1, pl.ds(r0, 8), :]
        out = []
        for t in range(nt):
            w = w_ref[pl.ds(r0, 8), t * 128:(t + 1) * 128]
            out.append(acc[2 * t] + w * s0)
            out.append(acc[2 * t + 1] + w * s1)
        return tuple(out)

    zero = jnp.zeros((8, 128), F32)
    acc = lax.fori_loop(0, kdim // 8, body, (zero,) * (2 * nt), unroll=8)
    for t in range(nt):
        cols = slice(t * 128, (t + 1) * 128)
        o_ref[0:1, cols] = jnp.sum(acc[2 * t], axis=0, keepdims=True) + b_ref[:, cols]
        o_ref[1:2, cols] = jnp.sum(acc[2 * t + 1], axis=0, keepdims=True) + b_ref[:, cols]


def _ada_mod(c2, w, b):
    kdim, n = w.shape
    tn = _tile(n, 512)
    return pl.pallas_call(
        _ada_kernel,
        out_shape=jax.ShapeDtypeStruct((2, n), F32),
        grid=(n // tn,),
        in_specs=[pl.BlockSpec((kdim, 2), lambda j: (0, 0)),
                  pl.BlockSpec((kdim, tn), lambda j: (0, j)),
                  pl.BlockSpec((1, tn), lambda j: (0, j))],
        out_specs=pl.BlockSpec((2, tn), lambda j: (0, j)),
        scratch_shapes=[pltpu.VMEM((2, kdim, 128), F32)],
        compiler_params=_params("arbitrary"),
        name="ada_mod",
    )(c2.T, w, b.reshape(1, n))


def _normalize(x):
    mu = jnp.mean(x, axis=-1, keepdims=True)
    xc = x - mu
    var = jnp.mean(xc * xc, axis=-1, keepdims=True)
    return xc * lax.rsqrt(var + LN_EPS)


def _ln_mod_kernel(x_ref, mod_ref, o_ref):
    y = _normalize(x_ref[...])
    o_ref[...] = (y * (1.0 + mod_ref[1:2, :]) + mod_ref[0:1, :]).astype(o_ref.dtype)


def _ln_mod(x, mod3):
    s, d = x.shape
    tm = _tile(s, 256)
    return pl.pallas_call(
        _ln_mod_kernel,
        out_shape=jax.ShapeDtypeStruct((s, d), BF16),
        grid=(s // tm,),
        in_specs=[pl.BlockSpec((tm, d), lambda i: (i, 0)),
                  pl.BlockSpec((3, d), lambda i: (0, 0))],
        out_specs=pl.BlockSpec((tm, d), lambda i: (i, 0)),
        compiler_params=_params("parallel"),
        name="ln_mod",
    )(x, mod3)


def _proj_kernel(x_ref, w_ref, o_ref, wb_ref, *, sigmoid):
    @pl.when(pl.program_id(1) == 0)
    def _():
        wb_ref[...] = w_ref[...].astype(BF16)

    acc = jnp.dot(x_ref[...], wb_ref[...], preferred_element_type=F32)
    if sigmoid:
        acc = jax.nn.sigmoid(acc)
    o_ref[...] = acc.astype(o_ref.dtype)


def _proj(h, w, col0, ncols, sigmoid, out_dtype, tm_want=1024, tn_want=512):
    s, kdim = h.shape
    tm, tn = _tile(s, tm_want), _tile(math.gcd(col0, ncols), tn_want)
    j0 = col0 // tn
    return pl.pallas_call(
        functools.partial(_proj_kernel, sigmoid=sigmoid),
        out_shape=jax.ShapeDtypeStruct((s, ncols), out_dtype),
        grid=(ncols // tn, s // tm),
        in_specs=[pl.BlockSpec((tm, kdim), lambda j, i: (i, 0)),
                  pl.BlockSpec((kdim, tn), lambda j, i: (0, j0 + j))],
        out_specs=pl.BlockSpec((tm, tn), lambda j, i: (i, j)),
        scratch_shapes=[pltpu.VMEM((kdim, tn), BF16)],
        compiler_params=_params("parallel", "arbitrary"),
        name="in_proj",
    )(h, w)


def _qkv_kernel(x_ref, w_ref, cos_ref, sin_ref, o_ref, *scratch, dil, q_scale):
    j = pl.program_id(1)
    acc = jnp.dot(x_ref[...], w_ref[...], preferred_element_type=F32)
    tm = acc.shape[0]

    def put(h, t):
        if dil == 1:
            o_ref[0, :, _head_cols(h)] = t.astype(o_ref.dtype)
        else:
            scratch[0][h] = t

    @pl.when(j < 2)
    def _():
        scale = jnp.where(j == 0, q_scale, 1.0)
        cos = cos_ref[...] * scale
        sin = sin_ref[...] * scale
        for h in range(HEADS_PER_GROUP):
            t = acc[:, _head_cols(h)]
            put(h, t * cos + pltpu.roll(t, HEAD_DIM // 2, 1) * sin)

    @pl.when(j == 2)
    def _():
        for h in range(HEADS_PER_GROUP):
            put(h, acc[:, _head_cols(h)])

    if dil > 1:
        for r in range(dil):
            for h in range(HEADS_PER_GROUP):
                rows = scratch[0][h, pl.ds(r, tm // dil, stride=dil), :]
                o_ref[r, :, _head_cols(h)] = rows.astype(o_ref.dtype)


def _qkv_proj(h, w_qkv, gi, cos, sin):
    s, kdim = h.shape
    dil = DILATIONS[gi]
    tm = _tile(s, 512)
    ncol = ATTN_WIDTH // GROUP_WIDTH
    scratch = [] if dil == 1 else [pltpu.VMEM((HEADS_PER_GROUP, tm, HEAD_DIM), F32)]
    return pl.pallas_call(
        functools.partial(_qkv_kernel, dil=dil, q_scale=HEAD_DIM ** -0.5),
        out_shape=jax.ShapeDtypeStruct((3, dil, s // dil, GROUP_WIDTH), BF16),
        grid=(s // tm, 3),
        in_specs=[pl.BlockSpec((tm, kdim), lambda i, j: (i, 0)),
                  pl.BlockSpec((kdim, GROUP_WIDTH), lambda i, j: (0, ncol * j + gi)),
                  pl.BlockSpec((tm, HEAD_DIM), lambda i, j: (i, 0)),
                  pl.BlockSpec((tm, HEAD_DIM), lambda i, j: (i, 0))],
        out_specs=pl.BlockSpec((None, dil, tm // dil, GROUP_WIDTH), lambda i, j: (j, 0, i, 0)),
        scratch_shapes=scratch,
        compiler_params=_params("parallel", "arbitrary"),
        name=f"qkv_proj_{dil}",
    )(h, w_qkv, cos, sin)


def _rope_tables(s):
    half = HEAD_DIM // 2
    inv = ROPE_THETA ** (-jnp.arange(half, dtype=F32) / half)
    ang = jnp.arange(s).astype(F32)[:, None] * inv[None, :]
    cos, sin = jnp.cos(ang), jnp.sin(ang)
    return jnp.concatenate([cos, cos], axis=-1), jnp.concatenate([-sin, sin], axis=-1)


def _attn_kernel(q_ref, kp_ref, kc_ref, kn_ref, vp_ref, vc_ref, vn_ref, o_ref, l_ref, kw_ref, vw_ref,
                 *, tq, seq):
    r = ATTN_RADIUS
    kw_ref[0:r, :] = kp_ref[...]
    kw_ref[r:r + tq, :] = kc_ref[...]
    kw_ref[r + tq:, :] = kn_ref[...]
    vw_ref[0:r, :] = vp_ref[...]
    vw_ref[r:r + tq, :] = vc_ref[...]
    vw_ref[r + tq:, :] = vn_ref[...]

    base = pl.program_id(1) * tq
    nkeys = Q_SUB + 2 * r
    qi = lax.broadcasted_iota(jnp.int32, (Q_SUB, nkeys), 0)
    kj = lax.broadcasted_iota(jnp.int32, (Q_SUB, nkeys), 1)
    lane_head = lax.broadcasted_iota(jnp.int32, (Q_SUB, HEAD_DIM), 1) // LSE_LANES

    for sb in range(tq // Q_SUB):
        rows = slice(sb * Q_SUB, (sb + 1) * Q_SUB)
        first = base + sb * Q_SUB - r
        lo = jnp.maximum(qi, -first)
        hi = jnp.minimum(qi + 2 * r, seq - 1 - first)
        keep = (kj >= lo) & (kj <= hi)
        lse_tile = jnp.zeros((Q_SUB, HEAD_DIM), F32)
        for h in range(HEADS_PER_GROUP):
            cols = _head_cols(h)
            qs = q_ref[rows, cols]
            ks = kw_ref[sb * Q_SUB:sb * Q_SUB + nkeys, cols]
            vs = vw_ref[sb * Q_SUB:sb * Q_SUB + nkeys, cols]
            sc = lax.dot_general(qs, ks, (((1,), (1,)), ((), ())), preferred_element_type=F32)
            sc = jnp.where(keep, sc, NEG_INF)
            m = jnp.max(sc, axis=-1, keepdims=True)
            p = jnp.exp(sc - m)
            den = jnp.sum(p, axis=-1, keepdims=True)
            o = jnp.dot(p.astype(BF16), vs, preferred_element_type=F32) / den
            o_ref[rows, cols] = o.astype(o_ref.dtype)
            lse_tile = jnp.where(lane_head == h, m + jnp.log(den), lse_tile)
        l_ref[rows, :] = lse_tile


def _attn_group(qkv):
    _, dil, seq, _ = qkv.shape
    tq = _tile(seq, 512)
    r = ATTN_RADIUS
    halo_per_tile = tq // r
    n_halo = seq // r

    def main(which):
        return pl.BlockSpec((None, None, tq, GROUP_WIDTH), lambda rr, lb: (which, rr, lb, 0))

    def before(which):
        return pl.BlockSpec((None, None, r, GROUP_WIDTH),
                            lambda rr, lb: (which, rr, jnp.maximum(lb * halo_per_tile - 1, 0), 0))

    def after(which):
        return pl.BlockSpec((None, None, r, GROUP_WIDTH),
                            lambda rr, lb: (which, rr, jnp.minimum((lb + 1) * halo_per_tile, n_halo - 1), 0))

    return pl.pallas_call(
        functools.partial(_attn_kernel, tq=tq, seq=seq),
        out_shape=(jax.ShapeDtypeStruct((dil, seq, GROUP_WIDTH), BF16),
                   jax.ShapeDtypeStruct((dil, seq, HEAD_DIM), F32)),
        grid=(dil, seq // tq),
        in_specs=[main(0), before(1), main(1), after(1), before(2), main(2), after(2)],
        out_specs=(pl.BlockSpec((None, tq, GROUP_WIDTH), lambda rr, lb: (rr, lb, 0)),
                   pl.BlockSpec((None, tq, HEAD_DIM), lambda rr, lb: (rr, lb, 0))),
        scratch_shapes=[pltpu.VMEM((tq + 2 * r, GROUP_WIDTH), BF16),
                        pltpu.VMEM((tq + 2 * r, GROUP_WIDTH), BF16)],
        compiler_params=_params("parallel", "arbitrary"),
        name=f"banded_attn_{dil}",
    )(qkv, qkv, qkv, qkv, qkv, qkv, qkv)


def _combine_kernel(*refs):
    ng = len(DILATIONS)
    o_refs, l_refs = refs[0:2 * ng:2], refs[1:2 * ng:2]
    out_ref, os_ref, ls_ref = refs[2 * ng:]
    t = out_ref.shape[0]
    for g, dil in enumerate(DILATIONS):
        if dil == 1:
            continue
        for r in range(dil):
            dst = pl.ds(r, t // dil, stride=dil)
            ls_ref[g, dst, :] = l_refs[g][r]
            for h in range(HEADS_PER_GROUP):
                os_ref[g, h, dst, :] = o_refs[g][r, :, _head_cols(h)].astype(F32)

    def lse_of(g):
        return l_refs[g][0] if DILATIONS[g] == 1 else ls_ref[g]

    top = functools.reduce(jnp.maximum, [lse_of(g) for g in range(ng)])
    e = [jnp.exp(lse_of(g) - top) for g in range(ng)]
    inv = 1.0 / functools.reduce(lambda a, b: a + b, e)
    w = [eg * inv for eg in e]
    for h in range(HEADS_PER_GROUP):
        acc = None
        for g, dil in enumerate(DILATIONS):
            og = o_refs[g][0, :, _head_cols(h)].astype(F32) if dil == 1 else os_ref[g, h]
            term = w[g][:, h * LSE_LANES:h * LSE_LANES + 1] * og
            acc = term if acc is None else acc + term
        out_ref[:, _head_cols(h)] = acc.astype(out_ref.dtype)


def _combine(outs):
    ng = len(DILATIONS)
    s = outs[0][0].shape[0] * outs[0][0].shape[1]
    t = _tile(s, 512)
    args, in_specs = [], []
    for (o, l), dil in zip(outs, DILATIONS):
        args += [o, l]
        in_specs += [pl.BlockSpec((dil, t // dil, GROUP_WIDTH), lambda i: (0, i, 0)),
                     pl.BlockSpec((dil, t // dil, HEAD_DIM), lambda i: (0, i, 0))]
    return pl.pallas_call(
        _combine_kernel,
        out_shape=jax.ShapeDtypeStruct((s, GROUP_WIDTH), BF16),
        grid=(s // t,),
        in_specs=in_specs,
        out_specs=pl.BlockSpec((t, GROUP_WIDTH), lambda i: (i, 0)),
        scratch_shapes=[pltpu.VMEM((ng, HEADS_PER_GROUP, t, HEAD_DIM), F32),
                        pltpu.VMEM((ng, t, HEAD_DIM), F32)],
        compiler_params=_params("parallel"),
        name="attn_combine",
    )(*args)


def _dft_tables(s, cg):
    n1, n2 = DFT_ROWS, s // DFT_ROWS

    def cs(rows, cols, period):
        ang = 2.0 * np.pi * ((np.arange(rows)[:, None] * np.arange(cols)[None, :]) % period) / period
        return np.cos(ang), np.sin(ang)

    c1, s1 = cs(n1, n1, n1)
    c2, s2 = cs(n2, n2, n2)
    ct, st = cs(n1, n2, s)
    cc, sc = cs(cg, cg, cg)
    norm = 1.0 / math.sqrt(s * cg)
    as_bf16 = lambda a: jnp.asarray(a, F32).astype(BF16)
    return dict(
        w1=as_bf16(np.concatenate([c1, -s1], axis=0)),
        tw_cos=jnp.asarray(ct, F32), tw_sin=jnp.asarray(st, F32),
        w2_re=as_bf16(np.concatenate([c2, -s2], axis=0)),
        w2_im=as_bf16(np.concatenate([s2, c2], axis=0)),
        wc_re=as_bf16(cc * norm), wc_im=as_bf16(sc * norm))


def _dft_stage1_kernel(x_ref, w1_ref, twc_ref, tws_ref, br_ref, bi_ref):
    n1, slab, _ = x_ref.shape
    lane = lax.broadcasted_iota(jnp.int32, twc_ref.shape, 1)
    for b in range(slab):
        n2 = pl.program_id(0) * slab + b
        a = jnp.dot(w1_ref[...], x_ref[:, b, :].astype(BF16), preferred_element_type=F32)
        ar, ai = a[:n1], a[n1:]
        c = jnp.sum(jnp.where(lane == n2, twc_ref[...], 0.0), axis=1, keepdims=True)
        sn = jnp.sum(jnp.where(lane == n2, tws_ref[...], 0.0), axis=1, keepdims=True)
        br_ref[:, b, :] = ar * c + ai * sn
        bi_ref[:, b, :] = ai * c - ar * sn


def _dft_stage2_kernel(br_ref, bi_ref, w2r_ref, w2i_ref, wcr_ref, wci_ref, o_ref):
    slab, n2, _ = br_ref.shape
    cg = wcr_ref.shape[0]
    for kk in range(slab):
        z = (jnp.dot(w2r_ref[...], br_ref[kk].astype(BF16), preferred_element_type=F32)
             + jnp.dot(w2i_ref[...], bi_ref[kk].astype(BF16), preferred_element_type=F32))
        zr, zi = z[:n2].astype(BF16), z[n2:].astype(BF16)
        for g in range(FOURIER_GROUPS):
            cols = slice(g * cg, (g + 1) * cg)
            o_ref[:, kk, cols] = (jnp.dot(zr[:, cols], wcr_ref[...], preferred_element_type=F32)
                                  + jnp.dot(zi[:, cols], wci_ref[...], preferred_element_type=F32))


def _fourier_mix(f):
    s, width = f.shape
    n1, n2 = DFT_ROWS, s // DFT_ROWS
    cg = width // FOURIER_GROUPS
    t = _dft_tables(s, cg)
    slab = DFT_SLAB
    full = lambda a: pl.BlockSpec(a.shape, lambda i: (0,) * a.ndim)
    br, bi = pl.pallas_call(
        _dft_stage1_kernel,
        out_shape=(jax.ShapeDtypeStruct((n1, n2, width), F32),) * 2,
        grid=(n2 // slab,),
        in_specs=[pl.BlockSpec((n1, slab, width), lambda i: (0, i, 0)),
                  full(t["w1"]), full(t["tw_cos"]), full(t["tw_sin"])],
        out_specs=(pl.BlockSpec((n1, slab, width), lambda i: (0, i, 0)),) * 2,
        compiler_params=_params("parallel"),
        name="dft_stage1",
    )(f.reshape(n1, n2, width), t["w1"], t["tw_cos"], t["tw_sin"])
    out = pl.pallas_call(
        _dft_stage2_kernel,
        out_shape=jax.ShapeDtypeStruct((n2, n1, width), F32),
        grid=(n1 // slab,),
        in_specs=[pl.BlockSpec((slab, n2, width), lambda i: (i, 0, 0)),
                  pl.BlockSpec((slab, n2, width), lambda i: (i, 0, 0)),
                  full(t["w2_re"]), full(t["w2_im"]), full(t["wc_re"]), full(t["wc_im"])],
        out_specs=pl.BlockSpec((n2, slab, width), lambda i: (0, i, 0)),
        compiler_params=_params("parallel"),
        name="dft_stage2",
    )(br, bi, t["w2_re"], t["w2_im"], t["wc_re"], t["wc_im"])
    return out.reshape(s, width)


def _merge_kernel(a_ref, f_ref, wa_ref, wf_ref, ga_ref, gf_ref, o_ref):
    ab = jnp.dot(a_ref[...], wa_ref[...], preferred_element_type=F32)
    fb = jnp.dot(f_ref[...].astype(BF16), wf_ref[...], preferred_element_type=F32)
    o_ref[...] = (ga_ref[...].astype(F32) * ab + gf_ref[...].astype(F32) * fb).astype(o_ref.dtype)


def _merge(attn, four, wa, wf, ga, gf):
    s, ka = attn.shape
    kf = four.shape[1]
    n = wa.shape[1]
    tm, tn = _tile(s, 512), _tile(n, 1024)
    return pl.pallas_call(
        _merge_kernel,
        out_shape=jax.ShapeDtypeStruct((s, n), BF16),
        grid=(s // tm, n // tn),
        in_specs=[pl.BlockSpec((tm, ka), lambda i, j: (i, 0)),
                  pl.BlockSpec((tm, kf), lambda i, j: (i, 0)),
                  pl.BlockSpec((ka, tn), lambda i, j: (0, j)),
                  pl.BlockSpec((kf, tn), lambda i, j: (0, j)),
                  pl.BlockSpec((tm, tn), lambda i, j: (i, j)),
                  pl.BlockSpec((tm, tn), lambda i, j: (i, j))],
        out_specs=pl.BlockSpec((tm, tn), lambda i, j: (i, j)),
        compiler_params=_params("parallel", "arbitrary"),
        name="branch_merge",
    )(attn, four, wa, wf, ga, gf)


def _proj_res_kernel(a_ref, w_ref, res_ref, gate_ref, o_ref, *, alpha):
    acc = jnp.dot(a_ref[...], w_ref[...], preferred_element_type=F32)
    o_ref[...] = alpha * res_ref[...] + gate_ref[...] * acc


def _proj_res(a, w, res, gate, alpha, tm_want, tn_want):
    s, kdim = a.shape
    n = w.shape[1]
    tm, tn = _tile(s, tm_want), _tile(n, tn_want)
    return pl.pallas_call(
        functools.partial(_proj_res_kernel, alpha=alpha),
        out_shape=jax.ShapeDtypeStruct((s, n), F32),
        grid=(s // tm, n // tn),
        in_specs=[pl.BlockSpec((tm, kdim), lambda i, j: (i, 0)),
                  pl.BlockSpec((kdim, tn), lambda i, j: (0, j)),
                  pl.BlockSpec((tm, tn), lambda i, j: (i, j)),
                  pl.BlockSpec((1, tn), lambda i, j: (0, j))],
        out_specs=pl.BlockSpec((tm, tn), lambda i, j: (i, j)),
        compiler_params=_params("parallel", "arbitrary"),
        name="proj_res",
    )(a, w, res, gate)


def _ln_out_kernel(t_ref, g_ref, b_ref, *refs):
    y = _normalize(t_ref[...]) * g_ref[...] + b_ref[...]
    if len(refs) == 1:
        refs[0][...] = y
    else:
        mod_ref, y_ref, h_ref = refs
        y_ref[...] = y
        h_ref[...] = (_normalize(y) * (1.0 + mod_ref[1:2, :]) + mod_ref[0:1, :]).astype(h_ref.dtype)


def _ln_out(t, g, b, mod3=None):
    s, d = t.shape
    tm = _tile(s, 256)
    row = pl.BlockSpec((tm, d), lambda i: (i, 0))
    vec = lambda rows: pl.BlockSpec((rows, d), lambda i: (0, 0))
    args, in_specs = [t, g.reshape(1, d), b.reshape(1, d)], [row, vec(1), vec(1)]
    out_shape, out_specs = [jax.ShapeDtypeStruct((s, d), F32)], [row]
    if mod3 is not None:
        args.append(mod3)
        in_specs.append(vec(3))
        out_shape.append(jax.ShapeDtypeStruct((s, d), BF16))
        out_specs.append(row)
    return pl.pallas_call(
        _ln_out_kernel,
        out_shape=tuple(out_shape),
        grid=(s // tm,),
        in_specs=in_specs,
        out_specs=tuple(out_specs),
        compiler_params=_params("parallel"),
        name="ln_out",
    )(*args)


def _swiglu_kernel(x_ref, wg_ref, wu_ref, o_ref, wgb_ref, wub_ref):
    @pl.when(pl.program_id(1) == 0)
    def _():
        wgb_ref[...] = wg_ref[...].astype(BF16)
        wub_ref[...] = wu_ref[...].astype(BF16)

    x = x_ref[...]
    gpre = jnp.dot(x, wgb_ref[...], preferred_element_type=F32)
    up = jnp.dot(x, wub_ref[...], preferred_element_type=F32)
    o_ref[...] = (gpre * jax.nn.sigmoid(gpre) * up).astype(o_ref.dtype)


def _swiglu(h, wg, wu):
    s, kdim = h.shape
    n = wg.shape[1]
    tm, tn = _tile(s, 1024), _tile(n, 256)
    return pl.pallas_call(
        _swiglu_kernel,
        out_shape=jax.ShapeDtypeStruct((s, n), BF16),
        grid=(n // tn, s // tm),
        in_specs=[pl.BlockSpec((tm, kdim), lambda j, i: (i, 0)),
                  pl.BlockSpec((kdim, tn), lambda j, i: (0, j)),
                  pl.BlockSpec((kdim, tn), lambda j, i: (0, j))],
        out_specs=pl.BlockSpec((tm, tn), lambda j, i: (i, j)),
        scratch_shapes=[pltpu.VMEM((kdim, tn), BF16), pltpu.VMEM((kdim, tn), BF16)],
        compiler_params=_params("parallel", "arbitrary"),
        name="swiglu",
    )(h, wg, wu)


def _layer(x, mod6, wts, alpha):
    s, d = x.shape
    h = _ln_mod(x, mod6[0:3])

    cos, sin = _rope_tables(s)
    outs = [_attn_group(_qkv_proj(h, wts["w_qkv"], gi, cos, sin)) for gi in range(len(DILATIONS))]
    attn = _combine(outs)

    fw = d // FOURIER_WIDTH_DIVISOR
    f = _proj(h, wts["w_in"], 3 * ATTN_WIDTH, fw, False, F32)
    ga = _proj(h, wts["w_in"], 3 * ATTN_WIDTH + fw, d, True, BF16)
    gf = _proj(h, wts["w_in"], 3 * ATTN_WIDTH + fw + d, d, True, BF16)
    four = _fourier_mix(f)

    merged = _merge(attn, four, wts["w_attn_up"], wts["w_fourier_up"], ga, gf)
    t1 = _proj_res(merged, wts["w_mix_out"], x, mod6[2:3], alpha, 1024, 512)
    x1, h2 = _ln_out(t1, wts["ln1_g"], wts["ln1_b"], mod6[3:6])
    u = _swiglu(h2, wts["w_gate"], wts["w_up"])
    t2 = _proj_res(u, wts["w_down"], x1, mod6[5:6], alpha, 512, 512)
    (y,) = _ln_out(t2, wts["ln2_g"], wts["ln2_b"])
    return y


def kernel(x_prompt, x_sample, c_prompt, c_sample, w_ada, b_ada, w_in, w_attn_up, w_fourier_up,
           w_mix_out, ln1_g, ln1_b, w_gate, w_up, w_down, ln2_g, ln2_b):
    depth = w_ada.shape[0]
    d = x_prompt.shape[-1]
    alpha = (2.0 * depth) ** 0.25
    xs = [x_prompt[0], x_sample[0]]
    c2 = jnp.concatenate([c_prompt, c_sample], axis=0)
    for l in range(depth):
        mod = _ada_mod(c2, w_ada[l], b_ada[l]).reshape(2, N_MOD, d)
        wts = dict(w_qkv=w_in[l][:, :3 * ATTN_WIDTH].astype(BF16), w_in=w_in[l],
                   w_attn_up=w_attn_up[l].astype(BF16), w_fourier_up=w_fourier_up[l].astype(BF16),
                   w_mix_out=w_mix_out[l].astype(BF16), ln1_g=ln1_g[l], ln1_b=ln1_b[l],
                   w_gate=w_gate[l], w_up=w_up[l],
                   w_down=w_down[l].astype(BF16), ln2_g=ln2_g[l], ln2_b=ln2_b[l])
        xs = [_layer(xs[g], mod[g], wts, alpha) for g in range(2)]
    return (xs[0][None], xs[1][None])
```

```python
import functools
import math

import numpy as np
import jax
import jax.numpy as jnp
from jax import lax
from jax.experimental import pallas as pl
from jax.experimental.pallas import tpu as pltpu

HEAD_DIM = 128
HEADS_PER_GROUP = 8
GROUP_WIDTH = HEADS_PER_GROUP * HEAD_DIM
DILATIONS = (1, 4, 16)
ATTN_RADIUS = 64
ATTN_WIDTH = len(DILATIONS) * GROUP_WIDTH
FOURIER_WIDTH_DIVISOR = 4
FOURIER_GROUPS = 4
DFT_ROWS = 128
DFT_SLAB = 8
N_MOD = 6
ROPE_THETA = 10000.0
LN_EPS = 1e-5
NEG_INF = -1e30
Q_SUB = 128
LSE_LANES = HEAD_DIM // HEADS_PER_GROUP

VMEM_LIMIT_BYTES = 56 * 1024 * 1024

F32 = jnp.float32
BF16 = jnp.bfloat16


def _params(*sem):
    return pltpu.CompilerParams(dimension_semantics=sem, vmem_limit_bytes=VMEM_LIMIT_BYTES)


def _tile(n, want):
    t = min(n, want)
    while n % t:
        t //= 2
    return t


def _head_cols(h):
    return slice(h * HEAD_DIM, (h + 1) * HEAD_DIM)


def _sigmoid(x):
    return 0.5 * jnp.tanh(0.5 * x) + 0.5


def _ada_kernel(ct_ref, w_ref, b_ref, o_ref, sb_ref):
    kdim, tn = w_ref.shape
    nt = tn // 128

    @pl.when(pl.program_id(0) == 0)
    def _():
        c = ct_ref[...]
        s = c * jax.nn.sigmoid(c)
        sb_ref[0] = jnp.broadcast_to(s[:, 0:1], (kdim, 128))
        sb_ref[1] = jnp.broadcast_to(s[:, 1:2], (kdim, 128))

    def body(kc, acc):
        r0 = pl.multiple_of(kc * 8, 8)
        s0 = sb_ref[0, pl.ds(r0, 8), :]
        s1 = sb_ref[1, pl.ds(r0, 8), :]
        out = []
        for t in range(nt):
            w = w_ref[pl.ds(r0, 8), t * 128:(t + 1) * 128]
            out.append(acc[2 * t] + w * s0)
            out.append(acc[2 * t + 1] + w * s1)
        return tuple(out)

    zero = jnp.zeros((8, 128), F32)
    acc = lax.fori_loop(0, kdim // 8, body, (zero,) * (2 * nt), unroll=8)
    for t in range(nt):
        cols = slice(t * 128, (t + 1) * 128)
        o_ref[0:1, cols] = jnp.sum(acc[2 * t], axis=0, keepdims=True) + b_ref[:, cols]
        o_ref[1:2, cols] = jnp.sum(acc[2 * t + 1], axis=0, keepdims=True) + b_ref[:, cols]


def _ada_mod(c2, w, b):
    kdim, n = w.shape
    tn = _tile(n, 512)
    return pl.pallas_call(
        _ada_kernel,
        out_shape=jax.ShapeDtypeStruct((2, n), F32),
        grid=(n // tn,),
        in_specs=[pl.BlockSpec((kdim, 2), lambda j: (0, 0)),
                  pl.BlockSpec((kdim, tn), lambda j: (0, j)),
                  pl.BlockSpec((1, tn), lambda j: (0, j))],
        out_specs=pl.BlockSpec((2, tn), lambda j: (0, j)),
        scratch_shapes=[pltpu.VMEM((2, kdim, 128), F32)],
        compiler_params=_params("arbitrary"),
        name="ada_mod",
    )(c2.T, w, b.reshape(1, n))


def _normalize(x):
    mu = jnp.mean(x, axis=-1, keepdims=True)
    xc = x - mu
    var = jnp.mean(xc * xc, axis=-1, keepdims=True)
    return xc * lax.rsqrt(var + LN_EPS)


def _ln_mod_kernel(x_ref, mod_ref, o_ref):
    y = _normalize(x_ref[...])
    o_ref[...] = (y * (1.0 + mod_ref[1:2, :]) + mod_ref[0:1, :]).astype(o_ref.dtype)


def _ln_mod(x, mod3):
    s, d = x.shape
    tm = _tile(s, 256)
    return pl.pallas_call(
        _ln_mod_kernel,
        out_shape=jax.ShapeDtypeStruct((s, d), BF16),
        grid=(s // tm,),
        in_specs=[pl.BlockSpec((tm, d), lambda i: (i, 0)),
                  pl.BlockSpec((3, d), lambda i: (0, 0))],
        out_specs=pl.BlockSpec((tm, d), lambda i: (i, 0)),
        compiler_params=_params("parallel"),
        name="ln_mod",
    )(x, mod3)


def _proj_kernel(x_ref, w_ref, o_ref, wb_ref, *, sigmoid):
    @pl.when(pl.program_id(1) == 0)
    def _():
        wb_ref[...] = w_ref[...].astype(BF16)

    acc = jnp.dot(x_ref[...], wb_ref[...], preferred_element_type=F32)
    if sigmoid:
        acc = _sigmoid(acc)
    o_ref[...] = acc.astype(o_ref.dtype)


def _proj(h, w, col0, ncols, sigmoid, out_dtype, tm_want=1024, tn_want=512):
    s, kdim = h.shape
    tm, tn = _tile(s, tm_want), _tile(math.gcd(col0, ncols), tn_want)
    j0 = col0 // tn
    return pl.pallas_call(
        functools.partial(_proj_kernel, sigmoid=sigmoid),
        out_shape=jax.ShapeDtypeStruct((s, ncols), out_dtype),
        grid=(ncols // tn, s // tm),
        in_specs=[pl.BlockSpec((tm, kdim), lambda j, i: (i, 0)),
                  pl.BlockSpec((kdim, tn), lambda j, i: (0, j0 + j))],
        out_specs=pl.BlockSpec((tm, tn), lambda j, i: (i, j)),
        scratch_shapes=[pltpu.VMEM((kdim, tn), BF16)],
        compiler_params=_params("parallel", "arbitrary"),
        name="in_proj",
    )(h, w)


def _qkv_kernel(x_ref, w_ref, cos_ref, sin_ref, o_ref, *scratch, dil, q_scale):
    j = pl.program_id(1)
    acc = jnp.dot(x_ref[...], w_ref[...], preferred_element_type=F32)
    tm = acc.shape[0]

    scale = jnp.where(j == 0, q_scale, 1.0)
    cos = jnp.where(j < 2, cos_ref[...] * scale, 1.0)
    sin = jnp.where(j < 2, sin_ref[...] * scale, 0.0)
    for h in range(HEADS_PER_GROUP):
        t = acc[:, _head_cols(h)]
        t = t * cos + pltpu.roll(t, HEAD_DIM // 2, 1) * sin
        if dil == 1:
            o_ref[0, :, _head_cols(h)] = t.astype(o_ref.dtype)
        else:
            scratch[0][h] = t

    if dil > 1:
        for r in range(dil):
            for h in range(HEADS_PER_GROUP):
                rows = scratch[0][h, pl.ds(r, tm // dil, stride=dil), :]
                o_ref[r, :, _head_cols(h)] = rows.astype(o_ref.dtype)


def _qkv_proj(h, w_qkv, gi, cos, sin):
    s, kdim = h.shape
    dil = DILATIONS[gi]
    tm = _tile(s, 1024)
    ncol = ATTN_WIDTH // GROUP_WIDTH
    scratch = [] if dil == 1 else [pltpu.VMEM((HEADS_PER_GROUP, tm, HEAD_DIM), F32)]
    return pl.pallas_call(
        functools.partial(_qkv_kernel, dil=dil, q_scale=HEAD_DIM ** -0.5),
        out_shape=jax.ShapeDtypeStruct((3, dil, s // dil, GROUP_WIDTH), BF16),
        grid=(s // tm, 3),
        in_specs=[pl.BlockSpec((tm, kdim), lambda i, j: (i, 0)),
                  pl.BlockSpec((kdim, GROUP_WIDTH), lambda i, j: (0, ncol * j + gi)),
                  pl.BlockSpec((tm, HEAD_DIM), lambda i, j: (i, 0)),
                  pl.BlockSpec((tm, HEAD_DIM), lambda i, j: (i, 0))],
        out_specs=pl.BlockSpec((None, dil, tm // dil, GROUP_WIDTH), lambda i, j: (j, 0, i, 0)),
        scratch_shapes=scratch,
        compiler_params=_params("parallel", "arbitrary"),
        name=f"qkv_proj_{dil}",
    )(h, w_qkv, cos, sin)


def _rope_tables(s):
    half = HEAD_DIM // 2
    inv = ROPE_THETA ** (-jnp.arange(half, dtype=F32) / half)
    ang = jnp.arange(s).astype(F32)[:, None] * inv[None, :]
    cos, sin = jnp.cos(ang), jnp.sin(ang)
    return jnp.concatenate([cos, cos], axis=-1), jnp.concatenate([-sin, sin], axis=-1)


def _attn_kernel(q_ref, kp_ref, kc_ref, kn_ref, vp_ref, vc_ref, vn_ref, o_ref, l_ref, kw_ref, vw_ref,
                 *, tq, seq):
    r = ATTN_RADIUS
    kw_ref[0:r, :] = kp_ref[...]
    kw_ref[r:r + tq, :] = kc_ref[...]
    kw_ref[r + tq:, :] = kn_ref[...]
    vw_ref[0:r, :] = vp_ref[...]
    vw_ref[r:r + tq, :] = vc_ref[...]
    vw_ref[r + tq:, :] = vn_ref[...]

    base = pl.program_id(1) * tq
    nkeys = Q_SUB + 2 * r
    qi = lax.broadcasted_iota(jnp.int32, (Q_SUB, nkeys), 0)
    kj = lax.broadcasted_iota(jnp.int32, (Q_SUB, nkeys), 1)
    lane_head = lax.broadcasted_iota(jnp.int32, (Q_SUB, HEAD_DIM), 1) // LSE_LANES

    for sb in range(tq // Q_SUB):
        rows = slice(sb * Q_SUB, (sb + 1) * Q_SUB)
        first = base + sb * Q_SUB - r
        lo = jnp.maximum(qi, -first)
        hi = jnp.minimum(qi + 2 * r, seq - 1 - first)
        keep = (kj >= lo) & (kj <= hi)
        lse_tile = jnp.zeros((Q_SUB, HEAD_DIM), F32)
        for h in range(HEADS_PER_GROUP):
            cols = _head_cols(h)
            qs = q_ref[rows, cols]
            ks = kw_ref[sb * Q_SUB:sb * Q_SUB + nkeys, cols]
            vs = vw_ref[sb * Q_SUB:sb * Q_SUB + nkeys, cols]
            sc = lax.dot_general(qs, ks, (((1,), (1,)), ((), ())), preferred_element_type=F32)
            sc = jnp.where(keep, sc, NEG_INF)
            m = jnp.max(sc, axis=-1, keepdims=True)
            p = jnp.exp(sc - m)
            den = jnp.sum(p, axis=-1, keepdims=True)
            o = jnp.dot(p.astype(BF16), vs, preferred_element_type=F32) / den
            o_ref[rows, cols] = o.astype(o_ref.dtype)
            lse_tile = jnp.where(lane_head == h, m + jnp.log(den), lse_tile)
        l_ref[rows, :] = lse_tile


def _attn_group(qkv):
    _, dil, seq, _ = qkv.shape
    tq = _tile(seq, 512)
    r = ATTN_RADIUS
    halo_per_tile = tq // r
    n_halo = seq // r

    def main(which):
        return pl.BlockSpec((None, None, tq, GROUP_WIDTH), lambda rr, lb: (which, rr, lb, 0))

    def before(which):
        return pl.BlockSpec((None, None, r, GROUP_WIDTH),
                            lambda rr, lb: (which, rr, jnp.maximum(lb * halo_per_tile - 1, 0), 0))

    def after(which):
        return pl.BlockSpec((None, None, r, GROUP_WIDTH),
                            lambda rr, lb: (which, rr, jnp.minimum((lb + 1) * halo_per_tile, n_halo - 1), 0))

    return pl.pallas_call(
        functools.partial(_attn_kernel, tq=tq, seq=seq),
        out_shape=(jax.ShapeDtypeStruct((dil, seq, GROUP_WIDTH), BF16),
                   jax.ShapeDtypeStruct((dil, seq, HEAD_DIM), F32)),
        grid=(dil, seq // tq),
        in_specs=[main(0), before(1), main(1), after(1), before(2), main(2), after(2)],
        out_specs=(pl.BlockSpec((None, tq, GROUP_WIDTH), lambda rr, lb: (rr, lb, 0)),
                   pl.BlockSpec((None, tq, HEAD_DIM), lambda rr, lb: (rr, lb, 0))),
        scratch_shapes=[pltpu.VMEM((tq + 2 * r, GROUP_WIDTH), BF16),
                        pltpu.VMEM((tq + 2 * r, GROUP_WIDTH), BF16)],
        compiler_params=_params("parallel", "arbitrary"),
        name=f"banded_attn_{dil}",
    )(qkv, qkv, qkv, qkv, qkv, qkv, qkv)


def _combine_kernel(*refs):
    ng = len(DILATIONS)
    o_refs, l_refs = refs[0:2 * ng:2], refs[1:2 * ng:2]
    out_ref, os_ref, ls_ref = refs[2 * ng:]
    t = out_ref.shape[0]
    for g, dil in enumerate(DILATIONS):
        if dil == 1:
            continue
        for r in range(dil):
            dst = pl.ds(r, t // dil, stride=dil)
            ls_ref[g, dst, :] = l_refs[g][r]
            for h in range(HEADS_PER_GROUP):
                os_ref[g, h, dst, :] = o_refs[g][r, :, _head_cols(h)].astype(F32)

    def lse_of(g):
        return l_refs[g][0] if DILATIONS[g] == 1 else ls_ref[g]

    top = functools.reduce(jnp.maximum, [lse_of(g) for g in range(ng)])
    e = [jnp.exp(lse_of(g) - top) for g in range(ng)]
    inv = 1.0 / functools.reduce(lambda a, b: a + b, e)
    w = [eg * inv for eg in e]
    for h in range(HEADS_PER_GROUP):
        acc = None
        for g, dil in enumerate(DILATIONS):
            og = o_refs[g][0, :, _head_cols(h)].astype(F32) if dil == 1 else os_ref[g, h]
            term = w[g][:, h * LSE_LANES:h * LSE_LANES + 1] * og
            acc = term if acc is None else acc + term
        out_ref[:, _head_cols(h)] = acc.astype(out_ref.dtype)


def _combine(outs):
    ng = len(DILATIONS)
    s = outs[0][0].shape[0] * outs[0][0].shape[1]
    t = _tile(s, 512)
    args, in_specs = [], []
    for (o, l), dil in zip(outs, DILATIONS):
        args += [o, l]
        in_specs += [pl.BlockSpec((dil, t // dil, GROUP_WIDTH), lambda i: (0, i, 0)),
                     pl.BlockSpec((dil, t // dil, HEAD_DIM), lambda i: (0, i, 0))]
    return pl.pallas_call(
        _combine_kernel,
        out_shape=jax.ShapeDtypeStruct((s, GROUP_WIDTH), BF16),
        grid=(s // t,),
        in_specs=in_specs,
        out_specs=pl.BlockSpec((t, GROUP_WIDTH), lambda i: (i, 0)),
        scratch_shapes=[pltpu.VMEM((ng, HEADS_PER_GROUP, t, HEAD_DIM), F32),
                        pltpu.VMEM((ng, t, HEAD_DIM), F32)],
        compiler_params=_params("parallel"),
        name="attn_combine",
    )(*args)


def _dft_tables(s, cg):
    n1, n2 = DFT_ROWS, s // DFT_ROWS

    def cs(rows, cols, period):
        ang = 2.0 * np.pi * ((np.arange(rows)[:, None] * np.arange(cols)[None, :]) % period) / period
        return np.cos(ang), np.sin(ang)

    c1, s1 = cs(n1, n1, n1)
    c2, s2 = cs(n2, n2, n2)
    ct, st = cs(n1, n2, s)
    cc, sc = cs(cg, cg, cg)
    norm = 1.0 / math.sqrt(s * cg)
    as_bf16 = lambda a: jnp.asarray(a, F32).astype(BF16)
    return dict(
        w1=as_bf16(np.concatenate([c1, -s1], axis=0)),
        tw_cos=jnp.asarray(ct, F32), tw_sin=jnp.asarray(st, F32),
        w2_re=as_bf16(np.concatenate([c2, -s2], axis=0)),
        w2_im=as_bf16(np.concatenate([s2, c2], axis=0)),
        wc_re=as_bf16(cc * norm), wc_im=as_bf16(sc * norm))


def _dft_stage1_kernel(x_ref, w1_ref, twc_ref, tws_ref, br_ref, bi_ref):
    n1, slab, _ = x_ref.shape
    lane = lax.broadcasted_iota(jnp.int32, twc_ref.shape, 1)
    for b in range(slab):
        n2 = pl.program_id(0) * slab + b
        a = jnp.dot(w1_ref[...], x_ref[:, b, :].astype(BF16), preferred_element_type=F32)
        ar, ai = a[:n1], a[n1:]
        c = jnp.sum(jnp.where(lane == n2, twc_ref[...], 0.0), axis=1, keepdims=True)
        sn = jnp.sum(jnp.where(lane == n2, tws_ref[...], 0.0), axis=1, keepdims=True)
        br_ref[:, b, :] = ar * c + ai * sn
        bi_ref[:, b, :] = ai * c - ar * sn


def _dft_stage2_kernel(br_ref, bi_ref, w2r_ref, w2i_ref, wcr_ref, wci_ref, o_ref):
    slab, n2, _ = br_ref.shape
    cg = wcr_ref.shape[0]
    for kk in range(slab):
        z = (jnp.dot(w2r_ref[...], br_ref[kk].astype(BF16), preferred_element_type=F32)
             + jnp.dot(w2i_ref[...], bi_ref[kk].astype(BF16), preferred_element_type=F32))
        zr, zi = z[:n2].astype(BF16), z[n2:].astype(BF16)
        for g in range(FOURIER_GROUPS):
            cols = slice(g * cg, (g + 1) * cg)
            o_ref[:, kk, cols] = (jnp.dot(zr[:, cols], wcr_ref[...], preferred_element_type=F32)
                                  + jnp.dot(zi[:, cols], wci_ref[...], preferred_element_type=F32))


def _fourier_mix(f):
    s, width = f.shape
    n1, n2 = DFT_ROWS, s // DFT_ROWS
    cg = width // FOURIER_GROUPS
    t = _dft_tables(s, cg)
    slab = DFT_SLAB
    full = lambda a: pl.BlockSpec(a.shape, lambda i: (0,) * a.ndim)
    br, bi = pl.pallas_call(
        _dft_stage1_kernel,
        out_shape=(jax.ShapeDtypeStruct((n1, n2, width), F32),) * 2,
        grid=(n2 // slab,),
        in_specs=[pl.BlockSpec((n1, slab, width), lambda i: (0, i, 0)),
                  full(t["w1"]), full(t["tw_cos"]), full(t["tw_sin"])],
        out_specs=(pl.BlockSpec((n1, slab, width), lambda i: (0, i, 0)),) * 2,
        compiler_params=_params("parallel"),
        name="dft_stage1",
    )(f.reshape(n1, n2, width), t["w1"], t["tw_cos"], t["tw_sin"])
    out = pl.pallas_call(
        _dft_stage2_kernel,
        out_shape=jax.ShapeDtypeStruct((n2, n1, width), F32),
        grid=(n1 // slab,),
        in_specs=[pl.BlockSpec((slab, n2, width), lambda i: (i, 0, 0)),
                  pl.BlockSpec((slab, n2, width), lambda i: (i, 0, 0)),
                  full(t["w2_re"]), full(t["w2_im"]), full(t["wc_re"]), full(t["wc_im"])],
        out_specs=pl.BlockSpec((n2, slab, width), lambda i: (0, i, 0)),
        compiler_params=_params("parallel"),
        name="dft_stage2",
    )(br, bi, t["w2_re"], t["w2_im"], t["wc_re"], t["wc_im"])
    return out.reshape(s, width)


def _merge_kernel(a_ref, f_ref, wa_ref, wf_ref, ga_ref, gf_ref, o_ref):
    ab = jnp.dot(a_ref[...], wa_ref[...], preferred_element_type=F32)
    fb = jnp.dot(f_ref[...].astype(BF16), wf_ref[...], preferred_element_type=F32)
    o_ref[...] = (ga_ref[...].astype(F32) * ab + gf_ref[...].astype(F32) * fb).astype(o_ref.dtype)


def _merge(attn, four, wa, wf, ga, gf):
    s, ka = attn.shape
    kf = four.shape[1]
    n = wa.shape[1]
    tm, tn = _tile(s, 512), _tile(n, 1024)
    return pl.pallas_call(
        _merge_kernel,
        out_shape=jax.ShapeDtypeStruct((s, n), BF16),
        grid=(s // tm, n // tn),
        in_specs=[pl.BlockSpec((tm, ka), lambda i, j: (i, 0)),
                  pl.BlockSpec((tm, kf), lambda i, j: (i, 0)),
                  pl.BlockSpec((ka, tn), lambda i, j: (0, j)),
                  pl.BlockSpec((kf, tn), lambda i, j: (0, j)),
                  pl.BlockSpec((tm, tn), lambda i, j: (i, j)),
                  pl.BlockSpec((tm, tn), lambda i, j: (i, j))],
        out_specs=pl.BlockSpec((tm, tn), lambda i, j: (i, j)),
        compiler_params=_params("parallel", "arbitrary"),
        name="branch_merge",
    )(attn, four, wa, wf, ga, gf)


def _proj_res_kernel(a_ref, w_ref, res_ref, gate_ref, o_ref, *, alpha):
    acc = jnp.dot(a_ref[...], w_ref[...], preferred_element_type=F32)
    o_ref[...] = alpha * res_ref[...] + gate_ref[...] * acc


def _proj_res(a, w, res, gate, alpha, tm_want, tn_want):
    s, kdim = a.shape
    n = w.shape[1]
    tm, tn = _tile(s, tm_want), _tile(n, tn_want)
    return pl.pallas_call(
        functools.partial(_proj_res_kernel, alpha=alpha),
        out_shape=jax.ShapeDtypeStruct((s, n), F32),
        grid=(s // tm, n // tn),
        in_specs=[pl.BlockSpec((tm, kdim), lambda i, j: (i, 0)),
                  pl.BlockSpec((kdim, tn), lambda i, j: (0, j)),
                  pl.BlockSpec((tm, tn), lambda i, j: (i, j)),
                  pl.BlockSpec((1, tn), lambda i, j: (0, j))],
        out_specs=pl.BlockSpec((tm, tn), lambda i, j: (i, j)),
        compiler_params=_params("parallel", "arbitrary"),
        name="proj_res",
    )(a, w, res, gate)


def _ln_out_kernel(t_ref, g_ref, b_ref, *refs):
    y = _normalize(t_ref[...]) * g_ref[...] + b_ref[...]
    if len(refs) == 1:
        refs[0][...] = y
    else:
        mod_ref, y_ref, h_ref = refs
        y_ref[...] = y
        h_ref[...] = (_normalize(y) * (1.0 + mod_ref[1:2, :]) + mod_ref[0:1, :]).astype(h_ref.dtype)


def _ln_out(t, g, b, mod3=None):
    s, d = t.shape
    tm = _tile(s, 256)
    row = pl.BlockSpec((tm, d), lambda i: (i, 0))
    vec = lambda rows: pl.BlockSpec((rows, d), lambda i: (0, 0))
    args, in_specs = [t, g.reshape(1, d), b.reshape(1, d)], [row, vec(1), vec(1)]
    out_shape, out_specs = [jax.ShapeDtypeStruct((s, d), F32)], [row]
    if mod3 is not None:
        args.append(mod3)
        in_specs.append(vec(3))
        out_shape.append(jax.ShapeDtypeStruct((s, d), BF16))
        out_specs.append(row)
    return pl.pallas_call(
        _ln_out_kernel,
        out_shape=tuple(out_shape),
        grid=(s // tm,),
        in_specs=in_specs,
        out_specs=tuple(out_specs),
        compiler_params=_params("parallel"),
        name="ln_out",
    )(*args)


def _swiglu_kernel(x_ref, wg_ref, wu_ref, o_ref, wgb_ref, wub_ref):
    @pl.when(pl.program_id(1) == 0)
    def _():
        wgb_ref[...] = wg_ref[...].astype(BF16)
        wub_ref[...] = wu_ref[...].astype(BF16)

    x = x_ref[...]
    gpre = jnp.dot(x, wgb_ref[...], preferred_element_type=F32)
    up = jnp.dot(x, wub_ref[...], preferred_element_type=F32)
    o_ref[...] = (gpre * _sigmoid(gpre) * up).astype(o_ref.dtype)


def _swiglu(h, wg, wu):
    s, kdim = h.shape
    n = wg.shape[1]
    tm, tn = _tile(s, 1024), _tile(n, 256)
    return pl.pallas_call(
        _swiglu_kernel,
        out_shape=jax.ShapeDtypeStruct((s, n), BF16),
        grid=(n // tn, s // tm),
        in_specs=[pl.BlockSpec((tm, kdim), lambda j, i: (i, 0)),
                  pl.BlockSpec((kdim, tn), lambda j, i: (0, j)),
                  pl.BlockSpec((kdim, tn), lambda j, i: (0, j))],
        out_specs=pl.BlockSpec((tm, tn), lambda j, i: (i, j)),
        scratch_shapes=[pltpu.VMEM((kdim, tn), BF16), pltpu.VMEM((kdim, tn), BF16)],
        compiler_params=_params("parallel", "arbitrary"),
        name="swiglu",
    )(h, wg, wu)


def _layer(x, mod6, wts, alpha):
    s, d = x.shape
    h = _ln_mod(x, mod6[0:3])

    cos, sin = _rope_tables(s)
    outs = [_attn_group(_qkv_proj(h, wts["w_qkv"], gi, cos, sin)) for gi in range(len(DILATIONS))]
    attn = _combine(outs)

    fw = d // FOURIER_WIDTH_DIVISOR
    f = _proj(h, wts["w_in"], 3 * ATTN_WIDTH, fw, False, F32)
    ga = _proj(h, wts["w_in"], 3 * ATTN_WIDTH + fw, d, True, BF16)
    gf = _proj(h, wts["w_in"], 3 * ATTN_WIDTH + fw + d, d, True, BF16)
    four = _fourier_mix(f)

    merged = _merge(attn, four, wts["w_attn_up"], wts["w_fourier_up"], ga, gf)
    t1 = _proj_res(merged, wts["w_mix_out"], x, mod6[2:3], alpha, 1024, 512)
    x1, h2 = _ln_out(t1, wts["ln1_g"], wts["ln1_b"], mod6[3:6])
    u = _swiglu(h2, wts["w_gate"], wts["w_up"])
    t2 = _proj_res(u, wts["w_down"], x1, mod6[5:6], alpha, 512, 512)
    (y,) = _ln_out(t2, wts["ln2_g"], wts["ln2_b"])
    return y


def kernel(x_prompt, x_sample, c_prompt, c_sample, w_ada, b_ada, w_in, w_attn_up, w_fourier_up,
           w_mix_out, ln1_g, ln1_b, w_gate, w_up, w_down, ln2_g, ln2_b):
    depth = w_ada.shape[0]
    d = x_prompt.shape[-1]
    alpha = (2.0 * depth) ** 0.25
    xs = [x_prompt[0], x_sample[0]]
    c2 = jnp.concatenate([c_prompt, c_sample], axis=0)
    for l in range(depth):
        mod = _ada_mod(c2, w_ada[l], b_ada[l]).reshape(2, N_MOD, d)
        wts = dict(w_qkv=w_in[l][:, :3 * ATTN_WIDTH].astype(BF16), w_in=w_in[l],
                   w_attn_up=w_attn_up[l].astype(BF16), w_fourier_up=w_fourier_up[l].astype(BF16),
                   w_mix_out=w_mix_out[l].astype(BF16), ln1_g=ln1_g[l], ln1_b=ln1_b[l],
                   w_gate=w_gate[l], w_up=w_up[l],
                   w_down=w_down[l].astype(BF16), ln2_g=ln2_g[l], ln2_b=ln2_b[l])
        xs = [_layer(xs[g], mod[g], wts, alpha) for g in range(2)]
    return (xs[0][None], xs[1][None])
```

```python
import functools
import math

import numpy as np
import jax
import jax.numpy as jnp
from jax import lax
from jax.experimental import pallas as pl
from jax.experimental.pallas import tpu as pltpu

HEAD_DIM = 128
HEADS_PER_GROUP = 8
GROUP_WIDTH = HEADS_PER_GROUP * HEAD_DIM
DILATIONS = (1, 4, 16)
ATTN_RADIUS = 64
ATTN_WIDTH = len(DILATIONS) * GROUP_WIDTH
FOURIER_WIDTH_DIVISOR = 4
FOURIER_GROUPS = 4
DFT_ROWS = 128
DFT_SLAB = 8
N_MOD = 6
ROPE_THETA = 10000.0
LN_EPS = 1e-5
NEG_INF = -1e30
Q_SUB = 128
LSE_LANES = HEAD_DIM // HEADS_PER_GROUP

VMEM_LIMIT_BYTES = 56 * 1024 * 1024

F32 = jnp.float32
BF16 = jnp.bfloat16


def _params(*sem):
    return pltpu.CompilerParams(dimension_semantics=sem, vmem_limit_bytes=VMEM_LIMIT_BYTES)


def _tile(n, want):
    t = min(n, want)
    while n % t:
        t //= 2
    return t


def _head_cols(h):
    return slice(h * HEAD_DIM, (h + 1) * HEAD_DIM)


def _sigmoid(x):
    return 0.5 * jnp.tanh(0.5 * x) + 0.5


def _ada_kernel(ct_ref, w_ref, b_ref, o_ref, sb_ref):
    kdim, tn = w_ref.shape
    nt = tn // 128

    @pl.when(pl.program_id(0) == 0)
    def _():
        c = ct_ref[...]
        s = c * jax.nn.sigmoid(c)
        sb_ref[0] = jnp.broadcast_to(s[:, 0:1], (kdim, 128))
        sb_ref[1] = jnp.broadcast_to(s[:, 1:2], (kdim, 128))

    def body(kc, acc):
        r0 = pl.multiple_of(kc * 8, 8)
        s0 = sb_ref[0, pl.ds(r0, 8), :]
        s1 = sb_ref[1, pl.ds(r0, 8), :]
        out = []
        for t in range(nt):
            w = w_ref[pl.ds(r0, 8), t * 128:(t + 1) * 128]
            out.append(acc[2 * t] + w * s0)
            out.append(acc[2 * t + 1] + w * s1)
        return tuple(out)

    zero = jnp.zeros((8, 128), F32)
    acc = lax.fori_loop(0, kdim // 8, body, (zero,) * (2 * nt), unroll=8)
    for t in range(nt):
        cols = slice(t * 128, (t + 1) * 128)
        o_ref[0:1, cols] = jnp.sum(acc[2 * t], axis=0, keepdims=True) + b_ref[:, cols]
        o_ref[1:2, cols] = jnp.sum(acc[2 * t + 1], axis=0, keepdims=True) + b_ref[:, cols]


def _ada_mod(c2, w, b):
    kdim, n = w.shape
    tn = _tile(n, 512)
    return pl.pallas_call(
        _ada_kernel,
        out_shape=jax.ShapeDtypeStruct((2, n), F32),
        grid=(n // tn,),
        in_specs=[pl.BlockSpec((kdim, 2), lambda j: (0, 0)),
                  pl.BlockSpec((kdim, tn), lambda j: (0, j)),
                  pl.BlockSpec((1, tn), lambda j: (0, j))],
        out_specs=pl.BlockSpec((2, tn), lambda j: (0, j)),
        scratch_shapes=[pltpu.VMEM((2, kdim, 128), F32)],
        compiler_params=_params("arbitrary"),
        name="ada_mod",
    )(c2.T, w, b.reshape(1, n))


def _normalize(x):
    mu = jnp.mean(x, axis=-1, keepdims=True)
    xc = x - mu
    var = jnp.mean(xc * xc, axis=-1, keepdims=True)
    return xc * lax.rsqrt(var + LN_EPS)


def _ln_mod_kernel(x_ref, mod_ref, o_ref):
    y = _normalize(x_ref[...])
    o_ref[...] = (y * (1.0 + mod_ref[1:2, :]) + mod_ref[0:1, :]).astype(o_ref.dtype)


def _ln_mod(x, mod3):
    s, d = x.shape
    tm = _tile(s, 256)
    return pl.pallas_call(
        _ln_mod_kernel,
        out_shape=jax.ShapeDtypeStruct((s, d), BF16),
        grid=(s // tm,),
        in_specs=[pl.BlockSpec((tm, d), lambda i: (i, 0)),
                  pl.BlockSpec((3, d), lambda i: (0, 0))],
        out_specs=pl.BlockSpec((tm, d), lambda i: (i, 0)),
        compiler_params=_params("parallel"),
        name="ln_mod",
    )(x, mod3)


def _proj_kernel(x_ref, w_ref, o_ref, *, sigmoid):
    acc = jnp.dot(x_ref[...], w_ref[...].astype(BF16), preferred_element_type=F32)
    if sigmoid:
        acc = _sigmoid(acc)
    o_ref[...] = acc.astype(o_ref.dtype)


def _proj(h, w, col0, ncols, sigmoid, out_dtype, tm_want=1024, tn_want=512):
    s, kdim = h.shape
    tm, tn = _tile(s, tm_want), _tile(math.gcd(col0, ncols), tn_want)
    j0 = col0 // tn
    return pl.pallas_call(
        functools.partial(_proj_kernel, sigmoid=sigmoid),
        out_shape=jax.ShapeDtypeStruct((s, ncols), out_dtype),
        grid=(s // tm, ncols // tn),
        in_specs=[pl.BlockSpec((tm, kdim), lambda i, j: (i, 0)),
                  pl.BlockSpec((kdim, tn), lambda i, j: (0, j0 + j))],
        out_specs=pl.BlockSpec((tm, tn), lambda i, j: (i, j)),
        compiler_params=_params("parallel", "arbitrary"),
        name="in_proj",
    )(h, w)


def _qkv_kernel(x_ref, w_ref, cos_ref, sin_ref, o_ref, *scratch, dil, q_scale):
    j = pl.program_id(1)
    acc = jnp.dot(x_ref[...], w_ref[...], preferred_element_type=F32)
    tm = acc.shape[0]

    scale = jnp.where(j == 0, q_scale, 1.0)
    cos = jnp.where(j < 2, cos_ref[...] * scale, 1.0)
    sin = jnp.where(j < 2, sin_ref[...] * scale, 0.0)
    for h in range(HEADS_PER_GROUP):
        t = acc[:, _head_cols(h)]
        t = t * cos + pltpu.roll(t, HEAD_DIM // 2, 1) * sin
        if dil == 1:
            o_ref[0, :, _head_cols(h)] = t.astype(o_ref.dtype)
        else:
            scratch[0][h] = t

    if dil > 1:
        for r in range(dil):
            for h in range(HEADS_PER_GROUP):
                rows = scratch[0][h, pl.ds(r, tm // dil, stride=dil), :]
                o_ref[r, :, _head_cols(h)] = rows.astype(o_ref.dtype)


def _qkv_proj(h, w_qkv, gi, cos, sin):
    s, kdim = h.shape
    dil = DILATIONS[gi]
    tm = _tile(s, 1024)
    ncol = ATTN_WIDTH // GROUP_WIDTH
    scratch = [] if dil == 1 else [pltpu.VMEM((HEADS_PER_GROUP, tm, HEAD_DIM), F32)]
    return pl.pallas_call(
        functools.partial(_qkv_kernel, dil=dil, q_scale=HEAD_DIM ** -0.5),
        out_shape=jax.ShapeDtypeStruct((3, dil, s // dil, GROUP_WIDTH), BF16),
        grid=(s // tm, 3),
        in_specs=[pl.BlockSpec((tm, kdim), lambda i, j: (i, 0)),
                  pl.BlockSpec((kdim, GROUP_WIDTH), lambda i, j: (0, ncol * j + gi)),
                  pl.BlockSpec((tm, HEAD_DIM), lambda i, j: (i, 0)),
                  pl.BlockSpec((tm, HEAD_DIM), lambda i, j: (i, 0))],
        out_specs=pl.BlockSpec((None, dil, tm // dil, GROUP_WIDTH), lambda i, j: (j, 0, i, 0)),
        scratch_shapes=scratch,
        compiler_params=_params("parallel", "arbitrary"),
        name=f"qkv_proj_{dil}",
    )(h, w_qkv, cos, sin)


def _rope_tables(s):
    half = HEAD_DIM // 2
    inv = ROPE_THETA ** (-jnp.arange(half, dtype=F32) / half)
    ang = jnp.arange(s).astype(F32)[:, None] * inv[None, :]
    cos, sin = jnp.cos(ang), jnp.sin(ang)
    return jnp.concatenate([cos, cos], axis=-1), jnp.concatenate([-sin, sin], axis=-1)


def _attn_kernel(q_ref, kp_ref, kc_ref, kn_ref, vp_ref, vc_ref, vn_ref, o_ref, l_ref, kw_ref, vw_ref,
                 *, tq, seq):
    r = ATTN_RADIUS
    kw_ref[0:r, :] = kp_ref[...]
    kw_ref[r:r + tq, :] = kc_ref[...]
    kw_ref[r + tq:, :] = kn_ref[...]
    vw_ref[0:r, :] = vp_ref[...]
    vw_ref[r:r + tq, :] = vc_ref[...]
    vw_ref[r + tq:, :] = vn_ref[...]

    base = pl.program_id(1) * tq
    nkeys = Q_SUB + 2 * r
    qi = lax.broadcasted_iota(jnp.int32, (Q_SUB, nkeys), 0)
    kj = lax.broadcasted_iota(jnp.int32, (Q_SUB, nkeys), 1)
    lane_head = lax.broadcasted_iota(jnp.int32, (Q_SUB, HEAD_DIM), 1) // LSE_LANES

    for sb in range(tq // Q_SUB):
        rows = slice(sb * Q_SUB, (sb + 1) * Q_SUB)
        first = base + sb * Q_SUB - r
        lo = jnp.maximum(qi, -first)
        hi = jnp.minimum(qi + 2 * r, seq - 1 - first)
        keep = (kj >= lo) & (kj <= hi)
        lse_tile = jnp.zeros((Q_SUB, HEAD_DIM), F32)
        for h in range(HEADS_PER_GROUP):
            cols = _head_cols(h)
            qs = q_ref[rows, cols]
            ks = kw_ref[sb * Q_SUB:sb * Q_SUB + nkeys, cols]
            vs = vw_ref[sb * Q_SUB:sb * Q_SUB + nkeys, cols]
            sc = lax.dot_general(qs, ks, (((1,), (1,)), ((), ())), preferred_element_type=F32)
            sc = jnp.where(keep, sc, NEG_INF)
            m = jnp.max(sc, axis=-1, keepdims=True)
            p = jnp.exp(sc - m)
            den = jnp.sum(p, axis=-1, keepdims=True)
            o = jnp.dot(p.astype(BF16), vs, preferred_element_type=F32) / den
            o_ref[rows, cols] = o.astype(o_ref.dtype)
            lse_tile = jnp.where(lane_head == h, m + jnp.log(den), lse_tile)
        l_ref[rows, :] = lse_tile


def _attn_group(qkv):
    _, dil, seq, _ = qkv.shape
    tq = _tile(seq, 512)
    r = ATTN_RADIUS
    halo_per_tile = tq // r
    n_halo = seq // r

    def main(which):
        return pl.BlockSpec((None, None, tq, GROUP_WIDTH), lambda rr, lb: (which, rr, lb, 0))

    def before(which):
        return pl.BlockSpec((None, None, r, GROUP_WIDTH),
                            lambda rr, lb: (which, rr, jnp.maximum(lb * halo_per_tile - 1, 0), 0))

    def after(which):
        return pl.BlockSpec((None, None, r, GROUP_WIDTH),
                            lambda rr, lb: (which, rr, jnp.minimum((lb + 1) * halo_per_tile, n_halo - 1), 0))

    return pl.pallas_call(
        functools.partial(_attn_kernel, tq=tq, seq=seq),
        out_shape=(jax.ShapeDtypeStruct((dil, seq, GROUP_WIDTH), BF16),
                   jax.ShapeDtypeStruct((dil, seq, HEAD_DIM), F32)),
        grid=(dil, seq // tq),
        in_specs=[main(0), before(1), main(1), after(1), before(2), main(2), after(2)],
        out_specs=(pl.BlockSpec((None, tq, GROUP_WIDTH), lambda rr, lb: (rr, lb, 0)),
                   pl.BlockSpec((None, tq, HEAD_DIM), lambda rr, lb: (rr, lb, 0))),
        scratch_shapes=[pltpu.VMEM((tq + 2 * r, GROUP_WIDTH), BF16),
                        pltpu.VMEM((tq + 2 * r, GROUP_WIDTH), BF16)],
        compiler_params=_params("parallel", "arbitrary"),
        name=f"banded_attn_{dil}",
    )(qkv, qkv, qkv, qkv, qkv, qkv, qkv)


def _combine_kernel(*refs):
    ng = len(DILATIONS)
    o_refs, l_refs = refs[0:2 * ng:2], refs[1:2 * ng:2]
    out_ref, os_ref, ls_ref = refs[2 * ng:]
    t = out_ref.shape[0]
    for g, dil in enumerate(DILATIONS):
        if dil == 1:
            continue
        for r in range(dil):
            dst = pl.ds(r, t // dil, stride=dil)
            ls_ref[g, dst, :] = l_refs[g][r]
            for h in range(HEADS_PER_GROUP):
                os_ref[g, h, dst, :] = o_refs[g][r, :, _head_cols(h)].astype(F32)

    def lse_of(g):
        return l_refs[g][0] if DILATIONS[g] == 1 else ls_ref[g]

    top = functools.reduce(jnp.maximum, [lse_of(g) for g in range(ng)])
    e = [jnp.exp(lse_of(g) - top) for g in range(ng)]
    inv = 1.0 / functools.reduce(lambda a, b: a + b, e)
    w = [eg * inv for eg in e]
    for h in range(HEADS_PER_GROUP):
        acc = None
        for g, dil in enumerate(DILATIONS):
            og = o_refs[g][0, :, _head_cols(h)].astype(F32) if dil == 1 else os_ref[g, h]
            term = w[g][:, h * LSE_LANES:h * LSE_LANES + 1] * og
            acc = term if acc is None else acc + term
        out_ref[:, _head_cols(h)] = acc.astype(out_ref.dtype)


def _combine(outs):
    ng = len(DILATIONS)
    s = outs[0][0].shape[0] * outs[0][0].shape[1]
    t = _tile(s, 512)
    args, in_specs = [], []
    for (o, l), dil in zip(outs, DILATIONS):
        args += [o, l]
        in_specs += [pl.BlockSpec((dil, t // dil, GROUP_WIDTH), lambda i: (0, i, 0)),
                     pl.BlockSpec((dil, t // dil, HEAD_DIM), lambda i: (0, i, 0))]
    return pl.pallas_call(
        _combine_kernel,
        out_shape=jax.ShapeDtypeStruct((s, GROUP_WIDTH), BF16),
        grid=(s // t,),
        in_specs=in_specs,
        out_specs=pl.BlockSpec((t, GROUP_WIDTH), lambda i: (i, 0)),
        scratch_shapes=[pltpu.VMEM((ng, HEADS_PER_GROUP, t, HEAD_DIM), F32),
                        pltpu.VMEM((ng, t, HEAD_DIM), F32)],
        compiler_params=_params("parallel"),
        name="attn_combine",
    )(*args)


def _dft_tables(s, cg):
    n1, n2 = DFT_ROWS, s // DFT_ROWS

    def cs(rows, cols, period):
        ang = 2.0 * np.pi * ((np.arange(rows)[:, None] * np.arange(cols)[None, :]) % period) / period
        return np.cos(ang), np.sin(ang)

    c1, s1 = cs(n1, n1, n1)
    c2, s2 = cs(n2, n2, n2)
    ct, st = cs(n1, n2, s)
    cc, sc = cs(cg, cg, cg)
    norm = 1.0 / math.sqrt(s * cg)
    as_bf16 = lambda a: jnp.asarray(a, F32).astype(BF16)
    return dict(
        w1=as_bf16(np.concatenate([c1, -s1], axis=0)),
        tw_cos=jnp.asarray(ct, F32), tw_sin=jnp.asarray(st, F32),
        w2_re=as_bf16(np.concatenate([c2, -s2], axis=0)),
        w2_im=as_bf16(np.concatenate([s2, c2], axis=0)),
        wc_re=as_bf16(cc * norm), wc_im=as_bf16(sc * norm))


def _dft_stage1_kernel(x_ref, w1_ref, twc_ref, tws_ref, br_ref, bi_ref):
    n1, slab, _ = x_ref.shape
    lane = lax.broadcasted_iota(jnp.int32, twc_ref.shape, 1)
    for b in range(slab):
        n2 = pl.program_id(0) * slab + b
        a = jnp.dot(w1_ref[...], x_ref[:, b, :].astype(BF16), preferred_element_type=F32)
        ar, ai = a[:n1], a[n1:]
        c = jnp.sum(jnp.where(lane == n2, twc_ref[...], 0.0), axis=1, keepdims=True)
        sn = jnp.sum(jnp.where(lane == n2, tws_ref[...], 0.0), axis=1, keepdims=True)
        br_ref[:, b, :] = ar * c + ai * sn
        bi_ref[:, b, :] = ai * c - ar * sn


def _dft_stage2_kernel(br_ref, bi_ref, w2r_ref, w2i_ref, wcr_ref, wci_ref, o_ref):
    slab, n2, _ = br_ref.shape
    cg = wcr_ref.shape[0]
    for kk in range(slab):
        z = (jnp.dot(w2r_ref[...], br_ref[kk].astype(BF16), preferred_element_type=F32)
             + jnp.dot(w2i_ref[...], bi_ref[kk].astype(BF16), preferred_element_type=F32))
        zr, zi = z[:n2].astype(BF16), z[n2:].astype(BF16)
        for g in range(FOURIER_GROUPS):
            cols = slice(g * cg, (g + 1) * cg)
            o_ref[:, kk, cols] = (jnp.dot(zr[:, cols], wcr_ref[...], preferred_element_type=F32)
                                  + jnp.dot(zi[:, cols], wci_ref[...], preferred_element_type=F32))


def _fourier_mix(f):
    s, width = f.shape
    n1, n2 = DFT_ROWS, s // DFT_ROWS
    cg = width // FOURIER_GROUPS
    t = _dft_tables(s, cg)
    slab = DFT_SLAB
    full = lambda a: pl.BlockSpec(a.shape, lambda i: (0,) * a.ndim)
    br, bi = pl.pallas_call(
        _dft_stage1_kernel,
        out_shape=(jax.ShapeDtypeStruct((n1, n2, width), F32),) * 2,
        grid=(n2 // slab,),
        in_specs=[pl.BlockSpec((n1, slab, width), lambda i: (0, i, 0)),
                  full(t["w1"]), full(t["tw_cos"]), full(t["tw_sin"])],
        out_specs=(pl.BlockSpec((n1, slab, width), lambda i: (0, i, 0)),) * 2,
        compiler_params=_params("parallel"),
        name="dft_stage1",
    )(f.reshape(n1, n2, width), t["w1"], t["tw_cos"], t["tw_sin"])
    out = pl.pallas_call(
        _dft_stage2_kernel,
        out_shape=jax.ShapeDtypeStruct((n2, n1, width), F32),
        grid=(n1 // slab,),
        in_specs=[pl.BlockSpec((slab, n2, width), lambda i: (i, 0, 0)),
                  pl.BlockSpec((slab, n2, width), lambda i: (i, 0, 0)),
                  full(t["w2_re"]), full(t["w2_im"]), full(t["wc_re"]), full(t["wc_im"])],
        out_specs=pl.BlockSpec((n2, slab, width), lambda i: (0, i, 0)),
        compiler_params=_params("parallel"),
        name="dft_stage2",
    )(br, bi, t["w2_re"], t["w2_im"], t["wc_re"], t["wc_im"])
    return out.reshape(s, width)


def _merge_kernel(a_ref, f_ref, wa_ref, wf_ref, ga_ref, gf_ref, o_ref):
    ab = jnp.dot(a_ref[...], wa_ref[...], preferred_element_type=F32)
    fb = jnp.dot(f_ref[...].astype(BF16), wf_ref[...], preferred_element_type=F32)
    o_ref[...] = (ga_ref[...].astype(F32) * ab + gf_ref[...].astype(F32) * fb).astype(o_ref.dtype)


def _merge(attn, four, wa, wf, ga, gf):
    s, ka = attn.shape
    kf = four.shape[1]
    n = wa.shape[1]
    tm, tn = _tile(s, 512), _tile(n, 1024)
    return pl.pallas_call(
        _merge_kernel,
        out_shape=jax.ShapeDtypeStruct((s, n), BF16),
        grid=(s // tm, n // tn),
        in_specs=[pl.BlockSpec((tm, ka), lambda i, j: (i, 0)),
                  pl.BlockSpec((tm, kf), lambda i, j: (i, 0)),
                  pl.BlockSpec((ka, tn), lambda i, j: (0, j)),
                  pl.BlockSpec((kf, tn), lambda i, j: (0, j)),
                  pl.BlockSpec((tm, tn), lambda i, j: (i, j)),
                  pl.BlockSpec((tm, tn), lambda i, j: (i, j))],
        out_specs=pl.BlockSpec((tm, tn), lambda i, j: (i, j)),
        compiler_params=_params("parallel", "arbitrary"),
        name="branch_merge",
    )(attn, four, wa, wf, ga, gf)


def _proj_res_kernel(a_ref, w_ref, res_ref, gate_ref, o_ref, *, alpha):
    acc = jnp.dot(a_ref[...], w_ref[...].astype(BF16), preferred_element_type=F32)
    o_ref[...] = alpha * res_ref[...] + gate_ref[...] * acc


def _proj_res(a, w, res, gate, alpha, tm_want, tn_want):
    s, kdim = a.shape
    n = w.shape[1]
    tm, tn = _tile(s, tm_want), _tile(n, tn_want)
    return pl.pallas_call(
        functools.partial(_proj_res_kernel, alpha=alpha),
        out_shape=jax.ShapeDtypeStruct((s, n), F32),
        grid=(s // tm, n // tn),
        in_specs=[pl.BlockSpec((tm, kdim), lambda i, j: (i, 0)),
                  pl.BlockSpec((kdim, tn), lambda i, j: (0, j)),
                  pl.BlockSpec((tm, tn), lambda i, j: (i, j)),
                  pl.BlockSpec((1, tn), lambda i, j: (0, j))],
        out_specs=pl.BlockSpec((tm, tn), lambda i, j: (i, j)),
        compiler_params=_params("parallel", "arbitrary"),
        name="proj_res",
    )(a, w, res, gate)


def _ln_out_kernel(t_ref, g_ref, b_ref, *refs):
    y = _normalize(t_ref[...]) * g_ref[...] + b_ref[...]
    if len(refs) == 1:
        refs[0][...] = y
    else:
        mod_ref, y_ref, h_ref = refs
        y_ref[...] = y
        h_ref[...] = (_normalize(y) * (1.0 + mod_ref[1:2, :]) + mod_ref[0:1, :]).astype(h_ref.dtype)


def _ln_out(t, g, b, mod3=None):
    s, d = t.shape
    tm = _tile(s, 256)
    row = pl.BlockSpec((tm, d), lambda i: (i, 0))
    vec = lambda rows: pl.BlockSpec((rows, d), lambda i: (0, 0))
    args, in_specs = [t, g.reshape(1, d), b.reshape(1, d)], [row, vec(1), vec(1)]
    out_shape, out_specs = [jax.ShapeDtypeStruct((s, d), F32)], [row]
    if mod3 is not None:
        args.append(mod3)
        in_specs.append(vec(3))
        out_shape.append(jax.ShapeDtypeStruct((s, d), BF16))
        out_specs.append(row)
    return pl.pallas_call(
        _ln_out_kernel,
        out_shape=tuple(out_shape),
        grid=(s // tm,),
        in_specs=in_specs,
        out_specs=tuple(out_specs),
        compiler_params=_params("parallel"),
        name="ln_out",
    )(*args)


def _swiglu_kernel(x_ref, wg_ref, wu_ref, o_ref):
    x = x_ref[...]
    gpre = jnp.dot(x, wg_ref[...].astype(BF16), preferred_element_type=F32)
    up = jnp.dot(x, wu_ref[...].astype(BF16), preferred_element_type=F32)
    o_ref[...] = (gpre * _sigmoid(gpre) * up).astype(o_ref.dtype)


def _swiglu(h, wg, wu):
    s, kdim = h.shape
    n = wg.shape[1]
    tm, tn = _tile(s, 1024), _tile(n, 256)
    return pl.pallas_call(
        _swiglu_kernel,
        out_shape=jax.ShapeDtypeStruct((s, n), BF16),
        grid=(s // tm, n // tn),
        in_specs=[pl.BlockSpec((tm, kdim), lambda i, j: (i, 0)),
                  pl.BlockSpec((kdim, tn), lambda i, j: (0, j)),
                  pl.BlockSpec((kdim, tn), lambda i, j: (0, j))],
        out_specs=pl.BlockSpec((tm, tn), lambda i, j: (i, j)),
        compiler_params=_params("parallel", "arbitrary"),
        name="swiglu",
    )(h, wg, wu)


def _layer(x, mod6, wts, alpha):
    s, d = x.shape
    h = _ln_mod(x, mod6[0:3])

    cos, sin = _rope_tables(s)
    outs = [_attn_group(_qkv_proj(h, wts["w_qkv"], gi, cos, sin)) for gi in range(len(DILATIONS))]
    attn = _combine(outs)

    fw = d // FOURIER_WIDTH_DIVISOR
    f = _proj(h, wts["w_in"], 3 * ATTN_WIDTH, fw, False, F32)
    ga = _proj(h, wts["w_in"], 3 * ATTN_WIDTH + fw, d, True, BF16)
    gf = _proj(h, wts["w_in"], 3 * ATTN_WIDTH + fw + d, d, True, BF16)
    four = _fourier_mix(f)

    merged = _merge(attn, four, wts["w_attn_up"], wts["w_fourier_up"], ga, gf)
    t1 = _proj_res(merged, wts["w_mix_out"], x, mod6[2:3], alpha, 1024, 512)
    x1, h2 = _ln_out(t1, wts["ln1_g"], wts["ln1_b"], mod6[3:6])
    u = _swiglu(h2, wts["w_gate"], wts["w_up"])
    t2 = _proj_res(u, wts["w_down"], x1, mod6[5:6], alpha, 512, 512)
    (y,) = _ln_out(t2, wts["ln2_g"], wts["ln2_b"])
    return y


def kernel(x_prompt, x_sample, c_prompt, c_sample, w_ada, b_ada, w_in, w_attn_up, w_fourier_up,
           w_mix_out, ln1_g, ln1_b, w_gate, w_up, w_down, ln2_g, ln2_b):
    depth = w_ada.shape[0]
    d = x_prompt.shape[-1]
    alpha = (2.0 * depth) ** 0.25
    xs = [x_prompt[0], x_sample[0]]
    c2 = jnp.concatenate([c_prompt, c_sample], axis=0)
    for l in range(depth):
        mod = _ada_mod(c2, w_ada[l], b_ada[l]).reshape(2, N_MOD, d)
        wts = dict(w_qkv=w_in[l][:, :3 * ATTN_WIDTH].astype(BF16), w_in=w_in[l],
                   w_attn_up=w_attn_up[l].astype(BF16), w_fourier_up=w_fourier_up[l].astype(BF16),
                   w_mix_out=w_mix_out[l], ln1_g=ln1_g[l], ln1_b=ln1_b[l],
                   w_gate=w_gate[l], w_up=w_up[l],
                   w_down=w_down[l].astype(BF16), ln2_g=ln2_g[l], ln2_b=ln2_b[l])
        xs = [_layer(xs[g], mod[g], wts, alpha) for g in range(2)]
    return (xs[0][None], xs[1][None])
```

```python
import functools
import math

import numpy as np
import jax
import jax.numpy as jnp
from jax import lax
from jax.experimental import pallas as pl
from jax.experimental.pallas import tpu as pltpu

HEAD_DIM = 128
HEADS_PER_GROUP = 8
GROUP_WIDTH = HEADS_PER_GROUP * HEAD_DIM
DILATIONS = (1, 4, 16)
ATTN_RADIUS = 64
ATTN_WIDTH = len(DILATIONS) * GROUP_WIDTH
FOURIER_WIDTH_DIVISOR = 4
FOURIER_GROUPS = 4
DFT_ROWS = 128
DFT_SLAB = 8
N_MOD = 6
ROPE_THETA = 10000.0
LN_EPS = 1e-5
NEG_INF = -1e30
Q_SUB = 128
LSE_LANES = HEAD_DIM // HEADS_PER_GROUP

VMEM_LIMIT_BYTES = 56 * 1024 * 1024

F32 = jnp.float32
BF16 = jnp.bfloat16


def _params(*sem):
    return pltpu.CompilerParams(dimension_semantics=sem, vmem_limit_bytes=VMEM_LIMIT_BYTES)


def _tile(n, want):
    t = min(n, want)
    while n % t:
        t //= 2
    return t


def _head_cols(h):
    return slice(h * HEAD_DIM, (h + 1) * HEAD_DIM)


def _sigmoid(x):
    return 0.5 * jnp.tanh(0.5 * x) + 0.5


SIDE_MIN_ROWS = 16


def _side_job(body, rows_arg, out_dtypes, const_args):
    def build(n_steps, step_of):
        s, d = rows_arg.shape
        sr = SIDE_MIN_ROWS
        while s // sr > n_steps:
            sr *= 2
        last = s // sr - 1
        row = pl.BlockSpec((sr, d), lambda *idx: (jnp.minimum(step_of(*idx), last), 0))
        consts = [pl.BlockSpec(c.shape, lambda *idx: (0, 0)) for c in const_args]
        return dict(body=body, args=[rows_arg, *const_args], in_specs=[row] + consts,
                    out_shape=[jax.ShapeDtypeStruct((s, d), dt) for dt in out_dtypes],
                    out_specs=[row] * len(out_dtypes))
    return build


def _host_call(body, grid, in_specs, out_specs, out_shape, scratch, args, sem, name, side):
    n_in, n_out = len(in_specs), len(out_shape)
    if side is None:
        outs = pl.pallas_call(body, out_shape=tuple(out_shape), grid=grid, in_specs=in_specs,
                              out_specs=tuple(out_specs), scratch_shapes=scratch,
                              compiler_params=_params(*sem), name=name)(*args)
        return tuple(outs), ()

    def step_of(*idx):
        step = idx[0]
        for axis in range(1, len(grid)):
            step = step * grid[axis] + idx[axis]
        return step

    job = side(math.prod(grid), step_of)
    ns_in, ns_out = len(job["in_specs"]), len(job["out_shape"])

    def fused(*refs):
        a, b, c, e = n_in, n_in + ns_in, n_in + ns_in + n_out, n_in + ns_in + n_out + ns_out
        body(*refs[:a], *refs[b:c], *refs[e:])
        job["body"](*refs[a:b], *refs[c:e])

    outs = pl.pallas_call(fused, out_shape=tuple(out_shape) + tuple(job["out_shape"]), grid=grid,
                          in_specs=list(in_specs) + job["in_specs"],
                          out_specs=tuple(out_specs) + tuple(job["out_specs"]), scratch_shapes=scratch,
                          compiler_params=_params(*("arbitrary",) * len(grid)), name=name)(*args, *job["args"])
    return tuple(outs[:n_out]), tuple(outs[n_out:])


def _ada_kernel(ct_ref, w_ref, b_ref, o_ref, sb_ref):
    kdim, tn = w_ref.shape
    nt = tn // 128

    @pl.when(pl.program_id(0) == 0)
    def _():
        c = ct_ref[...]
        s = c * jax.nn.sigmoid(c)
        sb_ref[0] = jnp.broadcast_to(s[:, 0:1], (kdim, 128))
        sb_ref[1] = jnp.broadcast_to(s[:, 1:2], (kdim, 128))

    def body(kc, acc):
        r0 = pl.multiple_of(kc * 8, 8)
        s0 = sb_ref[0, pl.ds(r0, 8), :]
        s1 = sb_ref[1, pl.ds(r0, 8), :]
        out = []
        for t in range(nt):
            w = w_ref[pl.ds(r0, 8), t * 128:(t + 1) * 128]
            out.append(acc[2 * t] + w * s0)
            out.append(acc[2 * t + 1] + w * s1)
        return tuple(out)

    zero = jnp.zeros((8, 128), F32)
    acc = lax.fori_loop(0, kdim // 8, body, (zero,) * (2 * nt), unroll=8)
    for t in range(nt):
        cols = slice(t * 128, (t + 1) * 128)
        o_ref[0:1, cols] = jnp.sum(acc[2 * t], axis=0, keepdims=True) + b_ref[:, cols]
        o_ref[1:2, cols] = jnp.sum(acc[2 * t + 1], axis=0, keepdims=True) + b_ref[:, cols]


def _ada_mod(c2, w, b):
    kdim, n = w.shape
    tn = _tile(n, 512)
    return pl.pallas_call(
        _ada_kernel,
        out_shape=jax.ShapeDtypeStruct((2, n), F32),
        grid=(n // tn,),
        in_specs=[pl.BlockSpec((kdim, 2), lambda j: (0, 0)),
                  pl.BlockSpec((kdim, tn), lambda j: (0, j)),
                  pl.BlockSpec((1, tn), lambda j: (0, j))],
        out_specs=pl.BlockSpec((2, tn), lambda j: (0, j)),
        scratch_shapes=[pltpu.VMEM((2, kdim, 128), F32)],
        compiler_params=_params("arbitrary"),
        name="ada_mod",
    )(c2.T, w, b.reshape(1, n))


def _normalize(x):
    mu = jnp.mean(x, axis=-1, keepdims=True)
    xc = x - mu
    var = jnp.mean(xc * xc, axis=-1, keepdims=True)
    return xc * lax.rsqrt(var + LN_EPS)


def _ln_mod_kernel(x_ref, mod_ref, o_ref):
    y = _normalize(x_ref[...])
    o_ref[...] = (y * (1.0 + mod_ref[1:2, :]) + mod_ref[0:1, :]).astype(o_ref.dtype)


def _ln_mod(x, mod3):
    s, d = x.shape
    tm = _tile(s, 256)
    return pl.pallas_call(
        _ln_mod_kernel,
        out_shape=jax.ShapeDtypeStruct((s, d), BF16),
        grid=(s // tm,),
        in_specs=[pl.BlockSpec((tm, d), lambda i: (i, 0)),
                  pl.BlockSpec((3, d), lambda i: (0, 0))],
        out_specs=pl.BlockSpec((tm, d), lambda i: (i, 0)),
        compiler_params=_params("parallel"),
        name="ln_mod",
    )(x, mod3)


def _proj_kernel(x_ref, w_ref, o_ref, wb_ref, *, sigmoid):
    @pl.when(pl.program_id(1) == 0)
    def _():
        wb_ref[...] = w_ref[...].astype(BF16)

    acc = jnp.dot(x_ref[...], wb_ref[...], preferred_element_type=F32)
    if sigmoid:
        acc = _sigmoid(acc)
    o_ref[...] = acc.astype(o_ref.dtype)


def _proj(h, w, col0, ncols, sigmoid, out_dtype, side=None, tm_want=1024, tn_want=512):
    s, kdim = h.shape
    tm, tn = _tile(s, tm_want), _tile(math.gcd(col0, ncols), tn_want)
    j0 = col0 // tn
    (out,), side_out = _host_call(
        functools.partial(_proj_kernel, sigmoid=sigmoid),
        grid=(ncols // tn, s // tm),
        in_specs=[pl.BlockSpec((tm, kdim), lambda j, i: (i, 0)),
                  pl.BlockSpec((kdim, tn), lambda j, i: (0, j0 + j))],
        out_specs=[pl.BlockSpec((tm, tn), lambda j, i: (i, j))],
        out_shape=[jax.ShapeDtypeStruct((s, ncols), out_dtype)],
        scratch=[pltpu.VMEM((kdim, tn), BF16)], args=(h, w), sem=("parallel", "arbitrary"),
        name="in_proj", side=side)
    return out, side_out


def _qkv_kernel(x_ref, w_ref, cos_ref, sin_ref, o_ref, *scratch, dil, q_scale):
    j = pl.program_id(1)
    acc = jnp.dot(x_ref[...], w_ref[...], preferred_element_type=F32)
    tm = acc.shape[0]

    scale = jnp.where(j == 0, q_scale, 1.0)
    cos = jnp.where(j < 2, cos_ref[...] * scale, 1.0)
    sin = jnp.where(j < 2, sin_ref[...] * scale, 0.0)
    for h in range(HEADS_PER_GROUP):
        t = acc[:, _head_cols(h)]
        t = t * cos + pltpu.roll(t, HEAD_DIM // 2, 1) * sin
        if dil == 1:
            o_ref[0, :, _head_cols(h)] = t.astype(o_ref.dtype)
        else:
            scratch[0][h] = t

    if dil > 1:
        for r in range(dil):
            for h in range(HEADS_PER_GROUP):
                rows = scratch[0][h, pl.ds(r, tm // dil, stride=dil), :]
                o_ref[r, :, _head_cols(h)] = rows.astype(o_ref.dtype)


def _qkv_proj(h, w_qkv, gi, cos, sin):
    s, kdim = h.shape
    dil = DILATIONS[gi]
    tm = _tile(s, 1024)
    ncol = ATTN_WIDTH // GROUP_WIDTH
    scratch = [] if dil == 1 else [pltpu.VMEM((HEADS_PER_GROUP, tm, HEAD_DIM), F32)]
    (out,), _ = _host_call(
        functools.partial(_qkv_kernel, dil=dil, q_scale=HEAD_DIM ** -0.5),
        grid=(s // tm, 3),
        in_specs=[pl.BlockSpec((tm, kdim), lambda i, j: (i, 0)),
                  pl.BlockSpec((kdim, GROUP_WIDTH), lambda i, j: (0, ncol * j + gi)),
                  pl.BlockSpec((tm, HEAD_DIM), lambda i, j: (i, 0)),
                  pl.BlockSpec((tm, HEAD_DIM), lambda i, j: (i, 0))],
        out_specs=[pl.BlockSpec((None, dil, tm // dil, GROUP_WIDTH), lambda i, j: (j, 0, i, 0))],
        out_shape=[jax.ShapeDtypeStruct((3, dil, s // dil, GROUP_WIDTH), BF16)],
        scratch=scratch, args=(h, w_qkv, cos, sin), sem=("parallel", "arbitrary"),
        name=f"qkv_proj_{dil}", side=None)
    return out


def _rope_tables(s):
    half = HEAD_DIM // 2
    inv = ROPE_THETA ** (-jnp.arange(half, dtype=F32) / half)
    ang = jnp.arange(s).astype(F32)[:, None] * inv[None, :]
    cos, sin = jnp.cos(ang), jnp.sin(ang)
    return jnp.concatenate([cos, cos], axis=-1), jnp.concatenate([-sin, sin], axis=-1)


def _attn_kernel(q_ref, kp_ref, kc_ref, kn_ref, vp_ref, vc_ref, vn_ref, o_ref, l_ref, kw_ref, vw_ref,
                 *, tq, seq):
    r = ATTN_RADIUS
    kw_ref[0:r, :] = kp_ref[...]
    kw_ref[r:r + tq, :] = kc_ref[...]
    kw_ref[r + tq:, :] = kn_ref[...]
    vw_ref[0:r, :] = vp_ref[...]
    vw_ref[r:r + tq, :] = vc_ref[...]
    vw_ref[r + tq:, :] = vn_ref[...]

    base = pl.program_id(1) * tq
    nkeys = Q_SUB + 2 * r
    qi = lax.broadcasted_iota(jnp.int32, (Q_SUB, nkeys), 0)
    kj = lax.broadcasted_iota(jnp.int32, (Q_SUB, nkeys), 1)
    lane_head = lax.broadcasted_iota(jnp.int32, (Q_SUB, HEAD_DIM), 1) // LSE_LANES

    for sb in range(tq // Q_SUB):
        rows = slice(sb * Q_SUB, (sb + 1) * Q_SUB)
        first = base + sb * Q_SUB - r
        lo = jnp.maximum(qi, -first)
        hi = jnp.minimum(qi + 2 * r, seq - 1 - first)
        keep = (kj >= lo) & (kj <= hi)
        lse_tile = jnp.zeros((Q_SUB, HEAD_DIM), F32)
        for h in range(HEADS_PER_GROUP):
            cols = _head_cols(h)
            qs = q_ref[rows, cols]
            ks = kw_ref[sb * Q_SUB:sb * Q_SUB + nkeys, cols]
            vs = vw_ref[sb * Q_SUB:sb * Q_SUB + nkeys, cols]
            sc = lax.dot_general(qs, ks, (((1,), (1,)), ((), ())), preferred_element_type=F32)
            sc = jnp.where(keep, sc, NEG_INF)
            m = jnp.max(sc, axis=-1, keepdims=True)
            p = jnp.exp(sc - m)
            den = jnp.sum(p, axis=-1, keepdims=True)
            o = jnp.dot(p.astype(BF16), vs, preferred_element_type=F32) / den
            o_ref[rows, cols] = o.astype(o_ref.dtype)
            lse_tile = jnp.where(lane_head == h, m + jnp.log(den), lse_tile)
        l_ref[rows, :] = lse_tile


def _attn_group(qkv):
    _, dil, seq, _ = qkv.shape
    tq = _tile(seq, 512)
    r = ATTN_RADIUS
    halo_per_tile = tq // r
    n_halo = seq // r

    def main(which):
        return pl.BlockSpec((None, None, tq, GROUP_WIDTH), lambda rr, lb: (which, rr, lb, 0))

    def before(which):
        return pl.BlockSpec((None, None, r, GROUP_WIDTH),
                            lambda rr, lb: (which, rr, jnp.maximum(lb * halo_per_tile - 1, 0), 0))

    def after(which):
        return pl.BlockSpec((None, None, r, GROUP_WIDTH),
                            lambda rr, lb: (which, rr, jnp.minimum((lb + 1) * halo_per_tile, n_halo - 1), 0))

    return pl.pallas_call(
        functools.partial(_attn_kernel, tq=tq, seq=seq),
        out_shape=(jax.ShapeDtypeStruct((dil, seq, GROUP_WIDTH), BF16),
                   jax.ShapeDtypeStruct((dil, seq, HEAD_DIM), F32)),
        grid=(dil, seq // tq),
        in_specs=[main(0), before(1), main(1), after(1), before(2), main(2), after(2)],
        out_specs=(pl.BlockSpec((None, tq, GROUP_WIDTH), lambda rr, lb: (rr, lb, 0)),
                   pl.BlockSpec((None, tq, HEAD_DIM), lambda rr, lb: (rr, lb, 0))),
        scratch_shapes=[pltpu.VMEM((tq + 2 * r, GROUP_WIDTH), BF16),
                        pltpu.VMEM((tq + 2 * r, GROUP_WIDTH), BF16)],
        compiler_params=_params("parallel", "arbitrary"),
        name=f"banded_attn_{dil}",
    )(qkv, qkv, qkv, qkv, qkv, qkv, qkv)


def _combine_kernel(*refs):
    ng = len(DILATIONS)
    o_refs, l_refs = refs[0:2 * ng:2], refs[1:2 * ng:2]
    out_ref, os_ref, ls_ref = refs[2 * ng:]
    t = out_ref.shape[0]
    for g, dil in enumerate(DILATIONS):
        if dil == 1:
            continue
        for r in range(dil):
            dst = pl.ds(r, t // dil, stride=dil)
            ls_ref[g, dst, :] = l_refs[g][r]
            for h in range(HEADS_PER_GROUP):
                os_ref[g, h, dst, :] = o_refs[g][r, :, _head_cols(h)].astype(F32)

    def lse_of(g):
        return l_refs[g][0] if DILATIONS[g] == 1 else ls_ref[g]

    top = functools.reduce(jnp.maximum, [lse_of(g) for g in range(ng)])
    e = [jnp.exp(lse_of(g) - top) for g in range(ng)]
    inv = 1.0 / functools.reduce(lambda a, b: a + b, e)
    w = [eg * inv for eg in e]
    for h in range(HEADS_PER_GROUP):
        acc = None
        for g, dil in enumerate(DILATIONS):
            og = o_refs[g][0, :, _head_cols(h)].astype(F32) if dil == 1 else os_ref[g, h]
            term = w[g][:, h * LSE_LANES:h * LSE_LANES + 1] * og
            acc = term if acc is None else acc + term
        out_ref[:, _head_cols(h)] = acc.astype(out_ref.dtype)


def _combine(outs):
    ng = len(DILATIONS)
    s = outs[0][0].shape[0] * outs[0][0].shape[1]
    t = _tile(s, 512)
    args, in_specs = [], []
    for (o, l), dil in zip(outs, DILATIONS):
        args += [o, l]
        in_specs += [pl.BlockSpec((dil, t // dil, GROUP_WIDTH), lambda i: (0, i, 0)),
                     pl.BlockSpec((dil, t // dil, HEAD_DIM), lambda i: (0, i, 0))]
    return pl.pallas_call(
        _combine_kernel,
        out_shape=jax.ShapeDtypeStruct((s, GROUP_WIDTH), BF16),
        grid=(s // t,),
        in_specs=in_specs,
        out_specs=pl.BlockSpec((t, GROUP_WIDTH), lambda i: (i, 0)),
        scratch_shapes=[pltpu.VMEM((ng, HEADS_PER_GROUP, t, HEAD_DIM), F32),
                        pltpu.VMEM((ng, t, HEAD_DIM), F32)],
        compiler_params=_params("parallel"),
        name="attn_combine",
    )(*args)


def _dft_tables(s, cg):
    n1, n2 = DFT_ROWS, s // DFT_ROWS

    def cs(rows, cols, period):
        ang = 2.0 * np.pi * ((np.arange(rows)[:, None] * np.arange(cols)[None, :]) % period) / period
        return np.cos(ang), np.sin(ang)

    c1, s1 = cs(n1, n1, n1)
    c2, s2 = cs(n2, n2, n2)
    ct, st = cs(n1, n2, s)
    cc, sc = cs(cg, cg, cg)
    norm = 1.0 / math.sqrt(s * cg)
    as_bf16 = lambda a: jnp.asarray(a, F32).astype(BF16)
    return dict(
        w1=as_bf16(np.concatenate([c1, -s1], axis=0)),
        tw_cos=jnp.asarray(ct, F32), tw_sin=jnp.asarray(st, F32),
        w2_re=as_bf16(np.concatenate([c2, -s2], axis=0)),
        w2_im=as_bf16(np.concatenate([s2, c2], axis=0)),
        wc_re=as_bf16(cc * norm), wc_im=as_bf16(sc * norm))


def _dft_stage1_kernel(x_ref, w1_ref, twc_ref, tws_ref, br_ref, bi_ref):
    n1, slab, _ = x_ref.shape
    lane = lax.broadcasted_iota(jnp.int32, twc_ref.shape, 1)
    for b in range(slab):
        n2 = pl.program_id(0) * slab + b
        a = jnp.dot(w1_ref[...], x_ref[:, b, :].astype(BF16), preferred_element_type=F32)
        ar, ai = a[:n1], a[n1:]
        c = jnp.sum(jnp.where(lane == n2, twc_ref[...], 0.0), axis=1, keepdims=True)
        sn = jnp.sum(jnp.where(lane == n2, tws_ref[...], 0.0), axis=1, keepdims=True)
        br_ref[:, b, :] = ar * c + ai * sn
        bi_ref[:, b, :] = ai * c - ar * sn


def _dft_stage2_kernel(br_ref, bi_ref, w2r_ref, w2i_ref, wcr_ref, wci_ref, o_ref):
    slab, n2, _ = br_ref.shape
    cg = wcr_ref.shape[0]
    for kk in range(slab):
        z = (jnp.dot(w2r_ref[...], br_ref[kk].astype(BF16), preferred_element_type=F32)
             + jnp.dot(w2i_ref[...], bi_ref[kk].astype(BF16), preferred_element_type=F32))
        zr, zi = z[:n2].astype(BF16), z[n2:].astype(BF16)
        for g in range(FOURIER_GROUPS):
            cols = slice(g * cg, (g + 1) * cg)
            o_ref[:, kk, cols] = (jnp.dot(zr[:, cols], wcr_ref[...], preferred_element_type=F32)
                                  + jnp.dot(zi[:, cols], wci_ref[...], preferred_element_type=F32))


def _fourier_mix(f):
    s, width = f.shape
    n1, n2 = DFT_ROWS, s // DFT_ROWS
    cg = width // FOURIER_GROUPS
    t = _dft_tables(s, cg)
    slab = DFT_SLAB
    full = lambda a: pl.BlockSpec(a.shape, lambda i: (0,) * a.ndim)
    br, bi = pl.pallas_call(
        _dft_stage1_kernel,
        out_shape=(jax.ShapeDtypeStruct((n1, n2, width), F32),) * 2,
        grid=(n2 // slab,),
        in_specs=[pl.BlockSpec((n1, slab, width), lambda i: (0, i, 0)),
                  full(t["w1"]), full(t["tw_cos"]), full(t["tw_sin"])],
        out_specs=(pl.BlockSpec((n1, slab, width), lambda i: (0, i, 0)),) * 2,
        compiler_params=_params("parallel"),
        name="dft_stage1",
    )(f.reshape(n1, n2, width), t["w1"], t["tw_cos"], t["tw_sin"])
    out = pl.pallas_call(
        _dft_stage2_kernel,
        out_shape=jax.ShapeDtypeStruct((n2, n1, width), F32),
        grid=(n1 // slab,),
        in_specs=[pl.BlockSpec((slab, n2, width), lambda i: (i, 0, 0)),
                  pl.BlockSpec((slab, n2, width), lambda i: (i, 0, 0)),
                  full(t["w2_re"]), full(t["w2_im"]), full(t["wc_re"]), full(t["wc_im"])],
        out_specs=pl.BlockSpec((n2, slab, width), lambda i: (0, i, 0)),
        compiler_params=_params("parallel"),
        name="dft_stage2",
    )(br, bi, t["w2_re"], t["w2_im"], t["wc_re"], t["wc_im"])
    return out.reshape(s, width)


def _merge_kernel(a_ref, f_ref, wa_ref, wf_ref, ga_ref, gf_ref, o_ref):
    ab = jnp.dot(a_ref[...], wa_ref[...], preferred_element_type=F32)
    fb = jnp.dot(f_ref[...].astype(BF16), wf_ref[...], preferred_element_type=F32)
    o_ref[...] = (ga_ref[...].astype(F32) * ab + gf_ref[...].astype(F32) * fb).astype(o_ref.dtype)


def _merge(attn, four, wa, wf, ga, gf):
    s, ka = attn.shape
    kf = four.shape[1]
    n = wa.shape[1]
    tm, tn = _tile(s, 512), _tile(n, 1024)
    return pl.pallas_call(
        _merge_kernel,
        out_shape=jax.ShapeDtypeStruct((s, n), BF16),
        grid=(s // tm, n // tn),
        in_specs=[pl.BlockSpec((tm, ka), lambda i, j: (i, 0)),
                  pl.BlockSpec((tm, kf), lambda i, j: (i, 0)),
                  pl.BlockSpec((ka, tn), lambda i, j: (0, j)),
                  pl.BlockSpec((kf, tn), lambda i, j: (0, j)),
                  pl.BlockSpec((tm, tn), lambda i, j: (i, j)),
                  pl.BlockSpec((tm, tn), lambda i, j: (i, j))],
        out_specs=pl.BlockSpec((tm, tn), lambda i, j: (i, j)),
        compiler_params=_params("parallel", "arbitrary"),
        name="branch_merge",
    )(attn, four, wa, wf, ga, gf)


def _proj_res_kernel(a_ref, w_ref, res_ref, gate_ref, o_ref, *, alpha):
    acc = jnp.dot(a_ref[...], w_ref[...], preferred_element_type=F32)
    o_ref[...] = alpha * res_ref[...] + gate_ref[...] * acc


def _proj_res(a, w, res, gate, alpha, tm_want, tn_want):
    s, kdim = a.shape
    n = w.shape[1]
    tm, tn = _tile(s, tm_want), _tile(n, tn_want)
    return pl.pallas_call(
        functools.partial(_proj_res_kernel, alpha=alpha),
        out_shape=jax.ShapeDtypeStruct((s, n), F32),
        grid=(s // tm, n // tn),
        in_specs=[pl.BlockSpec((tm, kdim), lambda i, j: (i, 0)),
                  pl.BlockSpec((kdim, tn), lambda i, j: (0, j)),
                  pl.BlockSpec((tm, tn), lambda i, j: (i, j)),
                  pl.BlockSpec((1, tn), lambda i, j: (0, j))],
        out_specs=pl.BlockSpec((tm, tn), lambda i, j: (i, j)),
        compiler_params=_params("parallel", "arbitrary"),
        name="proj_res",
    )(a, w, res, gate)


def _ln_out_kernel(t_ref, g_ref, b_ref, *refs):
    y = _normalize(t_ref[...]) * g_ref[...] + b_ref[...]
    if len(refs) == 1:
        refs[0][...] = y
    else:
        mod_ref, y_ref, h_ref = refs
        y_ref[...] = y
        h_ref[...] = (_normalize(y) * (1.0 + mod_ref[1:2, :]) + mod_ref[0:1, :]).astype(h_ref.dtype)


def _ln_out(t, g, b, mod3=None):
    s, d = t.shape
    tm = _tile(s, 256)
    row = pl.BlockSpec((tm, d), lambda i: (i, 0))
    vec = lambda rows: pl.BlockSpec((rows, d), lambda i: (0, 0))
    args, in_specs = [t, g, b], [row, vec(1), vec(1)]
    out_shape, out_specs = [jax.ShapeDtypeStruct((s, d), F32)], [row]
    if mod3 is not None:
        args.append(mod3)
        in_specs.append(vec(3))
        out_shape.append(jax.ShapeDtypeStruct((s, d), BF16))
        out_specs.append(row)
    return pl.pallas_call(
        _ln_out_kernel,
        out_shape=tuple(out_shape),
        grid=(s // tm,),
        in_specs=in_specs,
        out_specs=tuple(out_specs),
        compiler_params=_params("parallel"),
        name="ln_out",
    )(*args)


def _swiglu_kernel(x_ref, wg_ref, wu_ref, o_ref):
    x = x_ref[...]
    gpre = jnp.dot(x, wg_ref[...].astype(BF16), preferred_element_type=F32)
    up = jnp.dot(x, wu_ref[...].astype(BF16), preferred_element_type=F32)
    o_ref[...] = (gpre * _sigmoid(gpre) * up).astype(o_ref.dtype)


def _swiglu(h, wg, wu, side=None):
    s, kdim = h.shape
    n = wg.shape[1]
    tm, tn = _tile(s, 1024), _tile(n, 256)
    (out,), side_out = _host_call(
        _swiglu_kernel,
        grid=(s // tm, n // tn),
        in_specs=[pl.BlockSpec((tm, kdim), lambda i, j: (i, 0)),
                  pl.BlockSpec((kdim, tn), lambda i, j: (0, j)),
                  pl.BlockSpec((kdim, tn), lambda i, j: (0, j))],
        out_specs=[pl.BlockSpec((tm, tn), lambda i, j: (i, j))],
        out_shape=[jax.ShapeDtypeStruct((s, n), BF16)],
        scratch=[], args=(h, wg, wu), sem=("parallel", "arbitrary"), name="swiglu", side=side)
    return out, side_out


def _mixer(x, h, mod6, wts, alpha, side=None):
    s, d = x.shape
    cos, sin = _rope_tables(s)
    fw = d // FOURIER_WIDTH_DIVISOR
    f, _ = _proj(h, wts["w_in"], 3 * ATTN_WIDTH, fw, False, F32)
    ga, side_out = _proj(h, wts["w_in"], 3 * ATTN_WIDTH + fw, d, True, BF16, side)
    gf, _ = _proj(h, wts["w_in"], 3 * ATTN_WIDTH + fw + d, d, True, BF16)
    outs = [_attn_group(_qkv_proj(h, wts["w_qkv"], gi, cos, sin)) for gi in range(len(DILATIONS))]
    attn = _combine(outs)
    four = _fourier_mix(f)

    merged = _merge(attn, four, wts["w_attn_up"], wts["w_fourier_up"], ga, gf)
    return _proj_res(merged, wts["w_mix_out"], x, mod6[2:3], alpha, 1024, 512), side_out


def kernel(x_prompt, x_sample, c_prompt, c_sample, w_ada, b_ada, w_in, w_attn_up, w_fourier_up,
           w_mix_out, ln1_g, ln1_b, w_gate, w_up, w_down, ln2_g, ln2_b):
    depth = w_ada.shape[0]
    d = x_prompt.shape[-1]
    alpha = (2.0 * depth) ** 0.25
    xp, xs = x_prompt[0], x_sample[0]
    c2 = jnp.concatenate([c_prompt, c_sample], axis=0)
    row = lambda v: v.reshape(1, d)
    for l in range(depth):
        mod = _ada_mod(c2, w_ada[l], b_ada[l]).reshape(2, N_MOD, d)
        mp, ms = mod[0], mod[1]
        wts = dict(w_qkv=w_in[l][:, :3 * ATTN_WIDTH].astype(BF16), w_in=w_in[l],
                   w_attn_up=w_attn_up[l].astype(BF16), w_fourier_up=w_fourier_up[l].astype(BF16),
                   w_mix_out=w_mix_out[l].astype(BF16))
        wg, wu, wd = w_gate[l], w_up[l], w_down[l].astype(BF16)
        g1, b1, g2, b2 = row(ln1_g[l]), row(ln1_b[l]), row(ln2_g[l]), row(ln2_b[l])

        hp = _ln_mod(xp, mp[0:3])
        t1p, (hs,) = _mixer(xp, hp, mp, wts, alpha, _side_job(_ln_mod_kernel, xs, [BF16], [ms[0:3]]))
        t1s, _ = _mixer(xs, hs, ms, wts, alpha)
        x1p, h2p = _ln_out(t1p, g1, b1, mp[3:6])
        up, (x1s, h2s) = _swiglu(h2p, wg, wu, _side_job(_ln_out_kernel, t1s, [F32, BF16], [g1, b1, ms[3:6]]))
        t2p = _proj_res(up, wd, x1p, mp[5:6], alpha, 512, 512)
        us, (xp,) = _swiglu(h2s, wg, wu, _side_job(_ln_out_kernel, t2p, [F32], [g2, b2]))
        t2s = _proj_res(us, wd, x1s, ms[5:6], alpha, 512, 512)
        (xs,) = _ln_out(t2s, g2, b2)
    return (xp[None], xs[None])
```

```python
import functools
import math

import numpy as np
import jax
import jax.numpy as jnp
from jax import lax
from jax.experimental import pallas as pl
from jax.experimental.pallas import tpu as pltpu

HEAD_DIM = 128
HEADS_PER_GROUP = 8
GROUP_WIDTH = HEADS_PER_GROUP * HEAD_DIM
DILATIONS = (1, 4, 16)
ATTN_RADIUS = 64
ATTN_WIDTH = len(DILATIONS) * GROUP_WIDTH
FOURIER_WIDTH_DIVISOR = 4
FOURIER_GROUPS = 4
DFT_ROWS = 128
DFT_SLAB = 8
N_MOD = 6
ROPE_THETA = 10000.0
LN_EPS = 1e-5
NEG_INF = -1e30
Q_SUB = 128
LSE_LANES = HEAD_DIM // HEADS_PER_GROUP

VMEM_LIMIT_BYTES = 56 * 1024 * 1024

F32 = jnp.float32
BF16 = jnp.bfloat16


def _params(*sem):
    return pltpu.CompilerParams(dimension_semantics=sem, vmem_limit_bytes=VMEM_LIMIT_BYTES)


def _tile(n, want):
    t = min(n, want)
    while n % t:
        t //= 2
    return t


def _head_cols(h):
    return slice(h * HEAD_DIM, (h + 1) * HEAD_DIM)


def _sigmoid(x):
    return 0.5 * jnp.tanh(0.5 * x) + 0.5


SIDE_MIN_ROWS = 16


def _side_job(body, rows_arg, out_dtypes, const_args):
    def build(n_steps, step_of):
        s, d = rows_arg.shape
        sr = SIDE_MIN_ROWS
        while s // sr > n_steps:
            sr *= 2
        last = s // sr - 1
        row = pl.BlockSpec((sr, d), lambda *idx: (jnp.minimum(step_of(*idx), last), 0))
        consts = [pl.BlockSpec(c.shape, lambda *idx: (0, 0)) for c in const_args]
        return dict(body=body, args=[rows_arg, *const_args], in_specs=[row] + consts,
                    out_shape=[jax.ShapeDtypeStruct((s, d), dt) for dt in out_dtypes],
                    out_specs=[row] * len(out_dtypes))
    return build


def _host_call(body, grid, in_specs, out_specs, out_shape, scratch, args, sem, name, side):
    n_in, n_out = len(in_specs), len(out_shape)
    if side is None:
        outs = pl.pallas_call(body, out_shape=tuple(out_shape), grid=grid, in_specs=in_specs,
                              out_specs=tuple(out_specs), scratch_shapes=scratch,
                              compiler_params=_params(*sem), name=name)(*args)
        return tuple(outs), ()

    def step_of(*idx):
        step = idx[0]
        for axis in range(1, len(grid)):
            step = step * grid[axis] + idx[axis]
        return step

    job = side(math.prod(grid), step_of)
    ns_in, ns_out = len(job["in_specs"]), len(job["out_shape"])

    def fused(*refs):
        a, b, c, e = n_in, n_in + ns_in, n_in + ns_in + n_out, n_in + ns_in + n_out + ns_out
        body(*refs[:a], *refs[b:c], *refs[e:])
        job["body"](*refs[a:b], *refs[c:e])

    outs = pl.pallas_call(fused, out_shape=tuple(out_shape) + tuple(job["out_shape"]), grid=grid,
                          in_specs=list(in_specs) + job["in_specs"],
                          out_specs=tuple(out_specs) + tuple(job["out_specs"]), scratch_shapes=scratch,
                          compiler_params=_params(*("arbitrary",) * len(grid)), name=name)(*args, *job["args"])
    return tuple(outs[:n_out]), tuple(outs[n_out:])


def _ada_kernel(ct_ref, w_ref, b_ref, o_ref, sb_ref):
    kdim, tn = w_ref.shape
    nt = tn // 128

    @pl.when(pl.program_id(0) == 0)
    def _():
        c = ct_ref[...]
        s = c * jax.nn.sigmoid(c)
        sb_ref[0] = jnp.broadcast_to(s[:, 0:1], (kdim, 128))
        sb_ref[1] = jnp.broadcast_to(s[:, 1:2], (kdim, 128))

    def body(kc, acc):
        r0 = pl.multiple_of(kc * 8, 8)
        s0 = sb_ref[0, pl.ds(r0, 8), :]
        s1 = sb_ref[1, pl.ds(r0, 8), :]
        out = []
        for t in range(nt):
            w = w_ref[pl.ds(r0, 8), t * 128:(t + 1) * 128]
            out.append(acc[2 * t] + w * s0)
            out.append(acc[2 * t + 1] + w * s1)
        return tuple(out)

    zero = jnp.zeros((8, 128), F32)
    acc = lax.fori_loop(0, kdim // 8, body, (zero,) * (2 * nt), unroll=8)
    for t in range(nt):
        cols = slice(t * 128, (t + 1) * 128)
        o_ref[0:1, cols] = jnp.sum(acc[2 * t], axis=0, keepdims=True) + b_ref[:, cols]
        o_ref[1:2, cols] = jnp.sum(acc[2 * t + 1], axis=0, keepdims=True) + b_ref[:, cols]


def _ada_mod(c2, w, b):
    kdim, n = w.shape
    tn = _tile(n, 512)
    return pl.pallas_call(
        _ada_kernel,
        out_shape=jax.ShapeDtypeStruct((2, n), F32),
        grid=(n // tn,),
        in_specs=[pl.BlockSpec((kdim, 2), lambda j: (0, 0)),
                  pl.BlockSpec((kdim, tn), lambda j: (0, j)),
                  pl.BlockSpec((1, tn), lambda j: (0, j))],
        out_specs=pl.BlockSpec((2, tn), lambda j: (0, j)),
        scratch_shapes=[pltpu.VMEM((2, kdim, 128), F32)],
        compiler_params=_params("arbitrary"),
        name="ada_mod",
    )(c2.T, w, b.reshape(1, n))


def _normalize(x):
    mu = jnp.mean(x, axis=-1, keepdims=True)
    xc = x - mu
    var = jnp.mean(xc * xc, axis=-1, keepdims=True)
    return xc * lax.rsqrt(var + LN_EPS)


LN_CHUNK = 16


def _row_chunks(n):
    step = min(n, LN_CHUNK)
    return [slice(r, r + step) for r in range(0, n, step)]


def _ln_mod_kernel(x_ref, mod_ref, o_ref):
    for rows in _row_chunks(x_ref.shape[0]):
        y = _normalize(x_ref[rows, :])
        o_ref[rows, :] = (y * (1.0 + mod_ref[1:2, :]) + mod_ref[0:1, :]).astype(o_ref.dtype)


def _ln_mod(x, mod3):
    s, d = x.shape
    tm = _tile(s, 256)
    return pl.pallas_call(
        _ln_mod_kernel,
        out_shape=jax.ShapeDtypeStruct((s, d), BF16),
        grid=(s // tm,),
        in_specs=[pl.BlockSpec((tm, d), lambda i: (i, 0)),
                  pl.BlockSpec((3, d), lambda i: (0, 0))],
        out_specs=pl.BlockSpec((tm, d), lambda i: (i, 0)),
        compiler_params=_params("parallel"),
        name="ln_mod",
    )(x, mod3)


def _proj_kernel(x_ref, w_ref, o_ref, wb_ref, *, sigmoid):
    @pl.when(pl.program_id(1) == 0)
    def _():
        wb_ref[...] = w_ref[...].astype(BF16)

    acc = jnp.dot(x_ref[...], wb_ref[...], preferred_element_type=F32)
    if sigmoid:
        acc = _sigmoid(acc)
    o_ref[...] = acc.astype(o_ref.dtype)


def _proj(h, w, col0, ncols, sigmoid, out_dtype, side=None, tm_want=1024, tn_want=512):
    s, kdim = h.shape
    tm, tn = _tile(s, tm_want), _tile(math.gcd(col0, ncols), tn_want)
    j0 = col0 // tn
    (out,), side_out = _host_call(
        functools.partial(_proj_kernel, sigmoid=sigmoid),
        grid=(ncols // tn, s // tm),
        in_specs=[pl.BlockSpec((tm, kdim), lambda j, i: (i, 0)),
                  pl.BlockSpec((kdim, tn), lambda j, i: (0, j0 + j))],
        out_specs=[pl.BlockSpec((tm, tn), lambda j, i: (i, j))],
        out_shape=[jax.ShapeDtypeStruct((s, ncols), out_dtype)],
        scratch=[pltpu.VMEM((kdim, tn), BF16)], args=(h, w), sem=("parallel", "arbitrary"),
        name="in_proj", side=side)
    return out, side_out


def _qkv_kernel(x_ref, w_ref, cos_ref, sin_ref, o_ref, *scratch, dil, q_scale):
    j = pl.program_id(1)
    acc = jnp.dot(x_ref[...], w_ref[...], preferred_element_type=F32)
    tm = acc.shape[0]

    scale = jnp.where(j == 0, q_scale, 1.0)
    cos = jnp.where(j < 2, cos_ref[...] * scale, 1.0)
    sin = jnp.where(j < 2, sin_ref[...] * scale, 0.0)
    for h in range(HEADS_PER_GROUP):
        t = acc[:, _head_cols(h)]
        t = t * cos + pltpu.roll(t, HEAD_DIM // 2, 1) * sin
        if dil == 1:
            o_ref[0, :, _head_cols(h)] = t.astype(o_ref.dtype)
        else:
            scratch[0][h] = t

    if dil > 1:
        for r in range(dil):
            for h in range(HEADS_PER_GROUP):
                rows = scratch[0][h, pl.ds(r, tm // dil, stride=dil), :]
                o_ref[r, :, _head_cols(h)] = rows.astype(o_ref.dtype)


def _qkv_proj(h, w_qkv, gi, cos, sin):
    s, kdim = h.shape
    dil = DILATIONS[gi]
    tm = _tile(s, 1024)
    ncol = ATTN_WIDTH // GROUP_WIDTH
    scratch = [] if dil == 1 else [pltpu.VMEM((HEADS_PER_GROUP, tm, HEAD_DIM), F32)]
    (out,), _ = _host_call(
        functools.partial(_qkv_kernel, dil=dil, q_scale=HEAD_DIM ** -0.5),
        grid=(s // tm, 3),
        in_specs=[pl.BlockSpec((tm, kdim), lambda i, j: (i, 0)),
                  pl.BlockSpec((kdim, GROUP_WIDTH), lambda i, j: (0, ncol * j + gi)),
                  pl.BlockSpec((tm, HEAD_DIM), lambda i, j: (i, 0)),
                  pl.BlockSpec((tm, HEAD_DIM), lambda i, j: (i, 0))],
        out_specs=[pl.BlockSpec((None, dil, tm // dil, GROUP_WIDTH), lambda i, j: (j, 0, i, 0))],
        out_shape=[jax.ShapeDtypeStruct((3, dil, s // dil, GROUP_WIDTH), BF16)],
        scratch=scratch, args=(h, w_qkv, cos, sin), sem=("parallel", "arbitrary"),
        name=f"qkv_proj_{dil}", side=None)
    return out


def _rope_tables(s):
    half = HEAD_DIM // 2
    inv = ROPE_THETA ** (-jnp.arange(half, dtype=F32) / half)
    ang = jnp.arange(s).astype(F32)[:, None] * inv[None, :]
    cos, sin = jnp.cos(ang), jnp.sin(ang)
    return jnp.concatenate([cos, cos], axis=-1), jnp.concatenate([-sin, sin], axis=-1)


def _attn_kernel(q_ref, kp_ref, kc_ref, kn_ref, vp_ref, vc_ref, vn_ref, o_ref, l_ref, kw_ref, vw_ref,
                 *, tq, seq):
    r = ATTN_RADIUS
    kw_ref[0:r, :] = kp_ref[...]
    kw_ref[r:r + tq, :] = kc_ref[...]
    kw_ref[r + tq:, :] = kn_ref[...]
    vw_ref[0:r, :] = vp_ref[...]
    vw_ref[r:r + tq, :] = vc_ref[...]
    vw_ref[r + tq:, :] = vn_ref[...]

    base = pl.program_id(1) * tq
    nkeys = Q_SUB + 2 * r
    qi = lax.broadcasted_iota(jnp.int32, (Q_SUB, nkeys), 0)
    kj = lax.broadcasted_iota(jnp.int32, (Q_SUB, nkeys), 1)
    lane_head = lax.broadcasted_iota(jnp.int32, (Q_SUB, HEAD_DIM), 1) // LSE_LANES

    for sb in range(tq // Q_SUB):
        rows = slice(sb * Q_SUB, (sb + 1) * Q_SUB)
        first = base + sb * Q_SUB - r
        lo = jnp.maximum(qi, -first)
        hi = jnp.minimum(qi + 2 * r, seq - 1 - first)
        keep = (kj >= lo) & (kj <= hi)
        lse_tile = jnp.zeros((Q_SUB, HEAD_DIM), F32)
        for h in range(HEADS_PER_GROUP):
            cols = _head_cols(h)
            qs = q_ref[rows, cols]
            ks = kw_ref[sb * Q_SUB:sb * Q_SUB + nkeys, cols]
            vs = vw_ref[sb * Q_SUB:sb * Q_SUB + nkeys, cols]
            sc = lax.dot_general(qs, ks, (((1,), (1,)), ((), ())), preferred_element_type=F32)
            sc = jnp.where(keep, sc, NEG_INF)
            m = jnp.max(sc, axis=-1, keepdims=True)
            p = jnp.exp(sc - m)
            den = jnp.sum(p, axis=-1, keepdims=True)
            o = jnp.dot(p.astype(BF16), vs, preferred_element_type=F32) / den
            o_ref[rows, cols] = o.astype(o_ref.dtype)
            lse_tile = jnp.where(lane_head == h, m + jnp.log(den), lse_tile)
        l_ref[rows, :] = lse_tile


def _attn_group(qkv):
    _, dil, seq, _ = qkv.shape
    tq = _tile(seq, 512)
    r = ATTN_RADIUS
    halo_per_tile = tq // r
    n_halo = seq // r

    def main(which):
        return pl.BlockSpec((None, None, tq, GROUP_WIDTH), lambda rr, lb: (which, rr, lb, 0))

    def before(which):
        return pl.BlockSpec((None, None, r, GROUP_WIDTH),
                            lambda rr, lb: (which, rr, jnp.maximum(lb * halo_per_tile - 1, 0), 0))

    def after(which):
        return pl.BlockSpec((None, None, r, GROUP_WIDTH),
                            lambda rr, lb: (which, rr, jnp.minimum((lb + 1) * halo_per_tile, n_halo - 1), 0))

    return pl.pallas_call(
        functools.partial(_attn_kernel, tq=tq, seq=seq),
        out_shape=(jax.ShapeDtypeStruct((dil, seq, GROUP_WIDTH), BF16),
                   jax.ShapeDtypeStruct((dil, seq, HEAD_DIM), F32)),
        grid=(dil, seq // tq),
        in_specs=[main(0), before(1), main(1), after(1), before(2), main(2), after(2)],
        out_specs=(pl.BlockSpec((None, tq, GROUP_WIDTH), lambda rr, lb: (rr, lb, 0)),
                   pl.BlockSpec((None, tq, HEAD_DIM), lambda rr, lb: (rr, lb, 0))),
        scratch_shapes=[pltpu.VMEM((tq + 2 * r, GROUP_WIDTH), BF16),
                        pltpu.VMEM((tq + 2 * r, GROUP_WIDTH), BF16)],
        compiler_params=_params("parallel", "arbitrary"),
        name=f"banded_attn_{dil}",
    )(qkv, qkv, qkv, qkv, qkv, qkv, qkv)


def _combine_kernel(*refs):
    ng = len(DILATIONS)
    o_refs, l_refs = refs[0:2 * ng:2], refs[1:2 * ng:2]
    out_ref, os_ref, ls_ref = refs[2 * ng:]
    t = out_ref.shape[0]
    for g, dil in enumerate(DILATIONS):
        if dil == 1:
            continue
        for r in range(dil):
            dst = pl.ds(r, t // dil, stride=dil)
            ls_ref[g, dst, :] = l_refs[g][r]
            for h in range(HEADS_PER_GROUP):
                os_ref[g, h, dst, :] = o_refs[g][r, :, _head_cols(h)].astype(F32)

    def lse_of(g):
        return l_refs[g][0] if DILATIONS[g] == 1 else ls_ref[g]

    top = functools.reduce(jnp.maximum, [lse_of(g) for g in range(ng)])
    e = [jnp.exp(lse_of(g) - top) for g in range(ng)]
    inv = 1.0 / functools.reduce(lambda a, b: a + b, e)
    w = [eg * inv for eg in e]
    for h in range(HEADS_PER_GROUP):
        acc = None
        for g, dil in enumerate(DILATIONS):
            og = o_refs[g][0, :, _head_cols(h)].astype(F32) if dil == 1 else os_ref[g, h]
            term = w[g][:, h * LSE_LANES:h * LSE_LANES + 1] * og
            acc = term if acc is None else acc + term
        out_ref[:, _head_cols(h)] = acc.astype(out_ref.dtype)


def _combine(outs):
    ng = len(DILATIONS)
    s = outs[0][0].shape[0] * outs[0][0].shape[1]
    t = _tile(s, 512)
    args, in_specs = [], []
    for (o, l), dil in zip(outs, DILATIONS):
        args += [o, l]
        in_specs += [pl.BlockSpec((dil, t // dil, GROUP_WIDTH), lambda i: (0, i, 0)),
                     pl.BlockSpec((dil, t // dil, HEAD_DIM), lambda i: (0, i, 0))]
    return pl.pallas_call(
        _combine_kernel,
        out_shape=jax.ShapeDtypeStruct((s, GROUP_WIDTH), BF16),
        grid=(s // t,),
        in_specs=in_specs,
        out_specs=pl.BlockSpec((t, GROUP_WIDTH), lambda i: (i, 0)),
        scratch_shapes=[pltpu.VMEM((ng, HEADS_PER_GROUP, t, HEAD_DIM), F32),
                        pltpu.VMEM((ng, t, HEAD_DIM), F32)],
        compiler_params=_params("parallel"),
        name="attn_combine",
    )(*args)


def _dft_tables(s, cg):
    n1, n2 = DFT_ROWS, s // DFT_ROWS

    def cs(rows, cols, period):
        ang = 2.0 * np.pi * ((np.arange(rows)[:, None] * np.arange(cols)[None, :]) % period) / period
        return np.cos(ang), np.sin(ang)

    c1, s1 = cs(n1, n1, n1)
    c2, s2 = cs(n2, n2, n2)
    ct, st = cs(n1, n2, s)
    cc, sc = cs(cg, cg, cg)
    norm = 1.0 / math.sqrt(s * cg)
    as_bf16 = lambda a: jnp.asarray(a, F32).astype(BF16)
    return dict(
        w1=as_bf16(np.concatenate([c1, -s1], axis=0)),
        tw_cos=jnp.asarray(ct, F32), tw_sin=jnp.asarray(st, F32),
        w2_re=as_bf16(np.concatenate([c2, -s2], axis=0)),
        w2_im=as_bf16(np.concatenate([s2, c2], axis=0)),
        wc_re=as_bf16(cc * norm), wc_im=as_bf16(sc * norm))


LANES = 128


def _flatten_slabs(dst_ref, src_refs):
    for k, src in enumerate(src_refs):
        n = src.shape[0]
        dst_ref[k] = src[...].reshape(n * DFT_SLAB, LANES)


def _slab(ref, k, row, n):
    return ref[k, pl.ds(row, n, stride=DFT_SLAB), :]


def _lane_tiles(x, nt):
    return [x[:, t * LANES:(t + 1) * LANES] for t in range(nt)]


def _dft_stage1_kernel(*refs, nt):
    x_refs, (w1_ref, twc_ref, tws_ref, br_ref, bi_ref, xs_ref) = refs[:nt], refs[nt:]
    n1 = w1_ref.shape[1]
    _flatten_slabs(xs_ref, x_refs)
    lane = lax.broadcasted_iota(jnp.int32, twc_ref.shape, 1)
    for b in range(DFT_SLAB):
        n2 = pl.program_id(0) * DFT_SLAB + b
        xb = jnp.concatenate([_slab(xs_ref, k, b, n1).astype(BF16) for k in range(nt)], axis=1)
        a = jnp.dot(w1_ref[...], xb, preferred_element_type=F32)
        ar, ai = a[:n1], a[n1:]
        c = jnp.sum(jnp.where(lane == n2, twc_ref[...], 0.0), axis=1, keepdims=True)
        sn = jnp.sum(jnp.where(lane == n2, tws_ref[...], 0.0), axis=1, keepdims=True)
        br_ref[b] = ar * c + ai * sn
        bi_ref[b] = ai * c - ar * sn


def _dft_stage2_kernel(*refs, nt):
    br_refs, bi_refs = refs[:nt], refs[nt:2 * nt]
    w2r_ref, w2i_ref, wcr_ref, wci_ref, o_ref, bs_ref, scr_ref = refs[2 * nt:]
    n2 = w2r_ref.shape[1]
    cg = wcr_ref.shape[0]
    _flatten_slabs(bs_ref, br_refs + bi_refs)
    for kk in range(DFT_SLAB):
        br = jnp.concatenate([_slab(bs_ref, k, kk, n2).astype(BF16) for k in range(nt)], axis=1)
        bi = jnp.concatenate([_slab(bs_ref, nt + k, kk, n2).astype(BF16) for k in range(nt)], axis=1)
        z = (jnp.dot(w2r_ref[...], br, preferred_element_type=F32)
             + jnp.dot(w2i_ref[...], bi, preferred_element_type=F32))
        zr, zi = z[:n2].astype(BF16), z[n2:].astype(BF16)
        groups = []
        for g in range(nt * LANES // cg):
            cols = slice(g * cg, (g + 1) * cg)
            groups.append(jnp.dot(zr[:, cols], wcr_ref[...], preferred_element_type=F32)
                          + jnp.dot(zi[:, cols], wci_ref[...], preferred_element_type=F32))
        out = groups[0] if len(groups) == 1 else jnp.concatenate(groups, axis=1)
        for t, tile in enumerate(_lane_tiles(out, nt)):
            scr_ref[t, pl.ds(kk, n2, stride=DFT_SLAB), :] = tile
    for t in range(nt):
        o_ref[:, :, t * LANES:(t + 1) * LANES] = scr_ref[t].reshape(n2, DFT_SLAB, LANES)


def _fourier_mix(f):
    s, width = f.shape
    n1, n2 = DFT_ROWS, s // DFT_ROWS
    cg = width // FOURIER_GROUPS
    t = _dft_tables(s, cg)
    slab = DFT_SLAB
    full = lambda a: pl.BlockSpec(a.shape, lambda i, j: (0,) * a.ndim)

    def tile_specs(rows, nt, place):
        return [pl.BlockSpec((rows, slab, LANES), functools.partial(place, t=k)) for k in range(nt)]

    nt1 = min(4, width // LANES)
    br, bi = pl.pallas_call(
        functools.partial(_dft_stage1_kernel, nt=nt1),
        out_shape=(jax.ShapeDtypeStruct((n2, n1, width), F32),) * 2,
        grid=(n2 // slab, width // (nt1 * LANES)),
        in_specs=tile_specs(n1, nt1, lambda i, j, t: (0, i, j * nt1 + t))
        + [full(t["w1"]), full(t["tw_cos"]), full(t["tw_sin"])],
        out_specs=(pl.BlockSpec((slab, n1, nt1 * LANES), lambda i, j: (i, 0, j)),) * 2,
        scratch_shapes=[pltpu.VMEM((nt1, n1 * slab, LANES), F32)],
        compiler_params=_params("parallel", "parallel"),
        name="dft_stage1",
    )(*([f.reshape(n1, n2, width)] * nt1), t["w1"], t["tw_cos"], t["tw_sin"])

    nt2 = width // LANES
    stage2_in = tile_specs(n2, nt2, lambda i, j, t: (0, i, j * nt2 + t))
    out = pl.pallas_call(
        functools.partial(_dft_stage2_kernel, nt=nt2),
        out_shape=jax.ShapeDtypeStruct((n2, n1, width), F32),
        grid=(n1 // slab, width // (nt2 * LANES)),
        in_specs=stage2_in + stage2_in
        + [full(t["w2_re"]), full(t["w2_im"]), full(t["wc_re"]), full(t["wc_im"])],
        out_specs=pl.BlockSpec((n2, slab, nt2 * LANES), lambda i, j: (0, i, j)),
        scratch_shapes=[pltpu.VMEM((2 * nt2, n2 * slab, LANES), F32),
                        pltpu.VMEM((nt2, n2 * slab, LANES), F32)],
        compiler_params=_params("parallel", "parallel"),
        name="dft_stage2",
    )(*([br] * nt2), *([bi] * nt2), t["w2_re"], t["w2_im"], t["wc_re"], t["wc_im"])
    return out.reshape(s, width)


def _merge_kernel(a_ref, f_ref, wa_ref, wf_ref, ga_ref, gf_ref, o_ref):
    ab = jnp.dot(a_ref[...], wa_ref[...], preferred_element_type=F32)
    fb = jnp.dot(f_ref[...].astype(BF16), wf_ref[...], preferred_element_type=F32)
    o_ref[...] = (ga_ref[...].astype(F32) * ab + gf_ref[...].astype(F32) * fb).astype(o_ref.dtype)


def _merge(attn, four, wa, wf, ga, gf):
    s, ka = attn.shape
    kf = four.shape[1]
    n = wa.shape[1]
    tm, tn = _tile(s, 1024), _tile(n, 1024)
    return pl.pallas_call(
        _merge_kernel,
        out_shape=jax.ShapeDtypeStruct((s, n), BF16),
        grid=(s // tm, n // tn),
        in_specs=[pl.BlockSpec((tm, ka), lambda i, j: (i, 0)),
                  pl.BlockSpec((tm, kf), lambda i, j: (i, 0)),
                  pl.BlockSpec((ka, tn), lambda i, j: (0, j)),
                  pl.BlockSpec((kf, tn), lambda i, j: (0, j)),
                  pl.BlockSpec((tm, tn), lambda i, j: (i, j)),
                  pl.BlockSpec((tm, tn), lambda i, j: (i, j))],
        out_specs=pl.BlockSpec((tm, tn), lambda i, j: (i, j)),
        compiler_params=_params("parallel", "arbitrary"),
        name="branch_merge",
    )(attn, four, wa, wf, ga, gf)


def _proj_res_kernel(a_ref, w_ref, res_ref, gate_ref, o_ref, *, alpha):
    acc = jnp.dot(a_ref[...], w_ref[...], preferred_element_type=F32)
    o_ref[...] = alpha * res_ref[...] + gate_ref[...] * acc


def _proj_res(a, w, res, gate, alpha, tm_want, tn_want):
    s, kdim = a.shape
    n = w.shape[1]
    tm, tn = _tile(s, tm_want), _tile(n, tn_want)
    return pl.pallas_call(
        functools.partial(_proj_res_kernel, alpha=alpha),
        out_shape=jax.ShapeDtypeStruct((s, n), F32),
        grid=(s // tm, n // tn),
        in_specs=[pl.BlockSpec((tm, kdim), lambda i, j: (i, 0)),
                  pl.BlockSpec((kdim, tn), lambda i, j: (0, j)),
                  pl.BlockSpec((tm, tn), lambda i, j: (i, j)),
                  pl.BlockSpec((1, tn), lambda i, j: (0, j))],
        out_specs=pl.BlockSpec((tm, tn), lambda i, j: (i, j)),
        compiler_params=_params("parallel", "arbitrary"),
        name="proj_res",
    )(a, w, res, gate)


def _ln_out_kernel(t_ref, g_ref, b_ref, *refs):
    for rows in _row_chunks(t_ref.shape[0]):
        y = _normalize(t_ref[rows, :]) * g_ref[...] + b_ref[...]
        if len(refs) == 1:
            refs[0][rows, :] = y
        else:
            mod_ref, y_ref, h_ref = refs
            y_ref[rows, :] = y
            h_ref[rows, :] = (_normalize(y) * (1.0 + mod_ref[1:2, :]) + mod_ref[0:1, :]).astype(h_ref.dtype)


def _ln_out(t, g, b, mod3=None):
    s, d = t.shape
    tm = _tile(s, 256)
    row = pl.BlockSpec((tm, d), lambda i: (i, 0))
    vec = lambda rows: pl.BlockSpec((rows, d), lambda i: (0, 0))
    args, in_specs = [t, g, b], [row, vec(1), vec(1)]
    out_shape, out_specs = [jax.ShapeDtypeStruct((s, d), F32)], [row]
    if mod3 is not None:
        args.append(mod3)
        in_specs.append(vec(3))
        out_shape.append(jax.ShapeDtypeStruct((s, d), BF16))
        out_specs.append(row)
    return pl.pallas_call(
        _ln_out_kernel,
        out_shape=tuple(out_shape),
        grid=(s // tm,),
        in_specs=in_specs,
        out_specs=tuple(out_specs),
        compiler_params=_params("parallel"),
        name="ln_out",
    )(*args)


def _swiglu_kernel(x_ref, wg_ref, wu_ref, o_ref):
    x = x_ref[...]
    gpre = jnp.dot(x, wg_ref[...].astype(BF16), preferred_element_type=F32)
    up = jnp.dot(x, wu_ref[...].astype(BF16), preferred_element_type=F32)
    o_ref[...] = (gpre * _sigmoid(gpre) * up).astype(o_ref.dtype)


def _swiglu(h, wg, wu, side=None):
    s, kdim = h.shape
    n = wg.shape[1]
    tm, tn = _tile(s, 1024), _tile(n, 256)
    (out,), side_out = _host_call(
        _swiglu_kernel,
        grid=(s // tm, n // tn),
        in_specs=[pl.BlockSpec((tm, kdim), lambda i, j: (i, 0)),
                  pl.BlockSpec((kdim, tn), lambda i, j: (0, j)),
                  pl.BlockSpec((kdim, tn), lambda i, j: (0, j))],
        out_specs=[pl.BlockSpec((tm, tn), lambda i, j: (i, j))],
        out_shape=[jax.ShapeDtypeStruct((s, n), BF16)],
        scratch=[], args=(h, wg, wu), sem=("parallel", "arbitrary"), name="swiglu", side=side)
    return out, side_out


def _mixer(x, h, mod6, wts, alpha, side=None):
    s, d = x.shape
    cos, sin = _rope_tables(s)
    fw = d // FOURIER_WIDTH_DIVISOR
    f, _ = _proj(h, wts["w_in"], 3 * ATTN_WIDTH, fw, False, F32)
    ga, side_out = _proj(h, wts["w_in"], 3 * ATTN_WIDTH + fw, d, True, BF16, side)
    gf, _ = _proj(h, wts["w_in"], 3 * ATTN_WIDTH + fw + d, d, True, BF16)
    outs = [_attn_group(_qkv_proj(h, wts["w_qkv"], gi, cos, sin)) for gi in range(len(DILATIONS))]
    attn = _combine(outs)
    four = _fourier_mix(f)

    merged = _merge(attn, four, wts["w_attn_up"], wts["w_fourier_up"], ga, gf)
    return _proj_res(merged, wts["w_mix_out"], x, mod6[2:3], alpha, 1024, 512), side_out


def kernel(x_prompt, x_sample, c_prompt, c_sample, w_ada, b_ada, w_in, w_attn_up, w_fourier_up,
           w_mix_out, ln1_g, ln1_b, w_gate, w_up, w_down, ln2_g, ln2_b):
    depth = w_ada.shape[0]
    d = x_prompt.shape[-1]
    alpha = (2.0 * depth) ** 0.25
    xp, xs = x_prompt[0], x_sample[0]
    c2 = jnp.concatenate([c_prompt, c_sample], axis=0)
    row = lambda v: v.reshape(1, d)
    for l in range(depth):
        mod = _ada_mod(c2, w_ada[l], b_ada[l]).reshape(2, N_MOD, d)
        mp, ms = mod[0], mod[1]
        wts = dict(w_qkv=w_in[l][:, :3 * ATTN_WIDTH].astype(BF16), w_in=w_in[l],
                   w_attn_up=w_attn_up[l].astype(BF16), w_fourier_up=w_fourier_up[l].astype(BF16),
                   w_mix_out=w_mix_out[l].astype(BF16))
        wg, wu, wd = w_gate[l], w_up[l], w_down[l].astype(BF16)
        g1, b1, g2, b2 = row(ln1_g[l]), row(ln1_b[l]), row(ln2_g[l]), row(ln2_b[l])

        hp = _ln_mod(xp, mp[0:3])
        t1p, (hs,) = _mixer(xp, hp, mp, wts, alpha, _side_job(_ln_mod_kernel, xs, [BF16], [ms[0:3]]))
        t1s, _ = _mixer(xs, hs, ms, wts, alpha)
        x1p, h2p = _ln_out(t1p, g1, b1, mp[3:6])
        up, (x1s, h2s) = _swiglu(h2p, wg, wu, _side_job(_ln_out_kernel, t1s, [F32, BF16], [g1, b1, ms[3:6]]))
        t2p = _proj_res(up, wd, x1p, mp[5:6], alpha, 512, 512)
        us, (xp,) = _swiglu(h2s, wg, wu, _side_job(_ln_out_kernel, t2p, [F32], [g2, b2]))
        t2s = _proj_res(us, wd, x1s, ms[5:6], alpha, 512, 512)
        (xs,) = _ln_out(t2s, g2, b2)
    return (xp[None], xs[None])
```

```python
import functools
import math

import numpy as np
import jax
import jax.numpy as jnp
from jax import lax
from jax.experimental import pallas as pl
from jax.experimental.pallas import tpu as pltpu

HEAD_DIM = 128
HEADS_PER_GROUP = 8
GROUP_WIDTH = HEADS_PER_GROUP * HEAD_DIM
DILATIONS = (1, 4, 16)
ATTN_RADIUS = 64
ATTN_WIDTH = len(DILATIONS) * GROUP_WIDTH
FOURIER_WIDTH_DIVISOR = 4
FOURIER_GROUPS = 4
DFT_ROWS = 128
DFT_SLAB = 8
N_MOD = 6
ROPE_THETA = 10000.0
LN_EPS = 1e-5
NEG_INF = -1e30
Q_SUB = 128
LSE_LANES = HEAD_DIM // HEADS_PER_GROUP

VMEM_LIMIT_BYTES = 56 * 1024 * 1024

F32 = jnp.float32
BF16 = jnp.bfloat16


def _params(*sem):
    return pltpu.CompilerParams(dimension_semantics=sem, vmem_limit_bytes=VMEM_LIMIT_BYTES)


def _tile(n, want):
    t = min(n, want)
    while n % t:
        t //= 2
    return t


def _head_cols(h):
    return slice(h * HEAD_DIM, (h + 1) * HEAD_DIM)


def _sigmoid(x):
    return 0.5 * jnp.tanh(0.5 * x) + 0.5


SIDE_MIN_ROWS = 16


def _side_job(body, rows_arg, out_dtypes, const_args, part=(0, 1), into=None):
    def build(n_steps, step_of):
        s, d = rows_arg.shape
        span = s // part[1]
        sr = SIDE_MIN_ROWS
        while span // sr > n_steps:
            sr *= 2
        first = part[0] * (span // sr)
        last = first + span // sr - 1
        row = pl.BlockSpec((sr, d), lambda *idx: (jnp.minimum(first + step_of(*idx), last), 0))
        consts = [pl.BlockSpec(c.shape, lambda *idx: (0, 0)) for c in const_args]
        return dict(body=body, args=[rows_arg, *const_args], in_specs=[row] + consts,
                    out_shape=[jax.ShapeDtypeStruct((s, d), dt) for dt in out_dtypes],
                    out_specs=[row] * len(out_dtypes), into=list(into or ()))
    return build


def _host_call(body, grid, in_specs, out_specs, out_shape, scratch, args, sem, name, side):
    n_in, n_out = len(in_specs), len(out_shape)
    if side is None:
        outs = pl.pallas_call(body, out_shape=tuple(out_shape), grid=grid, in_specs=in_specs,
                              out_specs=tuple(out_specs), scratch_shapes=scratch,
                              compiler_params=_params(*sem), name=name)(*args)
        return tuple(outs), ()

    def step_of(*idx):
        step = idx[0]
        for axis in range(1, len(grid)):
            step = step * grid[axis] + idx[axis]
        return step

    job = side(math.prod(grid), step_of)
    ns_in, ns_out, n_into = len(job["in_specs"]), len(job["out_shape"]), len(job["into"])
    a, b = n_in, n_in + ns_in
    c = b + n_into
    e, g = c + n_out, c + n_out + ns_out

    def fused(*refs):
        body(*refs[:a], *refs[c:e], *refs[g:])
        job["body"](*refs[a:b], *refs[e:g])

    outs = pl.pallas_call(fused, out_shape=tuple(out_shape) + tuple(job["out_shape"]), grid=grid,
                          in_specs=list(in_specs) + job["in_specs"]
                          + [pl.BlockSpec(memory_space=pl.ANY)] * n_into,
                          out_specs=tuple(out_specs) + tuple(job["out_specs"]), scratch_shapes=scratch,
                          input_output_aliases={b + k: n_out + k for k in range(n_into)},
                          compiler_params=_params(*("arbitrary",) * len(grid)), name=name,
                          )(*args, *job["args"], *job["into"])
    return tuple(outs[:n_out]), tuple(outs[n_out:])


def _ada_kernel(ct_ref, w_ref, b_ref, o_ref, sb_ref):
    kdim, tn = w_ref.shape
    nt = tn // 128

    @pl.when(pl.program_id(0) == 0)
    def _():
        c = ct_ref[...]
        s = c * jax.nn.sigmoid(c)
        sb_ref[0] = jnp.broadcast_to(s[:, 0:1], (kdim, 128))
        sb_ref[1] = jnp.broadcast_to(s[:, 1:2], (kdim, 128))

    def body(kc, acc):
        r0 = pl.multiple_of(kc * 8, 8)
        s0 = sb_ref[0, pl.ds(r0, 8), :]
        s1 = sb_ref[1, pl.ds(r0, 8), :]
        out = []
        for t in range(nt):
            w = w_ref[pl.ds(r0, 8), t * 128:(t + 1) * 128]
            out.append(acc[2 * t] + w * s0)
            out.append(acc[2 * t + 1] + w * s1)
        return tuple(out)

    zero = jnp.zeros((8, 128), F32)
    acc = lax.fori_loop(0, kdim // 8, body, (zero,) * (2 * nt), unroll=8)
    for t in range(nt):
        cols = slice(t * 128, (t + 1) * 128)
        o_ref[0:1, cols] = jnp.sum(acc[2 * t], axis=0, keepdims=True) + b_ref[:, cols]
        o_ref[1:2, cols] = jnp.sum(acc[2 * t + 1], axis=0, keepdims=True) + b_ref[:, cols]


def _ada_mod(c2, w, b):
    kdim, n = w.shape
    tn = _tile(n, 512)
    return pl.pallas_call(
        _ada_kernel,
        out_shape=jax.ShapeDtypeStruct((2, n), F32),
        grid=(n // tn,),
        in_specs=[pl.BlockSpec((kdim, 2), lambda j: (0, 0)),
                  pl.BlockSpec((kdim, tn), lambda j: (0, j)),
                  pl.BlockSpec((1, tn), lambda j: (0, j))],
        out_specs=pl.BlockSpec((2, tn), lambda j: (0, j)),
        scratch_shapes=[pltpu.VMEM((2, kdim, 128), F32)],
        compiler_params=_params("arbitrary"),
        name="ada_mod",
    )(c2.T, w, b.reshape(1, n))


def _normalize(x):
    mu = jnp.mean(x, axis=-1, keepdims=True)
    xc = x - mu
    var = jnp.mean(xc * xc, axis=-1, keepdims=True)
    return xc * lax.rsqrt(var + LN_EPS)


LN_CHUNK = 16


def _row_chunks(n):
    step = min(n, LN_CHUNK)
    return [slice(r, r + step) for r in range(0, n, step)]


def _ln_mod_kernel(x_ref, mod_ref, o_ref):
    for rows in _row_chunks(x_ref.shape[0]):
        y = _normalize(x_ref[rows, :])
        o_ref[rows, :] = (y * (1.0 + mod_ref[1:2, :]) + mod_ref[0:1, :]).astype(o_ref.dtype)


def _ln_mod(x, mod3):
    s, d = x.shape
    tm = _tile(s, 256)
    return pl.pallas_call(
        _ln_mod_kernel,
        out_shape=jax.ShapeDtypeStruct((s, d), BF16),
        grid=(s // tm,),
        in_specs=[pl.BlockSpec((tm, d), lambda i: (i, 0)),
                  pl.BlockSpec((3, d), lambda i: (0, 0))],
        out_specs=pl.BlockSpec((tm, d), lambda i: (i, 0)),
        compiler_params=_params("parallel"),
        name="ln_mod",
    )(x, mod3)


def _proj_kernel(x_ref, w_ref, o_ref, wb_ref, *, sigmoid):
    @pl.when(pl.program_id(1) == 0)
    def _():
        wb_ref[...] = w_ref[...].astype(BF16)

    acc = jnp.dot(x_ref[...], wb_ref[...], preferred_element_type=F32)
    if sigmoid:
        acc = _sigmoid(acc)
    o_ref[...] = acc.astype(o_ref.dtype)


def _proj(h, w, col0, ncols, sigmoid, out_dtype, side=None, tm_want=1024, tn_want=512):
    s, kdim = h.shape
    tm, tn = _tile(s, tm_want), _tile(math.gcd(col0, ncols), tn_want)
    j0 = col0 // tn
    (out,), side_out = _host_call(
        functools.partial(_proj_kernel, sigmoid=sigmoid),
        grid=(ncols // tn, s // tm),
        in_specs=[pl.BlockSpec((tm, kdim), lambda j, i: (i, 0)),
                  pl.BlockSpec((kdim, tn), lambda j, i: (0, j0 + j))],
        out_specs=[pl.BlockSpec((tm, tn), lambda j, i: (i, j))],
        out_shape=[jax.ShapeDtypeStruct((s, ncols), out_dtype)],
        scratch=[pltpu.VMEM((kdim, tn), BF16)], args=(h, w), sem=("parallel", "arbitrary"),
        name="in_proj", side=side)
    return out, side_out


def _qkv_kernel(x_ref, w_ref, cos_ref, sin_ref, o_ref, *scratch, dil, q_scale):
    j = pl.program_id(1)
    acc = jnp.dot(x_ref[...], w_ref[...], preferred_element_type=F32)
    tm = acc.shape[0]

    scale = jnp.where(j == 0, q_scale, 1.0)
    cos = jnp.where(j < 2, cos_ref[...] * scale, 1.0)
    sin = jnp.where(j < 2, sin_ref[...] * scale, 0.0)
    for h in range(HEADS_PER_GROUP):
        t = acc[:, _head_cols(h)]
        t = t * cos + pltpu.roll(t, HEAD_DIM // 2, 1) * sin
        if dil == 1:
            o_ref[0, :, _head_cols(h)] = t.astype(o_ref.dtype)
        else:
            scratch[0][h] = t

    if dil > 1:
        for r in range(dil):
            for h in range(HEADS_PER_GROUP):
                rows = scratch[0][h, pl.ds(r, tm // dil, stride=dil), :]
                o_ref[r, :, _head_cols(h)] = rows.astype(o_ref.dtype)


def _qkv_proj(h, w_qkv, gi, cos, sin):
    s, kdim = h.shape
    dil = DILATIONS[gi]
    tm = _tile(s, 1024)
    ncol = ATTN_WIDTH // GROUP_WIDTH
    scratch = [] if dil == 1 else [pltpu.VMEM((HEADS_PER_GROUP, tm, HEAD_DIM), F32)]
    (out,), _ = _host_call(
        functools.partial(_qkv_kernel, dil=dil, q_scale=HEAD_DIM ** -0.5),
        grid=(s // tm, 3),
        in_specs=[pl.BlockSpec((tm, kdim), lambda i, j: (i, 0)),
                  pl.BlockSpec((kdim, GROUP_WIDTH), lambda i, j: (0, ncol * j + gi)),
                  pl.BlockSpec((tm, HEAD_DIM), lambda i, j: (i, 0)),
                  pl.BlockSpec((tm, HEAD_DIM), lambda i, j: (i, 0))],
        out_specs=[pl.BlockSpec((None, dil, tm // dil, GROUP_WIDTH), lambda i, j: (j, 0, i, 0))],
        out_shape=[jax.ShapeDtypeStruct((3, dil, s // dil, GROUP_WIDTH), BF16)],
        scratch=scratch, args=(h, w_qkv, cos, sin), sem=("parallel", "arbitrary"),
        name=f"qkv_proj_{dil}", side=None)
    return out


def _rope_tables(s):
    half = HEAD_DIM // 2
    inv = ROPE_THETA ** (-jnp.arange(half, dtype=F32) / half)
    ang = jnp.arange(s).astype(F32)[:, None] * inv[None, :]
    cos, sin = jnp.cos(ang), jnp.sin(ang)
    return jnp.concatenate([cos, cos], axis=-1), jnp.concatenate([-sin, sin], axis=-1)


def _attn_kernel(q_ref, kp_ref, kc_ref, kn_ref, vp_ref, vc_ref, vn_ref, o_ref, l_ref, kw_ref, vw_ref,
                 *, tq, seq):
    r = ATTN_RADIUS
    kw_ref[0:r, :] = kp_ref[...]
    kw_ref[r:r + tq, :] = kc_ref[...]
    kw_ref[r + tq:, :] = kn_ref[...]
    vw_ref[0:r, :] = vp_ref[...]
    vw_ref[r:r + tq, :] = vc_ref[...]
    vw_ref[r + tq:, :] = vn_ref[...]

    base = pl.program_id(1) * tq
    nkeys = Q_SUB + 2 * r
    qi = lax.broadcasted_iota(jnp.int32, (Q_SUB, nkeys), 0)
    kj = lax.broadcasted_iota(jnp.int32, (Q_SUB, nkeys), 1)
    lane_head = lax.broadcasted_iota(jnp.int32, (Q_SUB, HEAD_DIM), 1) // LSE_LANES

    for sb in range(tq // Q_SUB):
        rows = slice(sb * Q_SUB, (sb + 1) * Q_SUB)
        first = base + sb * Q_SUB - r
        lo = jnp.maximum(qi, -first)
        hi = jnp.minimum(qi + 2 * r, seq - 1 - first)
        keep = (kj >= lo) & (kj <= hi)
        lse_tile = jnp.zeros((Q_SUB, HEAD_DIM), F32)
        for h in range(HEADS_PER_GROUP):
            cols = _head_cols(h)
            qs = q_ref[rows, cols]
            ks = kw_ref[sb * Q_SUB:sb * Q_SUB + nkeys, cols]
            vs = vw_ref[sb * Q_SUB:sb * Q_SUB + nkeys, cols]
            sc = lax.dot_general(qs, ks, (((1,), (1,)), ((), ())), preferred_element_type=F32)
            sc = jnp.where(keep, sc, NEG_INF)
            m = jnp.max(sc, axis=-1, keepdims=True)
            p = jnp.exp(sc - m)
            den = jnp.sum(p, axis=-1, keepdims=True)
            o = jnp.dot(p.astype(BF16), vs, preferred_element_type=F32) / den
            o_ref[rows, cols] = o.astype(o_ref.dtype)
            lse_tile = jnp.where(lane_head == h, m + jnp.log(den), lse_tile)
        l_ref[rows, :] = lse_tile


def _attn_group(qkv):
    _, dil, seq, _ = qkv.shape
    tq = _tile(seq, 512)
    r = ATTN_RADIUS
    halo_per_tile = tq // r
    n_halo = seq // r

    def main(which):
        return pl.BlockSpec((None, None, tq, GROUP_WIDTH), lambda rr, lb: (which, rr, lb, 0))

    def before(which):
        return pl.BlockSpec((None, None, r, GROUP_WIDTH),
                            lambda rr, lb: (which, rr, jnp.maximum(lb * halo_per_tile - 1, 0), 0))

    def after(which):
        return pl.BlockSpec((None, None, r, GROUP_WIDTH),
                            lambda rr, lb: (which, rr, jnp.minimum((lb + 1) * halo_per_tile, n_halo - 1), 0))

    return pl.pallas_call(
        functools.partial(_attn_kernel, tq=tq, seq=seq),
        out_shape=(jax.ShapeDtypeStruct((dil, seq, GROUP_WIDTH), BF16),
                   jax.ShapeDtypeStruct((dil, seq, HEAD_DIM), F32)),
        grid=(dil, seq // tq),
        in_specs=[main(0), before(1), main(1), after(1), before(2), main(2), after(2)],
        out_specs=(pl.BlockSpec((None, tq, GROUP_WIDTH), lambda rr, lb: (rr, lb, 0)),
                   pl.BlockSpec((None, tq, HEAD_DIM), lambda rr, lb: (rr, lb, 0))),
        scratch_shapes=[pltpu.VMEM((tq + 2 * r, GROUP_WIDTH), BF16),
                        pltpu.VMEM((tq + 2 * r, GROUP_WIDTH), BF16)],
        compiler_params=_params("parallel", "arbitrary"),
        name=f"banded_attn_{dil}",
    )(qkv, qkv, qkv, qkv, qkv, qkv, qkv)


def _combine_kernel(*refs):
    ng = len(DILATIONS)
    o_refs, l_refs = refs[0:2 * ng:2], refs[1:2 * ng:2]
    out_ref, os_ref, ls_ref = refs[2 * ng:]
    t = out_ref.shape[0]
    for g, dil in enumerate(DILATIONS):
        if dil == 1:
            continue
        for r in range(dil):
            dst = pl.ds(r, t // dil, stride=dil)
            ls_ref[g, dst, :] = l_refs[g][r]
            for h in range(HEADS_PER_GROUP):
                os_ref[g, h, dst, :] = o_refs[g][r, :, _head_cols(h)].astype(F32)

    def lse_of(g):
        return l_refs[g][0] if DILATIONS[g] == 1 else ls_ref[g]

    top = functools.reduce(jnp.maximum, [lse_of(g) for g in range(ng)])
    e = [jnp.exp(lse_of(g) - top) for g in range(ng)]
    inv = 1.0 / functools.reduce(lambda a, b: a + b, e)
    w = [eg * inv for eg in e]
    for h in range(HEADS_PER_GROUP):
        acc = None
        for g, dil in enumerate(DILATIONS):
            og = o_refs[g][0, :, _head_cols(h)].astype(F32) if dil == 1 else os_ref[g, h]
            term = w[g][:, h * LSE_LANES:h * LSE_LANES + 1] * og
            acc = term if acc is None else acc + term
        out_ref[:, _head_cols(h)] = acc.astype(out_ref.dtype)


def _combine(outs):
    ng = len(DILATIONS)
    s = outs[0][0].shape[0] * outs[0][0].shape[1]
    t = _tile(s, 512)
    args, in_specs = [], []
    for (o, l), dil in zip(outs, DILATIONS):
        args += [o, l]
        in_specs += [pl.BlockSpec((dil, t // dil, GROUP_WIDTH), lambda i: (0, i, 0)),
                     pl.BlockSpec((dil, t // dil, HEAD_DIM), lambda i: (0, i, 0))]
    return pl.pallas_call(
        _combine_kernel,
        out_shape=jax.ShapeDtypeStruct((s, GROUP_WIDTH), BF16),
        grid=(s // t,),
        in_specs=in_specs,
        out_specs=pl.BlockSpec((t, GROUP_WIDTH), lambda i: (i, 0)),
        scratch_shapes=[pltpu.VMEM((ng, HEADS_PER_GROUP, t, HEAD_DIM), F32),
                        pltpu.VMEM((ng, t, HEAD_DIM), F32)],
        compiler_params=_params("parallel"),
        name="attn_combine",
    )(*args)


def _dft_tables(s, cg):
    n1, n2 = DFT_ROWS, s // DFT_ROWS

    def cs(rows, cols, period):
        ang = 2.0 * np.pi * ((np.arange(rows)[:, None] * np.arange(cols)[None, :]) % period) / period
        return np.cos(ang), np.sin(ang)

    c1, s1 = cs(n1, n1, n1)
    c2, s2 = cs(n2, n2, n2)
    ct, st = cs(n1, n2, s)
    cc, sc = cs(cg, cg, cg)
    norm = 1.0 / math.sqrt(s * cg)
    as_bf16 = lambda a: jnp.asarray(a, F32).astype(BF16)
    return dict(
        w1=as_bf16(np.concatenate([c1, -s1], axis=0)),
        tw_cos=jnp.asarray(ct, F32), tw_sin=jnp.asarray(st, F32),
        w2_re=as_bf16(np.concatenate([c2, -s2], axis=0)),
        w2_im=as_bf16(np.concatenate([s2, c2], axis=0)),
        wc_re=as_bf16(cc * norm), wc_im=as_bf16(sc * norm))


LANES = 128


def _flatten_slabs(dst_ref, src_refs):
    for k, src in enumerate(src_refs):
        n = src.shape[0]
        dst_ref[k] = src[...].reshape(n * DFT_SLAB, LANES)


def _slab(ref, k, row, n):
    return ref[k, pl.ds(row, n, stride=DFT_SLAB), :]


def _lane_tiles(x, nt):
    return [x[:, t * LANES:(t + 1) * LANES] for t in range(nt)]


def _dft_stage1_kernel(*refs, nt):
    x_refs, (w1_ref, twc_ref, tws_ref, br_ref, bi_ref, xs_ref) = refs[:nt], refs[nt:]
    n1 = w1_ref.shape[1]
    _flatten_slabs(xs_ref, x_refs)
    lane = lax.broadcasted_iota(jnp.int32, twc_ref.shape, 1)
    for b in range(DFT_SLAB):
        n2 = pl.program_id(0) * DFT_SLAB + b
        xb = jnp.concatenate([_slab(xs_ref, k, b, n1).astype(BF16) for k in range(nt)], axis=1)
        a = jnp.dot(w1_ref[...], xb, preferred_element_type=F32)
        ar, ai = a[:n1], a[n1:]
        c = jnp.sum(jnp.where(lane == n2, twc_ref[...], 0.0), axis=1, keepdims=True)
        sn = jnp.sum(jnp.where(lane == n2, tws_ref[...], 0.0), axis=1, keepdims=True)
        br_ref[b] = ar * c + ai * sn
        bi_ref[b] = ai * c - ar * sn


def _dft_stage2_kernel(*refs, nt):
    br_refs, bi_refs = refs[:nt], refs[nt:2 * nt]
    w2r_ref, w2i_ref, wcr_ref, wci_ref, o_ref, bs_ref, scr_ref = refs[2 * nt:]
    n2 = w2r_ref.shape[1]
    cg = wcr_ref.shape[0]
    _flatten_slabs(bs_ref, br_refs + bi_refs)
    for kk in range(DFT_SLAB):
        br = jnp.concatenate([_slab(bs_ref, k, kk, n2).astype(BF16) for k in range(nt)], axis=1)
        bi = jnp.concatenate([_slab(bs_ref, nt + k, kk, n2).astype(BF16) for k in range(nt)], axis=1)
        z = (jnp.dot(w2r_ref[...], br, preferred_element_type=F32)
             + jnp.dot(w2i_ref[...], bi, preferred_element_type=F32))
        zr, zi = z[:n2].astype(BF16), z[n2:].astype(BF16)
        groups = []
        for g in range(nt * LANES // cg):
            cols = slice(g * cg, (g + 1) * cg)
            groups.append(jnp.dot(zr[:, cols], wcr_ref[...], preferred_element_type=F32)
                          + jnp.dot(zi[:, cols], wci_ref[...], preferred_element_type=F32))
        out = groups[0] if len(groups) == 1 else jnp.concatenate(groups, axis=1)
        for t, tile in enumerate(_lane_tiles(out, nt)):
            scr_ref[t, pl.ds(kk, n2, stride=DFT_SLAB), :] = tile
    for t in range(nt):
        o_ref[:, :, t * LANES:(t + 1) * LANES] = scr_ref[t].reshape(n2, DFT_SLAB, LANES)


def _fourier_mix(f):
    s, width = f.shape
    n1, n2 = DFT_ROWS, s // DFT_ROWS
    cg = width // FOURIER_GROUPS
    t = _dft_tables(s, cg)
    slab = DFT_SLAB
    full = lambda a: pl.BlockSpec(a.shape, lambda i, j: (0,) * a.ndim)

    def tile_specs(rows, nt, place):
        return [pl.BlockSpec((rows, slab, LANES), functools.partial(place, t=k)) for k in range(nt)]

    nt1 = min(4, width // LANES)
    br, bi = pl.pallas_call(
        functools.partial(_dft_stage1_kernel, nt=nt1),
        out_shape=(jax.ShapeDtypeStruct((n2, n1, width), F32),) * 2,
        grid=(n2 // slab, width // (nt1 * LANES)),
        in_specs=tile_specs(n1, nt1, lambda i, j, t: (0, i, j * nt1 + t))
        + [full(t["w1"]), full(t["tw_cos"]), full(t["tw_sin"])],
        out_specs=(pl.BlockSpec((slab, n1, nt1 * LANES), lambda i, j: (i, 0, j)),) * 2,
        scratch_shapes=[pltpu.VMEM((nt1, n1 * slab, LANES), F32)],
        compiler_params=_params("parallel", "parallel"),
        name="dft_stage1",
    )(*([f.reshape(n1, n2, width)] * nt1), t["w1"], t["tw_cos"], t["tw_sin"])

    nt2 = width // LANES
    stage2_in = tile_specs(n2, nt2, lambda i, j, t: (0, i, j * nt2 + t))
    out = pl.pallas_call(
        functools.partial(_dft_stage2_kernel, nt=nt2),
        out_shape=jax.ShapeDtypeStruct((n2, n1, width), F32),
        grid=(n1 // slab, width // (nt2 * LANES)),
        in_specs=stage2_in + stage2_in
        + [full(t["w2_re"]), full(t["w2_im"]), full(t["wc_re"]), full(t["wc_im"])],
        out_specs=pl.BlockSpec((n2, slab, nt2 * LANES), lambda i, j: (0, i, j)),
        scratch_shapes=[pltpu.VMEM((2 * nt2, n2 * slab, LANES), F32),
                        pltpu.VMEM((nt2, n2 * slab, LANES), F32)],
        compiler_params=_params("parallel", "parallel"),
        name="dft_stage2",
    )(*([br] * nt2), *([bi] * nt2), t["w2_re"], t["w2_im"], t["wc_re"], t["wc_im"])
    return out.reshape(s, width)


def _merge_kernel(a_ref, f_ref, wa_ref, wf_ref, ga_ref, gf_ref, o_ref):
    ab = jnp.dot(a_ref[...], wa_ref[...], preferred_element_type=F32)
    fb = jnp.dot(f_ref[...].astype(BF16), wf_ref[...], preferred_element_type=F32)
    o_ref[...] = (ga_ref[...].astype(F32) * ab + gf_ref[...].astype(F32) * fb).astype(o_ref.dtype)


def _merge(attn, four, wa, wf, ga, gf):
    s, ka = attn.shape
    kf = four.shape[1]
    n = wa.shape[1]
    tm, tn = _tile(s, 1024), _tile(n, 1024)
    return pl.pallas_call(
        _merge_kernel,
        out_shape=jax.ShapeDtypeStruct((s, n), BF16),
        grid=(s // tm, n // tn),
        in_specs=[pl.BlockSpec((tm, ka), lambda i, j: (i, 0)),
                  pl.BlockSpec((tm, kf), lambda i, j: (i, 0)),
                  pl.BlockSpec((ka, tn), lambda i, j: (0, j)),
                  pl.BlockSpec((kf, tn), lambda i, j: (0, j)),
                  pl.BlockSpec((tm, tn), lambda i, j: (i, j)),
                  pl.BlockSpec((tm, tn), lambda i, j: (i, j))],
        out_specs=pl.BlockSpec((tm, tn), lambda i, j: (i, j)),
        compiler_params=_params("parallel", "arbitrary"),
        name="branch_merge",
    )(attn, four, wa, wf, ga, gf)


def _proj_res_kernel(a_ref, w_ref, res_ref, gate_ref, o_ref, *, alpha):
    acc = jnp.dot(a_ref[...], w_ref[...], preferred_element_type=F32)
    o_ref[...] = alpha * res_ref[...] + gate_ref[...] * acc


def _proj_res(a, w, res, gate, alpha, tm_want, tn_want):
    s, kdim = a.shape
    n = w.shape[1]
    tm, tn = _tile(s, tm_want), _tile(n, tn_want)
    return pl.pallas_call(
        functools.partial(_proj_res_kernel, alpha=alpha),
        out_shape=jax.ShapeDtypeStruct((s, n), F32),
        grid=(s // tm, n // tn),
        in_specs=[pl.BlockSpec((tm, kdim), lambda i, j: (i, 0)),
                  pl.BlockSpec((kdim, tn), lambda i, j: (0, j)),
                  pl.BlockSpec((tm, tn), lambda i, j: (i, j)),
                  pl.BlockSpec((1, tn), lambda i, j: (0, j))],
        out_specs=pl.BlockSpec((tm, tn), lambda i, j: (i, j)),
        compiler_params=_params("parallel", "arbitrary"),
        name="proj_res",
    )(a, w, res, gate)


def _ln_out_kernel(t_ref, g_ref, b_ref, *refs):
    for rows in _row_chunks(t_ref.shape[0]):
        y = _normalize(t_ref[rows, :]) * g_ref[...] + b_ref[...]
        if len(refs) == 1:
            refs[0][rows, :] = y
        else:
            mod_ref, y_ref, h_ref = refs
            y_ref[rows, :] = y
            h_ref[rows, :] = (_normalize(y) * (1.0 + mod_ref[1:2, :]) + mod_ref[0:1, :]).astype(h_ref.dtype)


def _ln_out(t, g, b, mod3=None):
    s, d = t.shape
    tm = _tile(s, 256)
    row = pl.BlockSpec((tm, d), lambda i: (i, 0))
    vec = lambda rows: pl.BlockSpec((rows, d), lambda i: (0, 0))
    args, in_specs = [t, g, b], [row, vec(1), vec(1)]
    out_shape, out_specs = [jax.ShapeDtypeStruct((s, d), F32)], [row]
    if mod3 is not None:
        args.append(mod3)
        in_specs.append(vec(3))
        out_shape.append(jax.ShapeDtypeStruct((s, d), BF16))
        out_specs.append(row)
    return pl.pallas_call(
        _ln_out_kernel,
        out_shape=tuple(out_shape),
        grid=(s // tm,),
        in_specs=in_specs,
        out_specs=tuple(out_specs),
        compiler_params=_params("parallel"),
        name="ln_out",
    )(*args)


def _swiglu_kernel(x_ref, wg_ref, wu_ref, o_ref):
    x = x_ref[...]
    gpre = jnp.dot(x, wg_ref[...].astype(BF16), preferred_element_type=F32)
    up = jnp.dot(x, wu_ref[...].astype(BF16), preferred_element_type=F32)
    o_ref[...] = (gpre * _sigmoid(gpre) * up).astype(o_ref.dtype)


def _swiglu(h, wg, wu, side=None):
    s, kdim = h.shape
    n = wg.shape[1]
    tm, tn = _tile(s, 1024), _tile(n, 256)
    (out,), side_out = _host_call(
        _swiglu_kernel,
        grid=(s // tm, n // tn),
        in_specs=[pl.BlockSpec((tm, kdim), lambda i, j: (i, 0)),
                  pl.BlockSpec((kdim, tn), lambda i, j: (0, j)),
                  pl.BlockSpec((kdim, tn), lambda i, j: (0, j))],
        out_specs=[pl.BlockSpec((tm, tn), lambda i, j: (i, j))],
        out_shape=[jax.ShapeDtypeStruct((s, n), BF16)],
        scratch=[], args=(h, wg, wu), sem=("parallel", "arbitrary"), name="swiglu", side=side)
    return out, side_out


def _mixer(x, h, mod6, wts, alpha, side_a=None, side_b=None):
    s, d = x.shape
    cos, sin = _rope_tables(s)
    fw = d // FOURIER_WIDTH_DIVISOR
    f, _ = _proj(h, wts["w_in"], 3 * ATTN_WIDTH, fw, False, F32)
    ga, side_out = _proj(h, wts["w_in"], 3 * ATTN_WIDTH + fw, d, True, BF16, side_a)
    gf, side_out_b = _proj(h, wts["w_in"], 3 * ATTN_WIDTH + fw + d, d, True, BF16,
                           side_b(side_out) if side_b else None)
    side_out = side_out_b or side_out
    outs = [_attn_group(_qkv_proj(h, wts["w_qkv"], gi, cos, sin)) for gi in range(len(DILATIONS))]
    attn = _combine(outs)
    four = _fourier_mix(f)

    merged = _merge(attn, four, wts["w_attn_up"], wts["w_fourier_up"], ga, gf)
    return _proj_res(merged, wts["w_mix_out"], x, mod6[2:3], alpha, 1024, 512), side_out


def kernel(x_prompt, x_sample, c_prompt, c_sample, w_ada, b_ada, w_in, w_attn_up, w_fourier_up,
           w_mix_out, ln1_g, ln1_b, w_gate, w_up, w_down, ln2_g, ln2_b):
    depth = w_ada.shape[0]
    d = x_prompt.shape[-1]
    alpha = (2.0 * depth) ** 0.25
    xp, xs = x_prompt[0], x_sample[0]
    c2 = jnp.concatenate([c_prompt, c_sample], axis=0)
    row = lambda v: v.reshape(1, d)
    for l in range(depth):
        mod = _ada_mod(c2, w_ada[l], b_ada[l]).reshape(2, N_MOD, d)
        mp, ms = mod[0], mod[1]
        wts = dict(w_qkv=w_in[l][:, :3 * ATTN_WIDTH].astype(BF16), w_in=w_in[l],
                   w_attn_up=w_attn_up[l].astype(BF16), w_fourier_up=w_fourier_up[l].astype(BF16),
                   w_mix_out=w_mix_out[l].astype(BF16))
        wg, wu, wd = w_gate[l], w_up[l], w_down[l].astype(BF16)
        g1, b1, g2, b2 = row(ln1_g[l]), row(ln1_b[l]), row(ln2_g[l]), row(ln2_b[l])

        hp = _ln_mod(xp, mp[0:3])
        t1p, (hs,) = _mixer(xp, hp, mp, wts, alpha, _side_job(_ln_mod_kernel, xs, [BF16], [ms[0:3]]))
        ln1p = functools.partial(_side_job, _ln_out_kernel, t1p, [F32, BF16], [g1, b1, mp[3:6]])
        t1s, (x1p, h2p) = _mixer(xs, hs, ms, wts, alpha, ln1p(part=(0, 2)),
                                 lambda first: ln1p(part=(1, 2), into=first))
        up, (x1s, h2s) = _swiglu(h2p, wg, wu, _side_job(_ln_out_kernel, t1s, [F32, BF16], [g1, b1, ms[3:6]]))
        t2p = _proj_res(up, wd, x1p, mp[5:6], alpha, 512, 512)
        us, (xp,) = _swiglu(h2s, wg, wu, _side_job(_ln_out_kernel, t2p, [F32], [g2, b2]))
        t2s = _proj_res(us, wd, x1s, ms[5:6], alpha, 512, 512)
        (xs,) = _ln_out(t2s, g2, b2)
    return (xp[None], xs[None])
```

```python
import functools
import math

import numpy as np
import jax
import jax.numpy as jnp
from jax import lax
from jax.experimental import pallas as pl
from jax.experimental.pallas import tpu as pltpu

HEAD_DIM = 128
HEADS_PER_GROUP = 8
GROUP_WIDTH = HEADS_PER_GROUP * HEAD_DIM
DILATIONS = (1, 4, 16)
ATTN_RADIUS = 64
ATTN_WIDTH = len(DILATIONS) * GROUP_WIDTH
FOURIER_WIDTH_DIVISOR = 4
FOURIER_GROUPS = 4
DFT_ROWS = 128
DFT_SLAB = 8
N_MOD = 6
ROPE_THETA = 10000.0
LN_EPS = 1e-5
NEG_INF = -1e30
Q_SUB = 128
LSE_LANES = HEAD_DIM // HEADS_PER_GROUP

VMEM_LIMIT_BYTES = 56 * 1024 * 1024

F32 = jnp.float32
BF16 = jnp.bfloat16


def _params(*sem):
    return pltpu.CompilerParams(dimension_semantics=sem, vmem_limit_bytes=VMEM_LIMIT_BYTES)


def _tile(n, want):
    t = min(n, want)
    while n % t:
        t //= 2
    return t


def _head_cols(h):
    return slice(h * HEAD_DIM, (h + 1) * HEAD_DIM)


def _sigmoid(x):
    return 0.5 * jnp.tanh(0.5 * x) + 0.5


SIDE_MIN_ROWS = 16


def _side_job(body, rows_arg, out_dtypes, const_args):
    def build(n_steps, step_of):
        s, d = rows_arg.shape
        sr = SIDE_MIN_ROWS
        while s // sr > n_steps:
            sr *= 2
        last = s // sr - 1
        row = pl.BlockSpec((sr, d), lambda *idx: (jnp.minimum(step_of(*idx), last), 0))
        consts = [pl.BlockSpec(c.shape, lambda *idx: (0, 0)) for c in const_args]
        return dict(body=body, args=[rows_arg, *const_args], in_specs=[row] + consts,
                    out_shape=[jax.ShapeDtypeStruct((s, d), dt) for dt in out_dtypes],
                    out_specs=[row] * len(out_dtypes))
    return build


def _host_call(body, grid, in_specs, out_specs, out_shape, scratch, args, sem, name, sides=()):
    n_in, n_out = len(in_specs), len(out_shape)
    if not sides:
        outs = pl.pallas_call(body, out_shape=tuple(out_shape), grid=grid, in_specs=in_specs,
                              out_specs=tuple(out_specs), scratch_shapes=scratch,
                              compiler_params=_params(*sem), name=name)(*args)
        return tuple(outs), []

    def step_of(*idx):
        step = idx[0]
        for axis in range(1, len(grid)):
            step = step * grid[axis] + idx[axis]
        return step

    jobs = [side(math.prod(grid), step_of) for side in sides]
    job_in = [spec for job in jobs for spec in job["in_specs"]]
    job_out = [spec for job in jobs for spec in job["out_specs"]]
    job_shape = [shape for job in jobs for shape in job["out_shape"]]
    job_args = [arg for job in jobs for arg in job["args"]]
    in_end, out_end = n_in + len(job_in), n_in + len(job_in) + n_out

    def fused(*refs):
        body(*refs[:n_in], *refs[in_end:out_end], *refs[out_end + len(job_out):])
        i, o = n_in, out_end
        for job in jobs:
            ni, no = len(job["in_specs"]), len(job["out_specs"])
            job["body"](*refs[i:i + ni], *refs[o:o + no])
            i, o = i + ni, o + no

    outs = pl.pallas_call(fused, out_shape=tuple(out_shape) + tuple(job_shape), grid=grid,
                          in_specs=list(in_specs) + job_in, out_specs=tuple(out_specs) + tuple(job_out),
                          scratch_shapes=scratch, compiler_params=_params(*("arbitrary",) * len(grid)),
                          name=name)(*args, *job_args)
    side_outs, o = [], n_out
    for job in jobs:
        side_outs.append(tuple(outs[o:o + len(job["out_specs"])]))
        o += len(job["out_specs"])
    return tuple(outs[:n_out]), side_outs


def _cast_kernel(x_ref, o_ref):
    o_ref[...] = x_ref[...].astype(o_ref.dtype)


def _side_cast(w):
    return _side_job(_cast_kernel, w, [BF16], [])


def _ada_kernel(ct_ref, w_ref, b_ref, o_ref, sb_ref):
    kdim, tn = w_ref.shape
    nt = tn // 128

    @pl.when(pl.program_id(0) == 0)
    def _():
        c = ct_ref[...]
        s = c * jax.nn.sigmoid(c)
        sb_ref[0] = jnp.broadcast_to(s[:, 0:1], (kdim, 128))
        sb_ref[1] = jnp.broadcast_to(s[:, 1:2], (kdim, 128))

    def body(kc, acc):
        r0 = pl.multiple_of(kc * 8, 8)
        s0 = sb_ref[0, pl.ds(r0, 8), :]
        s1 = sb_ref[1, pl.ds(r0, 8), :]
        out = []
        for t in range(nt):
            w = w_ref[pl.ds(r0, 8), t * 128:(t + 1) * 128]
            out.append(acc[2 * t] + w * s0)
            out.append(acc[2 * t + 1] + w * s1)
        return tuple(out)

    zero = jnp.zeros((8, 128), F32)
    acc = lax.fori_loop(0, kdim // 8, body, (zero,) * (2 * nt), unroll=8)
    for t in range(nt):
        cols = slice(t * 128, (t + 1) * 128)
        o_ref[0:1, cols] = jnp.sum(acc[2 * t], axis=0, keepdims=True) + b_ref[:, cols]
        o_ref[1:2, cols] = jnp.sum(acc[2 * t + 1], axis=0, keepdims=True) + b_ref[:, cols]


def _ada_mod(c2, w, b):
    kdim, n = w.shape
    tn = _tile(n, 512)
    return pl.pallas_call(
        _ada_kernel,
        out_shape=jax.ShapeDtypeStruct((2, n), F32),
        grid=(n // tn,),
        in_specs=[pl.BlockSpec((kdim, 2), lambda j: (0, 0)),
                  pl.BlockSpec((kdim, tn), lambda j: (0, j)),
                  pl.BlockSpec((1, tn), lambda j: (0, j))],
        out_specs=pl.BlockSpec((2, tn), lambda j: (0, j)),
        scratch_shapes=[pltpu.VMEM((2, kdim, 128), F32)],
        compiler_params=_params("arbitrary"),
        name="ada_mod",
    )(c2.T, w, b.reshape(1, n))


def _normalize(x):
    mu = jnp.mean(x, axis=-1, keepdims=True)
    xc = x - mu
    var = jnp.mean(xc * xc, axis=-1, keepdims=True)
    return xc * lax.rsqrt(var + LN_EPS)


LN_CHUNK = 16


def _row_chunks(n):
    step = min(n, LN_CHUNK)
    return [slice(r, r + step) for r in range(0, n, step)]


def _ln_mod_kernel(x_ref, mod_ref, o_ref):
    for rows in _row_chunks(x_ref.shape[0]):
        y = _normalize(x_ref[rows, :])
        o_ref[rows, :] = (y * (1.0 + mod_ref[1:2, :]) + mod_ref[0:1, :]).astype(o_ref.dtype)


def _ln_mod(x, mod3):
    s, d = x.shape
    tm = _tile(s, 256)
    return pl.pallas_call(
        _ln_mod_kernel,
        out_shape=jax.ShapeDtypeStruct((s, d), BF16),
        grid=(s // tm,),
        in_specs=[pl.BlockSpec((tm, d), lambda i: (i, 0)),
                  pl.BlockSpec((3, d), lambda i: (0, 0))],
        out_specs=pl.BlockSpec((tm, d), lambda i: (i, 0)),
        compiler_params=_params("parallel"),
        name="ln_mod",
    )(x, mod3)


def _proj_kernel(x_ref, w_ref, o_ref, wb_ref, *, sigmoid):
    @pl.when(pl.program_id(1) == 0)
    def _():
        wb_ref[...] = w_ref[...].astype(BF16)

    acc = jnp.dot(x_ref[...], wb_ref[...], preferred_element_type=F32)
    if sigmoid:
        acc = _sigmoid(acc)
    o_ref[...] = acc.astype(o_ref.dtype)


def _proj(h, w, col0, ncols, sigmoid, out_dtype, sides=(), tm_want=1024, tn_want=512):
    s, kdim = h.shape
    tm, tn = _tile(s, tm_want), _tile(math.gcd(col0, ncols), tn_want)
    j0 = col0 // tn
    (out,), side_out = _host_call(
        functools.partial(_proj_kernel, sigmoid=sigmoid),
        grid=(ncols // tn, s // tm),
        in_specs=[pl.BlockSpec((tm, kdim), lambda j, i: (i, 0)),
                  pl.BlockSpec((kdim, tn), lambda j, i: (0, j0 + j))],
        out_specs=[pl.BlockSpec((tm, tn), lambda j, i: (i, j))],
        out_shape=[jax.ShapeDtypeStruct((s, ncols), out_dtype)],
        scratch=[pltpu.VMEM((kdim, tn), BF16)], args=(h, w), sem=("parallel", "arbitrary"),
        name="in_proj", sides=sides)
    return out, side_out


def _qkv_kernel(x_ref, w_ref, cos_ref, sin_ref, o_ref, *scratch, dil, q_scale):
    j = pl.program_id(1)
    acc = jnp.dot(x_ref[...], w_ref[...], preferred_element_type=F32)
    tm = acc.shape[0]

    scale = jnp.where(j == 0, q_scale, 1.0)
    cos = jnp.where(j < 2, cos_ref[...] * scale, 1.0)
    sin = jnp.where(j < 2, sin_ref[...] * scale, 0.0)
    for h in range(HEADS_PER_GROUP):
        t = acc[:, _head_cols(h)]
        t = t * cos + pltpu.roll(t, HEAD_DIM // 2, 1) * sin
        if dil == 1:
            o_ref[0, :, _head_cols(h)] = t.astype(o_ref.dtype)
        else:
            scratch[0][h] = t

    if dil > 1:
        for r in range(dil):
            for h in range(HEADS_PER_GROUP):
                rows = scratch[0][h, pl.ds(r, tm // dil, stride=dil), :]
                o_ref[r, :, _head_cols(h)] = rows.astype(o_ref.dtype)


def _qkv_proj(h, w_qkv, gi, cos, sin):
    s, kdim = h.shape
    dil = DILATIONS[gi]
    tm = _tile(s, 1024)
    ncol = ATTN_WIDTH // GROUP_WIDTH
    scratch = [] if dil == 1 else [pltpu.VMEM((HEADS_PER_GROUP, tm, HEAD_DIM), F32)]
    (out,), _ = _host_call(
        functools.partial(_qkv_kernel, dil=dil, q_scale=HEAD_DIM ** -0.5),
        grid=(s // tm, 3),
        in_specs=[pl.BlockSpec((tm, kdim), lambda i, j: (i, 0)),
                  pl.BlockSpec((kdim, GROUP_WIDTH), lambda i, j: (0, ncol * j + gi)),
                  pl.BlockSpec((tm, HEAD_DIM), lambda i, j: (i, 0)),
                  pl.BlockSpec((tm, HEAD_DIM), lambda i, j: (i, 0))],
        out_specs=[pl.BlockSpec((None, dil, tm // dil, GROUP_WIDTH), lambda i, j: (j, 0, i, 0))],
        out_shape=[jax.ShapeDtypeStruct((3, dil, s // dil, GROUP_WIDTH), BF16)],
        scratch=scratch, args=(h, w_qkv, cos, sin), sem=("parallel", "arbitrary"),
        name=f"qkv_proj_{dil}")
    return out


def _rope_tables(s):
    half = HEAD_DIM // 2
    inv = ROPE_THETA ** (-jnp.arange(half, dtype=F32) / half)
    ang = jnp.arange(s).astype(F32)[:, None] * inv[None, :]
    cos, sin = jnp.cos(ang), jnp.sin(ang)
    return jnp.concatenate([cos, cos], axis=-1), jnp.concatenate([-sin, sin], axis=-1)


def _attn_kernel(q_ref, kp_ref, kc_ref, kn_ref, vp_ref, vc_ref, vn_ref, o_ref, l_ref, kw_ref, vw_ref,
                 *, tq, seq):
    r = ATTN_RADIUS
    kw_ref[0:r, :] = kp_ref[...]
    kw_ref[r:r + tq, :] = kc_ref[...]
    kw_ref[r + tq:, :] = kn_ref[...]
    vw_ref[0:r, :] = vp_ref[...]
    vw_ref[r:r + tq, :] = vc_ref[...]
    vw_ref[r + tq:, :] = vn_ref[...]

    base = pl.program_id(1) * tq
    nkeys = Q_SUB + 2 * r
    qi = lax.broadcasted_iota(jnp.int32, (Q_SUB, nkeys), 0)
    kj = lax.broadcasted_iota(jnp.int32, (Q_SUB, nkeys), 1)
    lane_head = lax.broadcasted_iota(jnp.int32, (Q_SUB, HEAD_DIM), 1) // LSE_LANES

    for sb in range(tq // Q_SUB):
        rows = slice(sb * Q_SUB, (sb + 1) * Q_SUB)
        first = base + sb * Q_SUB - r
        lo = jnp.maximum(qi, -first)
        hi = jnp.minimum(qi + 2 * r, seq - 1 - first)
        keep = (kj >= lo) & (kj <= hi)
        lse_tile = jnp.zeros((Q_SUB, HEAD_DIM), F32)
        for h in range(HEADS_PER_GROUP):
            cols = _head_cols(h)
            qs = q_ref[rows, cols]
            ks = kw_ref[sb * Q_SUB:sb * Q_SUB + nkeys, cols]
            vs = vw_ref[sb * Q_SUB:sb * Q_SUB + nkeys, cols]
            sc = lax.dot_general(qs, ks, (((1,), (1,)), ((), ())), preferred_element_type=F32)
            sc = jnp.where(keep, sc, NEG_INF)
            m = jnp.max(sc, axis=-1, keepdims=True)
            p = jnp.exp(sc - m)
            den = jnp.sum(p, axis=-1, keepdims=True)
            o = jnp.dot(p.astype(BF16), vs, preferred_element_type=F32) / den
            o_ref[rows, cols] = o.astype(o_ref.dtype)
            lse_tile = jnp.where(lane_head == h, m + jnp.log(den), lse_tile)
        l_ref[rows, :] = lse_tile


def _attn_group(qkv):
    _, dil, seq, _ = qkv.shape
    tq = _tile(seq, 512)
    r = ATTN_RADIUS
    halo_per_tile = tq // r
    n_halo = seq // r

    def main(which):
        return pl.BlockSpec((None, None, tq, GROUP_WIDTH), lambda rr, lb: (which, rr, lb, 0))

    def before(which):
        return pl.BlockSpec((None, None, r, GROUP_WIDTH),
                            lambda rr, lb: (which, rr, jnp.maximum(lb * halo_per_tile - 1, 0), 0))

    def after(which):
        return pl.BlockSpec((None, None, r, GROUP_WIDTH),
                            lambda rr, lb: (which, rr, jnp.minimum((lb + 1) * halo_per_tile, n_halo - 1), 0))

    return pl.pallas_call(
        functools.partial(_attn_kernel, tq=tq, seq=seq),
        out_shape=(jax.ShapeDtypeStruct((dil, seq, GROUP_WIDTH), BF16),
                   jax.ShapeDtypeStruct((dil, seq, HEAD_DIM), F32)),
        grid=(dil, seq // tq),
        in_specs=[main(0), before(1), main(1), after(1), before(2), main(2), after(2)],
        out_specs=(pl.BlockSpec((None, tq, GROUP_WIDTH), lambda rr, lb: (rr, lb, 0)),
                   pl.BlockSpec((None, tq, HEAD_DIM), lambda rr, lb: (rr, lb, 0))),
        scratch_shapes=[pltpu.VMEM((tq + 2 * r, GROUP_WIDTH), BF16),
                        pltpu.VMEM((tq + 2 * r, GROUP_WIDTH), BF16)],
        compiler_params=_params("parallel", "arbitrary"),
        name=f"banded_attn_{dil}",
    )(qkv, qkv, qkv, qkv, qkv, qkv, qkv)


def _combine_kernel(*refs):
    ng = len(DILATIONS)
    o_refs, l_refs = refs[0:2 * ng:2], refs[1:2 * ng:2]
    out_ref, os_ref, ls_ref = refs[2 * ng:]
    t = out_ref.shape[0]
    for g, dil in enumerate(DILATIONS):
        if dil == 1:
            continue
        for r in range(dil):
            dst = pl.ds(r, t // dil, stride=dil)
            ls_ref[g, dst, :] = l_refs[g][r]
            for h in range(HEADS_PER_GROUP):
                os_ref[g, h, dst, :] = o_refs[g][r, :, _head_cols(h)].astype(F32)

    def lse_of(g):
        return l_refs[g][0] if DILATIONS[g] == 1 else ls_ref[g]

    top = functools.reduce(jnp.maximum, [lse_of(g) for g in range(ng)])
    e = [jnp.exp(lse_of(g) - top) for g in range(ng)]
    inv = 1.0 / functools.reduce(lambda a, b: a + b, e)
    w = [eg * inv for eg in e]
    for h in range(HEADS_PER_GROUP):
        acc = None
        for g, dil in enumerate(DILATIONS):
            og = o_refs[g][0, :, _head_cols(h)].astype(F32) if dil == 1 else os_ref[g, h]
            term = w[g][:, h * LSE_LANES:h * LSE_LANES + 1] * og
            acc = term if acc is None else acc + term
        out_ref[:, _head_cols(h)] = acc.astype(out_ref.dtype)


def _combine(outs):
    ng = len(DILATIONS)
    s = outs[0][0].shape[0] * outs[0][0].shape[1]
    t = _tile(s, 512)
    args, in_specs = [], []
    for (o, l), dil in zip(outs, DILATIONS):
        args += [o, l]
        in_specs += [pl.BlockSpec((dil, t // dil, GROUP_WIDTH), lambda i: (0, i, 0)),
                     pl.BlockSpec((dil, t // dil, HEAD_DIM), lambda i: (0, i, 0))]
    return pl.pallas_call(
        _combine_kernel,
        out_shape=jax.ShapeDtypeStruct((s, GROUP_WIDTH), BF16),
        grid=(s // t,),
        in_specs=in_specs,
        out_specs=pl.BlockSpec((t, GROUP_WIDTH), lambda i: (i, 0)),
        scratch_shapes=[pltpu.VMEM((ng, HEADS_PER_GROUP, t, HEAD_DIM), F32),
                        pltpu.VMEM((ng, t, HEAD_DIM), F32)],
        compiler_params=_params("parallel"),
        name="attn_combine",
    )(*args)


def _dft_tables(s, cg):
    n1, n2 = DFT_ROWS, s // DFT_ROWS

    def cs(rows, cols, period):
        ang = 2.0 * np.pi * ((np.arange(rows)[:, None] * np.arange(cols)[None, :]) % period) / period
        return np.cos(ang), np.sin(ang)

    c1, s1 = cs(n1, n1, n1)
    c2, s2 = cs(n2, n2, n2)
    ct, st = cs(n1, n2, s)
    cc, sc = cs(cg, cg, cg)
    norm = 1.0 / math.sqrt(s * cg)
    as_bf16 = lambda a: jnp.asarray(a, F32).astype(BF16)
    return dict(
        w1=as_bf16(np.concatenate([c1, -s1], axis=0)),
        tw_cos=jnp.asarray(ct, F32), tw_sin=jnp.asarray(st, F32),
        w2_re=as_bf16(np.concatenate([c2, -s2], axis=0)),
        w2_im=as_bf16(np.concatenate([s2, c2], axis=0)),
        wc_re=as_bf16(cc * norm), wc_im=as_bf16(sc * norm))


LANES = 128


def _flatten_slabs(dst_ref, src_refs):
    for k, src in enumerate(src_refs):
        n = src.shape[0]
        dst_ref[k] = src[...].reshape(n * DFT_SLAB, LANES)


def _slab(ref, k, row, n):
    return ref[k, pl.ds(row, n, stride=DFT_SLAB), :]


def _lane_tiles(x, nt):
    return [x[:, t * LANES:(t + 1) * LANES] for t in range(nt)]


def _dft_stage1_kernel(*refs, nt):
    x_refs, (w1_ref, twc_ref, tws_ref, br_ref, bi_ref, xs_ref) = refs[:nt], refs[nt:]
    n1 = w1_ref.shape[1]
    _flatten_slabs(xs_ref, x_refs)
    lane = lax.broadcasted_iota(jnp.int32, twc_ref.shape, 1)
    for b in range(DFT_SLAB):
        n2 = pl.program_id(0) * DFT_SLAB + b
        xb = jnp.concatenate([_slab(xs_ref, k, b, n1).astype(BF16) for k in range(nt)], axis=1)
        a = jnp.dot(w1_ref[...], xb, preferred_element_type=F32)
        ar, ai = a[:n1], a[n1:]
        c = jnp.sum(jnp.where(lane == n2, twc_ref[...], 0.0), axis=1, keepdims=True)
        sn = jnp.sum(jnp.where(lane == n2, tws_ref[...], 0.0), axis=1, keepdims=True)
        br_ref[b] = ar * c + ai * sn
        bi_ref[b] = ai * c - ar * sn


def _dft_stage2_kernel(*refs, nt):
    br_refs, bi_refs = refs[:nt], refs[nt:2 * nt]
    w2r_ref, w2i_ref, wcr_ref, wci_ref, o_ref, bs_ref, scr_ref = refs[2 * nt:]
    n2 = w2r_ref.shape[1]
    cg = wcr_ref.shape[0]
    _flatten_slabs(bs_ref, br_refs + bi_refs)
    for kk in range(DFT_SLAB):
        br = jnp.concatenate([_slab(bs_ref, k, kk, n2).astype(BF16) for k in range(nt)], axis=1)
        bi = jnp.concatenate([_slab(bs_ref, nt + k, kk, n2).astype(BF16) for k in range(nt)], axis=1)
        z = (jnp.dot(w2r_ref[...], br, preferred_element_type=F32)
             + jnp.dot(w2i_ref[...], bi, preferred_element_type=F32))
        zr, zi = z[:n2].astype(BF16), z[n2:].astype(BF16)
        groups = []
        for g in range(nt * LANES // cg):
            cols = slice(g * cg, (g + 1) * cg)
            groups.append(jnp.dot(zr[:, cols], wcr_ref[...], preferred_element_type=F32)
                          + jnp.dot(zi[:, cols], wci_ref[...], preferred_element_type=F32))
        out = groups[0] if len(groups) == 1 else jnp.concatenate(groups, axis=1)
        for t, tile in enumerate(_lane_tiles(out, nt)):
            scr_ref[t, pl.ds(kk, n2, stride=DFT_SLAB), :] = tile
    for t in range(nt):
        o_ref[:, :, t * LANES:(t + 1) * LANES] = scr_ref[t].reshape(n2, DFT_SLAB, LANES)


def _fourier_mix(f):
    s, width = f.shape
    n1, n2 = DFT_ROWS, s // DFT_ROWS
    cg = width // FOURIER_GROUPS
    t = _dft_tables(s, cg)
    slab = DFT_SLAB
    full = lambda a: pl.BlockSpec(a.shape, lambda i, j: (0,) * a.ndim)

    def tile_specs(rows, nt, place):
        return [pl.BlockSpec((rows, slab, LANES), functools.partial(place, t=k)) for k in range(nt)]

    nt1 = min(4, width // LANES)
    br, bi = pl.pallas_call(
        functools.partial(_dft_stage1_kernel, nt=nt1),
        out_shape=(jax.ShapeDtypeStruct((n2, n1, width), F32),) * 2,
        grid=(n2 // slab, width // (nt1 * LANES)),
        in_specs=tile_specs(n1, nt1, lambda i, j, t: (0, i, j * nt1 + t))
        + [full(t["w1"]), full(t["tw_cos"]), full(t["tw_sin"])],
        out_specs=(pl.BlockSpec((slab, n1, nt1 * LANES), lambda i, j: (i, 0, j)),) * 2,
        scratch_shapes=[pltpu.VMEM((nt1, n1 * slab, LANES), F32)],
        compiler_params=_params("parallel", "parallel"),
        name="dft_stage1",
    )(*([f.reshape(n1, n2, width)] * nt1), t["w1"], t["tw_cos"], t["tw_sin"])

    nt2 = width // LANES
    stage2_in = tile_specs(n2, nt2, lambda i, j, t: (0, i, j * nt2 + t))
    out = pl.pallas_call(
        functools.partial(_dft_stage2_kernel, nt=nt2),
        out_shape=jax.ShapeDtypeStruct((n2, n1, width), F32),
        grid=(n1 // slab, width // (nt2 * LANES)),
        in_specs=stage2_in + stage2_in
        + [full(t["w2_re"]), full(t["w2_im"]), full(t["wc_re"]), full(t["wc_im"])],
        out_specs=pl.BlockSpec((n2, slab, nt2 * LANES), lambda i, j: (0, i, j)),
        scratch_shapes=[pltpu.VMEM((2 * nt2, n2 * slab, LANES), F32),
                        pltpu.VMEM((nt2, n2 * slab, LANES), F32)],
        compiler_params=_params("parallel", "parallel"),
        name="dft_stage2",
    )(*([br] * nt2), *([bi] * nt2), t["w2_re"], t["w2_im"], t["wc_re"], t["wc_im"])
    return out.reshape(s, width)


def _merge_kernel(a_ref, f_ref, wa_ref, wf_ref, ga_ref, gf_ref, o_ref):
    ab = jnp.dot(a_ref[...], wa_ref[...], preferred_element_type=F32)
    fb = jnp.dot(f_ref[...].astype(BF16), wf_ref[...], preferred_element_type=F32)
    o_ref[...] = (ga_ref[...].astype(F32) * ab + gf_ref[...].astype(F32) * fb).astype(o_ref.dtype)


def _merge(attn, four, wa, wf, ga, gf):
    s, ka = attn.shape
    kf = four.shape[1]
    n = wa.shape[1]
    tm, tn = _tile(s, 1024), _tile(n, 1024)
    return pl.pallas_call(
        _merge_kernel,
        out_shape=jax.ShapeDtypeStruct((s, n), BF16),
        grid=(s // tm, n // tn),
        in_specs=[pl.BlockSpec((tm, ka), lambda i, j: (i, 0)),
                  pl.BlockSpec((tm, kf), lambda i, j: (i, 0)),
                  pl.BlockSpec((ka, tn), lambda i, j: (0, j)),
                  pl.BlockSpec((kf, tn), lambda i, j: (0, j)),
                  pl.BlockSpec((tm, tn), lambda i, j: (i, j)),
                  pl.BlockSpec((tm, tn), lambda i, j: (i, j))],
        out_specs=pl.BlockSpec((tm, tn), lambda i, j: (i, j)),
        compiler_params=_params("parallel", "arbitrary"),
        name="branch_merge",
    )(attn, four, wa, wf, ga, gf)


def _proj_res_kernel(a_ref, w_ref, res_ref, gate_ref, o_ref, *, alpha):
    acc = jnp.dot(a_ref[...], w_ref[...], preferred_element_type=F32)
    o_ref[...] = alpha * res_ref[...] + gate_ref[...] * acc


def _proj_res(a, w, res, gate, alpha, tm_want, tn_want, sides=()):
    s, kdim = a.shape
    n = w.shape[1]
    tm, tn = _tile(s, tm_want), _tile(n, tn_want)
    (out,), side_out = _host_call(
        functools.partial(_proj_res_kernel, alpha=alpha),
        grid=(s // tm, n // tn),
        in_specs=[pl.BlockSpec((tm, kdim), lambda i, j: (i, 0)),
                  pl.BlockSpec((kdim, tn), lambda i, j: (0, j)),
                  pl.BlockSpec((tm, tn), lambda i, j: (i, j)),
                  pl.BlockSpec((1, tn), lambda i, j: (0, j))],
        out_specs=[pl.BlockSpec((tm, tn), lambda i, j: (i, j))],
        out_shape=[jax.ShapeDtypeStruct((s, n), F32)],
        scratch=[], args=(a, w, res, gate), sem=("parallel", "arbitrary"), name="proj_res", sides=sides)
    return out, side_out


def _ln_out_kernel(t_ref, g_ref, b_ref, *refs):
    for rows in _row_chunks(t_ref.shape[0]):
        y = _normalize(t_ref[rows, :]) * g_ref[...] + b_ref[...]
        if len(refs) == 1:
            refs[0][rows, :] = y
        else:
            mod_ref, y_ref, h_ref = refs
            y_ref[rows, :] = y
            h_ref[rows, :] = (_normalize(y) * (1.0 + mod_ref[1:2, :]) + mod_ref[0:1, :]).astype(h_ref.dtype)


def _ln_out(t, g, b, mod3=None):
    s, d = t.shape
    tm = _tile(s, 256)
    row = pl.BlockSpec((tm, d), lambda i: (i, 0))
    vec = lambda rows: pl.BlockSpec((rows, d), lambda i: (0, 0))
    args, in_specs = [t, g, b], [row, vec(1), vec(1)]
    out_shape, out_specs = [jax.ShapeDtypeStruct((s, d), F32)], [row]
    if mod3 is not None:
        args.append(mod3)
        in_specs.append(vec(3))
        out_shape.append(jax.ShapeDtypeStruct((s, d), BF16))
        out_specs.append(row)
    return pl.pallas_call(
        _ln_out_kernel,
        out_shape=tuple(out_shape),
        grid=(s // tm,),
        in_specs=in_specs,
        out_specs=tuple(out_specs),
        compiler_params=_params("parallel"),
        name="ln_out",
    )(*args)


def _swiglu_kernel(x_ref, wg_ref, wu_ref, o_ref):
    x = x_ref[...]
    gpre = jnp.dot(x, wg_ref[...], preferred_element_type=F32)
    up = jnp.dot(x, wu_ref[...], preferred_element_type=F32)
    o_ref[...] = (gpre * _sigmoid(gpre) * up).astype(o_ref.dtype)


def _swiglu(h, wg, wu, sides=()):
    s, kdim = h.shape
    n = wg.shape[1]
    tm, tn = _tile(s, 1024), _tile(n, 256)
    (out,), side_out = _host_call(
        _swiglu_kernel,
        grid=(s // tm, n // tn),
        in_specs=[pl.BlockSpec((tm, kdim), lambda i, j: (i, 0)),
                  pl.BlockSpec((kdim, tn), lambda i, j: (0, j)),
                  pl.BlockSpec((kdim, tn), lambda i, j: (0, j))],
        out_specs=[pl.BlockSpec((tm, tn), lambda i, j: (i, j))],
        out_shape=[jax.ShapeDtypeStruct((s, n), BF16)],
        scratch=[], args=(h, wg, wu), sem=("parallel", "arbitrary"), name="swiglu", sides=sides)
    return out, side_out


def _mixer(x, h, mod6, wts, alpha, sides=None):
    sides = sides or {}
    s, d = x.shape
    cos, sin = _rope_tables(s)
    fw = d // FOURIER_WIDTH_DIVISOR
    f, _ = _proj(h, wts["w_in"], 3 * ATTN_WIDTH, fw, False, F32)
    ga, out_ga = _proj(h, wts["w_in"], 3 * ATTN_WIDTH + fw, d, True, BF16, sides.get("ga", ()))
    gf, out_gf = _proj(h, wts["w_in"], 3 * ATTN_WIDTH + fw + d, d, True, BF16, sides.get("gf", ()))
    outs = [_attn_group(_qkv_proj(h, wts["w_qkv"], gi, cos, sin)) for gi in range(len(DILATIONS))]
    attn = _combine(outs)
    four = _fourier_mix(f)

    merged = _merge(attn, four, wts["w_attn_up"], wts["w_fourier_up"], ga, gf)
    t1, out_mix = _proj_res(merged, wts["w_mix_out"], x, mod6[2:3], alpha, 1024, 512, sides.get("mix", ()))
    return t1, dict(ga=out_ga, gf=out_gf, mix=out_mix)


def kernel(x_prompt, x_sample, c_prompt, c_sample, w_ada, b_ada, w_in, w_attn_up, w_fourier_up,
           w_mix_out, ln1_g, ln1_b, w_gate, w_up, w_down, ln2_g, ln2_b):
    depth = w_ada.shape[0]
    d = x_prompt.shape[-1]
    alpha = (2.0 * depth) ** 0.25
    xp, xs = x_prompt[0], x_sample[0]
    c2 = jnp.concatenate([c_prompt, c_sample], axis=0)
    row = lambda v: v.reshape(1, d)
    for l in range(depth):
        mod = _ada_mod(c2, w_ada[l], b_ada[l]).reshape(2, N_MOD, d)
        mp, ms = mod[0], mod[1]
        wts = dict(w_qkv=w_in[l][:, :3 * ATTN_WIDTH].astype(BF16), w_in=w_in[l],
                   w_attn_up=w_attn_up[l].astype(BF16), w_fourier_up=w_fourier_up[l].astype(BF16),
                   w_mix_out=w_mix_out[l].astype(BF16))
        g1, b1, g2, b2 = row(ln1_g[l]), row(ln1_b[l]), row(ln2_g[l]), row(ln2_b[l])

        hp = _ln_mod(xp, mp[0:3])
        t1p, done = _mixer(xp, hp, mp, wts, alpha, dict(
            ga=[_side_job(_ln_mod_kernel, xs, [BF16], [ms[0:3]])],
            gf=[_side_cast(w_gate[l])], mix=[_side_cast(w_up[l])]))
        (hs,), (wg,), (wu,) = done["ga"][0], done["gf"][0], done["mix"][0]
        t1s, _ = _mixer(xs, hs, ms, wts, alpha)
        x1p, h2p = _ln_out(t1p, g1, b1, mp[3:6])
        up, ((x1s, h2s), (wd,)) = _swiglu(h2p, wg, wu, [
            _side_job(_ln_out_kernel, t1s, [F32, BF16], [g1, b1, ms[3:6]]), _side_cast(w_down[l])])
        t2p, _ = _proj_res(up, wd, x1p, mp[5:6], alpha, 512, 512)
        us, ((xp,),) = _swiglu(h2s, wg, wu, [_side_job(_ln_out_kernel, t2p, [F32], [g2, b2])])
        t2s, _ = _proj_res(us, wd, x1s, ms[5:6], alpha, 512, 512)
        (xs,) = _ln_out(t2s, g2, b2)
    return (xp[None], xs[None])
```

```python
import functools
import math

import numpy as np
import jax
import jax.numpy as jnp
from jax import lax
from jax.experimental import pallas as pl
from jax.experimental.pallas import tpu as pltpu

HEAD_DIM = 128
HEADS_PER_GROUP = 8
GROUP_WIDTH = HEADS_PER_GROUP * HEAD_DIM
DILATIONS = (1, 4, 16)
ATTN_RADIUS = 64
ATTN_WIDTH = len(DILATIONS) * GROUP_WIDTH
FOURIER_WIDTH_DIVISOR = 4
FOURIER_GROUPS = 4
DFT_ROWS = 128
DFT_SLAB = 8
N_MOD = 6
ROPE_THETA = 10000.0
LN_EPS = 1e-5
NEG_INF = -1e30
Q_SUB = 128
LSE_LANES = HEAD_DIM // HEADS_PER_GROUP

VMEM_LIMIT_BYTES = 56 * 1024 * 1024

F32 = jnp.float32
BF16 = jnp.bfloat16


def _params(*sem):
    return pltpu.CompilerParams(dimension_semantics=sem, vmem_limit_bytes=VMEM_LIMIT_BYTES)


def _tile(n, want):
    t = min(n, want)
    while n % t:
        t //= 2
    return t


def _head_cols(h):
    return slice(h * HEAD_DIM, (h + 1) * HEAD_DIM)


def _sigmoid(x):
    return 0.5 * jnp.tanh(0.5 * x) + 0.5


SIDE_MIN_ROWS = 16


def _side_job(body, rows_arg, out_dtypes, const_args, cols=None):
    def build(n_steps, step_of):
        s, d = rows_arg.shape[0], cols or rows_arg.shape[1]
        sr = SIDE_MIN_ROWS
        while s // sr > n_steps:
            sr *= 2
        last = s // sr - 1
        row = pl.BlockSpec((sr, d), lambda *idx: (jnp.minimum(step_of(*idx), last), 0))
        consts = [pl.BlockSpec(c.shape, lambda *idx: (0, 0)) for c in const_args]
        return dict(body=body, args=[rows_arg, *const_args], in_specs=[row] + consts,
                    out_shape=[jax.ShapeDtypeStruct((s, d), dt) for dt in out_dtypes],
                    out_specs=[row] * len(out_dtypes))
    return build


def _host_call(body, grid, in_specs, out_specs, out_shape, scratch, args, sem, name, sides=()):
    n_in, n_out = len(in_specs), len(out_shape)
    if not sides:
        outs = pl.pallas_call(body, out_shape=tuple(out_shape), grid=grid, in_specs=in_specs,
                              out_specs=tuple(out_specs), scratch_shapes=scratch,
                              compiler_params=_params(*sem), name=name)(*args)
        return tuple(outs), []

    def step_of(*idx):
        step = idx[0]
        for axis in range(1, len(grid)):
            step = step * grid[axis] + idx[axis]
        return step

    jobs = [side(math.prod(grid), step_of) for side in sides]
    job_in = [spec for job in jobs for spec in job["in_specs"]]
    job_out = [spec for job in jobs for spec in job["out_specs"]]
    job_shape = [shape for job in jobs for shape in job["out_shape"]]
    job_args = [arg for job in jobs for arg in job["args"]]
    in_end, out_end = n_in + len(job_in), n_in + len(job_in) + n_out

    def fused(*refs):
        body(*refs[:n_in], *refs[in_end:out_end], *refs[out_end + len(job_out):])
        i, o = n_in, out_end
        for job in jobs:
            ni, no = len(job["in_specs"]), len(job["out_specs"])
            job["body"](*refs[i:i + ni], *refs[o:o + no])
            i, o = i + ni, o + no

    outs = pl.pallas_call(fused, out_shape=tuple(out_shape) + tuple(job_shape), grid=grid,
                          in_specs=list(in_specs) + job_in, out_specs=tuple(out_specs) + tuple(job_out),
                          scratch_shapes=scratch, compiler_params=_params(*("arbitrary",) * len(grid)),
                          name=name)(*args, *job_args)
    side_outs, o = [], n_out
    for job in jobs:
        side_outs.append(tuple(outs[o:o + len(job["out_specs"])]))
        o += len(job["out_specs"])
    return tuple(outs[:n_out]), side_outs


def _cast_kernel(x_ref, o_ref):
    o_ref[...] = x_ref[...].astype(o_ref.dtype)


def _side_cast(w, cols=None):
    return _side_job(_cast_kernel, w, [BF16], [], cols)


def _ada_kernel(ct_ref, w_ref, b_ref, o_ref, sb_ref):
    kdim, tn = w_ref.shape
    nt = tn // 128

    @pl.when(pl.program_id(0) == 0)
    def _():
        c = ct_ref[...]
        s = c * jax.nn.sigmoid(c)
        sb_ref[0] = jnp.broadcast_to(s[:, 0:1], (kdim, 128))
        sb_ref[1] = jnp.broadcast_to(s[:, 1:2], (kdim, 128))

    def body(kc, acc):
        r0 = pl.multiple_of(kc * 8, 8)
        s0 = sb_ref[0, pl.ds(r0, 8), :]
        s1 = sb_ref[1, pl.ds(r0, 8), :]
        out = []
        for t in range(nt):
            w = w_ref[pl.ds(r0, 8), t * 128:(t + 1) * 128]
            out.append(acc[2 * t] + w * s0)
            out.append(acc[2 * t + 1] + w * s1)
        return tuple(out)

    zero = jnp.zeros((8, 128), F32)
    acc = lax.fori_loop(0, kdim // 8, body, (zero,) * (2 * nt), unroll=8)
    for t in range(nt):
        cols = slice(t * 128, (t + 1) * 128)
        o_ref[0:1, cols] = jnp.sum(acc[2 * t], axis=0, keepdims=True) + b_ref[:, cols]
        o_ref[1:2, cols] = jnp.sum(acc[2 * t + 1], axis=0, keepdims=True) + b_ref[:, cols]


def _ada_mod(c2, w, b):
    kdim, n = w.shape
    tn = _tile(n, 1024)
    return pl.pallas_call(
        _ada_kernel,
        out_shape=jax.ShapeDtypeStruct((2, n), F32),
        grid=(n // tn,),
        in_specs=[pl.BlockSpec((kdim, 2), lambda j: (0, 0)),
                  pl.BlockSpec((kdim, tn), lambda j: (0, j)),
                  pl.BlockSpec((1, tn), lambda j: (0, j))],
        out_specs=pl.BlockSpec((2, tn), lambda j: (0, j)),
        scratch_shapes=[pltpu.VMEM((2, kdim, 128), F32)],
        compiler_params=_params("arbitrary"),
        name="ada_mod",
    )(c2.T, w, b.reshape(1, n))


def _normalize(x):
    mu = jnp.mean(x, axis=-1, keepdims=True)
    xc = x - mu
    var = jnp.mean(xc * xc, axis=-1, keepdims=True)
    return xc * lax.rsqrt(var + LN_EPS)


LN_CHUNK = 16


def _row_chunks(n):
    step = min(n, LN_CHUNK)
    return [slice(r, r + step) for r in range(0, n, step)]


def _ln_mod_kernel(x_ref, mod_ref, o_ref):
    for rows in _row_chunks(x_ref.shape[0]):
        y = _normalize(x_ref[rows, :])
        o_ref[rows, :] = (y * (1.0 + mod_ref[1:2, :]) + mod_ref[0:1, :]).astype(o_ref.dtype)


def _ln_mod(x, mod3):
    s, d = x.shape
    tm = _tile(s, 256)
    return pl.pallas_call(
        _ln_mod_kernel,
        out_shape=jax.ShapeDtypeStruct((s, d), BF16),
        grid=(s // tm,),
        in_specs=[pl.BlockSpec((tm, d), lambda i: (i, 0)),
                  pl.BlockSpec((3, d), lambda i: (0, 0))],
        out_specs=pl.BlockSpec((tm, d), lambda i: (i, 0)),
        compiler_params=_params("parallel"),
        name="ln_mod",
    )(x, mod3)


def _proj_kernel(x_ref, w_ref, o_ref, wb_ref, *, sigmoid):
    @pl.when(pl.program_id(1) == 0)
    def _():
        wb_ref[...] = w_ref[...].astype(BF16)

    acc = jnp.dot(x_ref[...], wb_ref[...], preferred_element_type=F32)
    if sigmoid:
        acc = _sigmoid(acc)
    o_ref[...] = acc.astype(o_ref.dtype)


def _proj(h, w, col0, ncols, sigmoid, out_dtype, sides=(), tm_want=1024, tn_want=512):
    s, kdim = h.shape
    tm, tn = _tile(s, tm_want), _tile(math.gcd(col0, ncols), tn_want)
    j0 = col0 // tn
    (out,), side_out = _host_call(
        functools.partial(_proj_kernel, sigmoid=sigmoid),
        grid=(ncols // tn, s // tm),
        in_specs=[pl.BlockSpec((tm, kdim), lambda j, i: (i, 0)),
                  pl.BlockSpec((kdim, tn), lambda j, i: (0, j0 + j))],
        out_specs=[pl.BlockSpec((tm, tn), lambda j, i: (i, j))],
        out_shape=[jax.ShapeDtypeStruct((s, ncols), out_dtype)],
        scratch=[pltpu.VMEM((kdim, tn), BF16)], args=(h, w), sem=("parallel", "arbitrary"),
        name="in_proj", sides=sides)
    return out, side_out


def _qkv_kernel(x_ref, w_ref, cos_ref, sin_ref, o_ref, *scratch, dil, q_scale):
    j = pl.program_id(1)
    acc = jnp.dot(x_ref[...], w_ref[...], preferred_element_type=F32)
    tm = acc.shape[0]

    scale = jnp.where(j == 0, q_scale, 1.0)
    cos = jnp.where(j < 2, cos_ref[...] * scale, 1.0)
    sin = jnp.where(j < 2, sin_ref[...] * scale, 0.0)
    for h in range(HEADS_PER_GROUP):
        t = acc[:, _head_cols(h)]
        t = t * cos + pltpu.roll(t, HEAD_DIM // 2, 1) * sin
        if dil == 1:
            o_ref[0, :, _head_cols(h)] = t.astype(o_ref.dtype)
        else:
            scratch[0][h] = t

    if dil > 1:
        for r in range(dil):
            for h in range(HEADS_PER_GROUP):
                rows = scratch[0][h, pl.ds(r, tm // dil, stride=dil), :]
                o_ref[r, :, _head_cols(h)] = rows.astype(o_ref.dtype)


def _qkv_proj(h, w_qkv, gi, cos, sin):
    s, kdim = h.shape
    dil = DILATIONS[gi]
    tm = _tile(s, 1024)
    ncol = ATTN_WIDTH // GROUP_WIDTH
    scratch = [] if dil == 1 else [pltpu.VMEM((HEADS_PER_GROUP, tm, HEAD_DIM), F32)]
    (out,), _ = _host_call(
        functools.partial(_qkv_kernel, dil=dil, q_scale=HEAD_DIM ** -0.5),
        grid=(s // tm, 3),
        in_specs=[pl.BlockSpec((tm, kdim), lambda i, j: (i, 0)),
                  pl.BlockSpec((kdim, GROUP_WIDTH), lambda i, j: (0, ncol * j + gi)),
                  pl.BlockSpec((tm, HEAD_DIM), lambda i, j: (i, 0)),
                  pl.BlockSpec((tm, HEAD_DIM), lambda i, j: (i, 0))],
        out_specs=[pl.BlockSpec((None, dil, tm // dil, GROUP_WIDTH), lambda i, j: (j, 0, i, 0))],
        out_shape=[jax.ShapeDtypeStruct((3, dil, s // dil, GROUP_WIDTH), BF16)],
        scratch=scratch, args=(h, w_qkv, cos, sin), sem=("parallel", "arbitrary"),
        name=f"qkv_proj_{dil}")
    return out


def _rope_tables(s):
    half = HEAD_DIM // 2
    inv = ROPE_THETA ** (-jnp.arange(half, dtype=F32) / half)
    ang = jnp.arange(s).astype(F32)[:, None] * inv[None, :]
    cos, sin = jnp.cos(ang), jnp.sin(ang)
    return jnp.concatenate([cos, cos], axis=-1), jnp.concatenate([-sin, sin], axis=-1)


def _attn_kernel(q_ref, kp_ref, kc_ref, kn_ref, vp_ref, vc_ref, vn_ref, o_ref, l_ref, kw_ref, vw_ref,
                 *, tq, seq):
    r = ATTN_RADIUS
    kw_ref[0:r, :] = kp_ref[...]
    kw_ref[r:r + tq, :] = kc_ref[...]
    kw_ref[r + tq:, :] = kn_ref[...]
    vw_ref[0:r, :] = vp_ref[...]
    vw_ref[r:r + tq, :] = vc_ref[...]
    vw_ref[r + tq:, :] = vn_ref[...]

    base = pl.program_id(1) * tq
    nkeys = Q_SUB + 2 * r
    qi = lax.broadcasted_iota(jnp.int32, (Q_SUB, nkeys), 0)
    kj = lax.broadcasted_iota(jnp.int32, (Q_SUB, nkeys), 1)
    lane_head = lax.broadcasted_iota(jnp.int32, (Q_SUB, HEAD_DIM), 1) // LSE_LANES

    for sb in range(tq // Q_SUB):
        rows = slice(sb * Q_SUB, (sb + 1) * Q_SUB)
        first = base + sb * Q_SUB - r
        lo = jnp.maximum(qi, -first)
        hi = jnp.minimum(qi + 2 * r, seq - 1 - first)
        keep = (kj >= lo) & (kj <= hi)
        lse_tile = jnp.zeros((Q_SUB, HEAD_DIM), F32)
        for h in range(HEADS_PER_GROUP):
            cols = _head_cols(h)
            qs = q_ref[rows, cols]
            ks = kw_ref[sb * Q_SUB:sb * Q_SUB + nkeys, cols]
            vs = vw_ref[sb * Q_SUB:sb * Q_SUB + nkeys, cols]
            sc = lax.dot_general(qs, ks, (((1,), (1,)), ((), ())), preferred_element_type=F32)
            sc = jnp.where(keep, sc, NEG_INF)
            m = jnp.max(sc, axis=-1, keepdims=True)
            p = jnp.exp(sc - m)
            den = jnp.sum(p, axis=-1, keepdims=True)
            o = jnp.dot(p.astype(BF16), vs, preferred_element_type=F32) / den
            o_ref[rows, cols] = o.astype(o_ref.dtype)
            lse_tile = jnp.where(lane_head == h, m + jnp.log(den), lse_tile)
        l_ref[rows, :] = lse_tile


def _attn_group(qkv):
    _, dil, seq, _ = qkv.shape
    tq = _tile(seq, 1024)
    r = ATTN_RADIUS
    halo_per_tile = tq // r
    n_halo = seq // r

    def main(which):
        return pl.BlockSpec((None, None, tq, GROUP_WIDTH), lambda rr, lb: (which, rr, lb, 0))

    def before(which):
        return pl.BlockSpec((None, None, r, GROUP_WIDTH),
                            lambda rr, lb: (which, rr, jnp.maximum(lb * halo_per_tile - 1, 0), 0))

    def after(which):
        return pl.BlockSpec((None, None, r, GROUP_WIDTH),
                            lambda rr, lb: (which, rr, jnp.minimum((lb + 1) * halo_per_tile, n_halo - 1), 0))

    return pl.pallas_call(
        functools.partial(_attn_kernel, tq=tq, seq=seq),
        out_shape=(jax.ShapeDtypeStruct((dil, seq, GROUP_WIDTH), BF16),
                   jax.ShapeDtypeStruct((dil, seq, HEAD_DIM), F32)),
        grid=(dil, seq // tq),
        in_specs=[main(0), before(1), main(1), after(1), before(2), main(2), after(2)],
        out_specs=(pl.BlockSpec((None, tq, GROUP_WIDTH), lambda rr, lb: (rr, lb, 0)),
                   pl.BlockSpec((None, tq, HEAD_DIM), lambda rr, lb: (rr, lb, 0))),
        scratch_shapes=[pltpu.VMEM((tq + 2 * r, GROUP_WIDTH), BF16),
                        pltpu.VMEM((tq + 2 * r, GROUP_WIDTH), BF16)],
        compiler_params=_params("parallel", "arbitrary"),
        name=f"banded_attn_{dil}",
    )(qkv, qkv, qkv, qkv, qkv, qkv, qkv)


def _combine_kernel(*refs):
    ng = len(DILATIONS)
    o_refs, l_refs = refs[0:2 * ng:2], refs[1:2 * ng:2]
    out_ref, os_ref, ls_ref = refs[2 * ng:]
    t = out_ref.shape[0]
    for g, dil in enumerate(DILATIONS):
        if dil == 1:
            continue
        for r in range(dil):
            dst = pl.ds(r, t // dil, stride=dil)
            ls_ref[g, dst, :] = l_refs[g][r]
            for h in range(HEADS_PER_GROUP):
                os_ref[g, h, dst, :] = o_refs[g][r, :, _head_cols(h)].astype(F32)

    def lse_of(g):
        return l_refs[g][0] if DILATIONS[g] == 1 else ls_ref[g]

    top = functools.reduce(jnp.maximum, [lse_of(g) for g in range(ng)])
    e = [jnp.exp(lse_of(g) - top) for g in range(ng)]
    inv = 1.0 / functools.reduce(lambda a, b: a + b, e)
    w = [eg * inv for eg in e]
    for h in range(HEADS_PER_GROUP):
        acc = None
        for g, dil in enumerate(DILATIONS):
            og = o_refs[g][0, :, _head_cols(h)].astype(F32) if dil == 1 else os_ref[g, h]
            term = w[g][:, h * LSE_LANES:h * LSE_LANES + 1] * og
            acc = term if acc is None else acc + term
        out_ref[:, _head_cols(h)] = acc.astype(out_ref.dtype)


def _combine(outs):
    ng = len(DILATIONS)
    s = outs[0][0].shape[0] * outs[0][0].shape[1]
    t = _tile(s, 512)
    args, in_specs = [], []
    for (o, l), dil in zip(outs, DILATIONS):
        args += [o, l]
        in_specs += [pl.BlockSpec((dil, t // dil, GROUP_WIDTH), lambda i: (0, i, 0)),
                     pl.BlockSpec((dil, t // dil, HEAD_DIM), lambda i: (0, i, 0))]
    return pl.pallas_call(
        _combine_kernel,
        out_shape=jax.ShapeDtypeStruct((s, GROUP_WIDTH), BF16),
        grid=(s // t,),
        in_specs=in_specs,
        out_specs=pl.BlockSpec((t, GROUP_WIDTH), lambda i: (i, 0)),
        scratch_shapes=[pltpu.VMEM((ng, HEADS_PER_GROUP, t, HEAD_DIM), F32),
                        pltpu.VMEM((ng, t, HEAD_DIM), F32)],
        compiler_params=_params("parallel"),
        name="attn_combine",
    )(*args)


def _dft_tables(s, cg):
    n1, n2 = DFT_ROWS, s // DFT_ROWS

    def cs(rows, cols, period):
        ang = 2.0 * np.pi * ((np.arange(rows)[:, None] * np.arange(cols)[None, :]) % period) / period
        return np.cos(ang), np.sin(ang)

    c1, s1 = cs(n1, n1, n1)
    c2, s2 = cs(n2, n2, n2)
    ct, st = cs(n1, n2, s)
    cc, sc = cs(cg, cg, cg)
    norm = 1.0 / math.sqrt(s * cg)
    as_bf16 = lambda a: jnp.asarray(a, F32).astype(BF16)
    return dict(
        w1=as_bf16(np.concatenate([c1, -s1], axis=0)),
        tw_cos=jnp.asarray(ct, F32), tw_sin=jnp.asarray(st, F32),
        w2_re=as_bf16(np.concatenate([c2, -s2], axis=0)),
        w2_im=as_bf16(np.concatenate([s2, c2], axis=0)),
        wc_re=as_bf16(cc * norm), wc_im=as_bf16(sc * norm))


LANES = 128


def _flatten_slabs(dst_ref, src_refs):
    for k, src in enumerate(src_refs):
        n = src.shape[0]
        dst_ref[k] = src[...].reshape(n * DFT_SLAB, LANES)


def _slab(ref, k, row, n):
    return ref[k, pl.ds(row, n, stride=DFT_SLAB), :]


def _lane_tiles(x, nt):
    return [x[:, t * LANES:(t + 1) * LANES] for t in range(nt)]


def _dft_stage1_kernel(*refs, nt):
    x_refs, (w1_ref, twc_ref, tws_ref, br_ref, bi_ref, xs_ref) = refs[:nt], refs[nt:]
    n1 = w1_ref.shape[1]
    _flatten_slabs(xs_ref, x_refs)
    lane = lax.broadcasted_iota(jnp.int32, twc_ref.shape, 1)
    for b in range(DFT_SLAB):
        n2 = pl.program_id(0) * DFT_SLAB + b
        xb = jnp.concatenate([_slab(xs_ref, k, b, n1).astype(BF16) for k in range(nt)], axis=1)
        a = jnp.dot(w1_ref[...], xb, preferred_element_type=F32)
        ar, ai = a[:n1], a[n1:]
        c = jnp.sum(jnp.where(lane == n2, twc_ref[...], 0.0), axis=1, keepdims=True)
        sn = jnp.sum(jnp.where(lane == n2, tws_ref[...], 0.0), axis=1, keepdims=True)
        br_ref[b] = ar * c + ai * sn
        bi_ref[b] = ai * c - ar * sn


def _dft_stage2_kernel(*refs, nt):
    br_refs, bi_refs = refs[:nt], refs[nt:2 * nt]
    w2r_ref, w2i_ref, wcr_ref, wci_ref, o_ref, bs_ref, scr_ref = refs[2 * nt:]
    n2 = w2r_ref.shape[1]
    cg = wcr_ref.shape[0]
    _flatten_slabs(bs_ref, br_refs + bi_refs)
    for kk in range(DFT_SLAB):
        br = jnp.concatenate([_slab(bs_ref, k, kk, n2).astype(BF16) for k in range(nt)], axis=1)
        bi = jnp.concatenate([_slab(bs_ref, nt + k, kk, n2).astype(BF16) for k in range(nt)], axis=1)
        z = (jnp.dot(w2r_ref[...], br, preferred_element_type=F32)
             + jnp.dot(w2i_ref[...], bi, preferred_element_type=F32))
        zr, zi = z[:n2].astype(BF16), z[n2:].astype(BF16)
        groups = []
        for g in range(nt * LANES // cg):
            cols = slice(g * cg, (g + 1) * cg)
            groups.append(jnp.dot(zr[:, cols], wcr_ref[...], preferred_element_type=F32)
                          + jnp.dot(zi[:, cols], wci_ref[...], preferred_element_type=F32))
        out = groups[0] if len(groups) == 1 else jnp.concatenate(groups, axis=1)
        for t, tile in enumerate(_lane_tiles(out, nt)):
            scr_ref[t, pl.ds(kk, n2, stride=DFT_SLAB), :] = tile
    for t in range(nt):
        o_ref[:, :, t * LANES:(t + 1) * LANES] = scr_ref[t].reshape(n2, DFT_SLAB, LANES)


def _fourier_mix(f):
    s, width = f.shape
    n1, n2 = DFT_ROWS, s // DFT_ROWS
    cg = width // FOURIER_GROUPS
    t = _dft_tables(s, cg)
    slab = DFT_SLAB
    full = lambda a: pl.BlockSpec(a.shape, lambda i, j: (0,) * a.ndim)

    def tile_specs(rows, nt, place):
        return [pl.BlockSpec((rows, slab, LANES), functools.partial(place, t=k)) for k in range(nt)]

    nt1 = min(4, width // LANES)
    br, bi = pl.pallas_call(
        functools.partial(_dft_stage1_kernel, nt=nt1),
        out_shape=(jax.ShapeDtypeStruct((n2, n1, width), F32),) * 2,
        grid=(n2 // slab, width // (nt1 * LANES)),
        in_specs=tile_specs(n1, nt1, lambda i, j, t: (0, i, j * nt1 + t))
        + [full(t["w1"]), full(t["tw_cos"]), full(t["tw_sin"])],
        out_specs=(pl.BlockSpec((slab, n1, nt1 * LANES), lambda i, j: (i, 0, j)),) * 2,
        scratch_shapes=[pltpu.VMEM((nt1, n1 * slab, LANES), F32)],
        compiler_params=_params("parallel", "parallel"),
        name="dft_stage1",
    )(*([f.reshape(n1, n2, width)] * nt1), t["w1"], t["tw_cos"], t["tw_sin"])

    nt2 = width // LANES
    stage2_in = tile_specs(n2, nt2, lambda i, j, t: (0, i, j * nt2 + t))
    out = pl.pallas_call(
        functools.partial(_dft_stage2_kernel, nt=nt2),
        out_shape=jax.ShapeDtypeStruct((n2, n1, width), F32),
        grid=(n1 // slab, width // (nt2 * LANES)),
        in_specs=stage2_in + stage2_in
        + [full(t["w2_re"]), full(t["w2_im"]), full(t["wc_re"]), full(t["wc_im"])],
        out_specs=pl.BlockSpec((n2, slab, nt2 * LANES), lambda i, j: (0, i, j)),
        scratch_shapes=[pltpu.VMEM((2 * nt2, n2 * slab, LANES), F32),
                        pltpu.VMEM((nt2, n2 * slab, LANES), F32)],
        compiler_params=_params("parallel", "parallel"),
        name="dft_stage2",
    )(*([br] * nt2), *([bi] * nt2), t["w2_re"], t["w2_im"], t["wc_re"], t["wc_im"])
    return out.reshape(s, width)


def _merge_kernel(a_ref, f_ref, wa_ref, wf_ref, ga_ref, gf_ref, o_ref):
    ab = jnp.dot(a_ref[...], wa_ref[...], preferred_element_type=F32)
    fb = jnp.dot(f_ref[...].astype(BF16), wf_ref[...], preferred_element_type=F32)
    o_ref[...] = (ga_ref[...].astype(F32) * ab + gf_ref[...].astype(F32) * fb).astype(o_ref.dtype)


def _merge(attn, four, wa, wf, ga, gf):
    s, ka = attn.shape
    kf = four.shape[1]
    n = wa.shape[1]
    tm, tn = _tile(s, 1024), _tile(n, 1024)
    return pl.pallas_call(
        _merge_kernel,
        out_shape=jax.ShapeDtypeStruct((s, n), BF16),
        grid=(s // tm, n // tn),
        in_specs=[pl.BlockSpec((tm, ka), lambda i, j: (i, 0)),
                  pl.BlockSpec((tm, kf), lambda i, j: (i, 0)),
                  pl.BlockSpec((ka, tn), lambda i, j: (0, j)),
                  pl.BlockSpec((kf, tn), lambda i, j: (0, j)),
                  pl.BlockSpec((tm, tn), lambda i, j: (i, j)),
                  pl.BlockSpec((tm, tn), lambda i, j: (i, j))],
        out_specs=pl.BlockSpec((tm, tn), lambda i, j: (i, j)),
        compiler_params=_params("parallel", "arbitrary"),
        name="branch_merge",
    )(attn, four, wa, wf, ga, gf)


def _proj_res_kernel(a_ref, w_ref, res_ref, gate_ref, o_ref, *, alpha):
    acc = jnp.dot(a_ref[...], w_ref[...], preferred_element_type=F32)
    o_ref[...] = alpha * res_ref[...] + gate_ref[...] * acc


def _proj_res(a, w, res, gate, alpha, tm_want, tn_want, sides=()):
    s, kdim = a.shape
    n = w.shape[1]
    tm, tn = _tile(s, tm_want), _tile(n, tn_want)
    (out,), side_out = _host_call(
        functools.partial(_proj_res_kernel, alpha=alpha),
        grid=(s // tm, n // tn),
        in_specs=[pl.BlockSpec((tm, kdim), lambda i, j: (i, 0)),
                  pl.BlockSpec((kdim, tn), lambda i, j: (0, j)),
                  pl.BlockSpec((tm, tn), lambda i, j: (i, j)),
                  pl.BlockSpec((1, tn), lambda i, j: (0, j))],
        out_specs=[pl.BlockSpec((tm, tn), lambda i, j: (i, j))],
        out_shape=[jax.ShapeDtypeStruct((s, n), F32)],
        scratch=[], args=(a, w, res, gate), sem=("parallel", "arbitrary"), name="proj_res", sides=sides)
    return out, side_out


def _ln_out_kernel(t_ref, g_ref, b_ref, *refs):
    for rows in _row_chunks(t_ref.shape[0]):
        y = _normalize(t_ref[rows, :]) * g_ref[...] + b_ref[...]
        if len(refs) == 1:
            refs[0][rows, :] = y
        else:
            mod_ref, y_ref, h_ref = refs
            y_ref[rows, :] = y
            h_ref[rows, :] = (_normalize(y) * (1.0 + mod_ref[1:2, :]) + mod_ref[0:1, :]).astype(h_ref.dtype)


def _ln_out(t, g, b, mod3=None):
    s, d = t.shape
    tm = _tile(s, 256)
    row = pl.BlockSpec((tm, d), lambda i: (i, 0))
    vec = lambda rows: pl.BlockSpec((rows, d), lambda i: (0, 0))
    args, in_specs = [t, g, b], [row, vec(1), vec(1)]
    out_shape, out_specs = [jax.ShapeDtypeStruct((s, d), F32)], [row]
    if mod3 is not None:
        args.append(mod3)
        in_specs.append(vec(3))
        out_shape.append(jax.ShapeDtypeStruct((s, d), BF16))
        out_specs.append(row)
    return pl.pallas_call(
        _ln_out_kernel,
        out_shape=tuple(out_shape),
        grid=(s // tm,),
        in_specs=in_specs,
        out_specs=tuple(out_specs),
        compiler_params=_params("parallel"),
        name="ln_out",
    )(*args)


def _swiglu_kernel(x_ref, wg_ref, wu_ref, o_ref):
    x = x_ref[...]
    gpre = jnp.dot(x, wg_ref[...], preferred_element_type=F32)
    up = jnp.dot(x, wu_ref[...], preferred_element_type=F32)
    o_ref[...] = (gpre * _sigmoid(gpre) * up).astype(o_ref.dtype)


def _swiglu(h, wg, wu, sides=()):
    s, kdim = h.shape
    n = wg.shape[1]
    tm, tn = _tile(s, 1024), _tile(n, 256)
    (out,), side_out = _host_call(
        _swiglu_kernel,
        grid=(s // tm, n // tn),
        in_specs=[pl.BlockSpec((tm, kdim), lambda i, j: (i, 0)),
                  pl.BlockSpec((kdim, tn), lambda i, j: (0, j)),
                  pl.BlockSpec((kdim, tn), lambda i, j: (0, j))],
        out_specs=[pl.BlockSpec((tm, tn), lambda i, j: (i, j))],
        out_shape=[jax.ShapeDtypeStruct((s, n), BF16)],
        scratch=[], args=(h, wg, wu), sem=("parallel", "arbitrary"), name="swiglu", sides=sides)
    return out, side_out


def _gate_projs(h, w_in, d, sides):
    fw = d // FOURIER_WIDTH_DIVISOR
    f, _ = _proj(h, w_in, 3 * ATTN_WIDTH, fw, False, F32)
    ga, out_ga = _proj(h, w_in, 3 * ATTN_WIDTH + fw, d, True, BF16, sides.get("ga", ()))
    gf, out_gf = _proj(h, w_in, 3 * ATTN_WIDTH + fw + d, d, True, BF16, sides.get("gf", ()))
    return (f, ga, gf), dict(ga=out_ga, gf=out_gf)


def _mixer(x, h, gates, gate_m, w_qkv, wts, alpha, rope, sides=()):
    f, ga, gf = gates
    s = x.shape[0]
    cos, sin = rope[0][:s], rope[1][:s]
    outs = [_attn_group(_qkv_proj(h, w_qkv, gi, cos, sin)) for gi in range(len(DILATIONS))]
    merged = _merge(_combine(outs), _fourier_mix(f), wts["w_attn_up"], wts["w_fourier_up"], ga, gf)
    return _proj_res(merged, wts["w_mix_out"], x, gate_m, alpha, 1024, 512, sides)


def kernel(x_prompt, x_sample, c_prompt, c_sample, w_ada, b_ada, w_in, w_attn_up, w_fourier_up,
           w_mix_out, ln1_g, ln1_b, w_gate, w_up, w_down, ln2_g, ln2_b):
    depth = w_ada.shape[0]
    d = x_prompt.shape[-1]
    alpha = (2.0 * depth) ** 0.25
    xp, xs = x_prompt[0], x_sample[0]
    c2 = jnp.concatenate([c_prompt, c_sample], axis=0)
    row = lambda v: v.reshape(1, d)
    rope = _rope_tables(max(xp.shape[0], xs.shape[0]))
    for l in range(depth):
        mod = _ada_mod(c2, w_ada[l], b_ada[l]).reshape(2, N_MOD, d)
        mp, ms = mod[0], mod[1]
        wts = dict(w_attn_up=w_attn_up[l].astype(BF16), w_fourier_up=w_fourier_up[l].astype(BF16),
                   w_mix_out=w_mix_out[l].astype(BF16))
        g1, b1, g2, b2 = row(ln1_g[l]), row(ln1_b[l]), row(ln2_g[l]), row(ln2_b[l])

        hp = _ln_mod(xp, mp[0:3])
        gates_p, done = _gate_projs(hp, w_in[l], d, dict(
            ga=[_side_job(_ln_mod_kernel, xs, [BF16], [ms[0:3]]), _side_cast(w_in[l], 3 * ATTN_WIDTH)],
            gf=[_side_cast(w_gate[l])]))
        ((hs,), (w_qkv,)), ((wg,),) = done["ga"], done["gf"]
        t1p, ((wu,),) = _mixer(xp, hp, gates_p, mp[2:3], w_qkv, wts, alpha, rope, [_side_cast(w_up[l])])
        gates_s, _ = _gate_projs(hs, w_in[l], d, {})
        t1s, _ = _mixer(xs, hs, gates_s, ms[2:3], w_qkv, wts, alpha, rope)
        x1p, h2p = _ln_out(t1p, g1, b1, mp[3:6])
        up, ((x1s, h2s), (wd,)) = _swiglu(h2p, wg, wu, [
            _side_job(_ln_out_kernel, t1s, [F32, BF16], [g1, b1, ms[3:6]]), _side_cast(w_down[l])])
        t2p, _ = _proj_res(up, wd, x1p, mp[5:6], alpha, 512, 512)
        us, ((xp,),) = _swiglu(h2s, wg, wu, [_side_job(_ln_out_kernel, t2p, [F32], [g2, b2])])
        t2s, _ = _proj_res(us, wd, x1s, ms[5:6], alpha, 512, 512)
        (xs,) = _ln_out(t2s, g2, b2)
    return (xp[None], xs[None])
```

```python
import functools
import math

import numpy as np
import jax
import jax.numpy as jnp
from jax import lax
from jax.experimental import pallas as pl
from jax.experimental.pallas import tpu as pltpu

HEAD_DIM = 128
HEADS_PER_GROUP = 8
GROUP_WIDTH = HEADS_PER_GROUP * HEAD_DIM
DILATIONS = (1, 4, 16)
ATTN_RADIUS = 64
ATTN_WIDTH = len(DILATIONS) * GROUP_WIDTH
FOURIER_WIDTH_DIVISOR = 4
FOURIER_GROUPS = 4
DFT_ROWS = 128
DFT_SLAB = 8
N_MOD = 6
ROPE_THETA = 10000.0
LN_EPS = 1e-5
NEG_INF = -1e30
Q_SUB = 128
LSE_LANES = HEAD_DIM // HEADS_PER_GROUP

VMEM_LIMIT_BYTES = 56 * 1024 * 1024

ROW_TILE = 1024
COL_TILE = 512
MERGE_COL_TILE = 1024
FFN_COL_TILE = 256
DOWN_ROW_TILE = 512
LN_ROW_TILE = 512
ADA_COL_TILE = 1024
ATTN_Q_TILE = 1024
COMBINE_TILE = 512

F32 = jnp.float32
BF16 = jnp.bfloat16


def _params(*sem):
    return pltpu.CompilerParams(dimension_semantics=sem, vmem_limit_bytes=VMEM_LIMIT_BYTES)


def _tile(n, want):
    t = min(n, want)
    while n % t:
        t //= 2
    return t


def _head_cols(h):
    return slice(h * HEAD_DIM, (h + 1) * HEAD_DIM)


def _sigmoid(x):
    return 0.5 * jnp.tanh(0.5 * x) + 0.5


SIDE_MIN_ROWS = 16


def _side_job(body, rows_arg, out_dtypes, const_args, cols=None):
    def build(n_steps, step_of):
        s, d = rows_arg.shape[0], cols or rows_arg.shape[1]
        sr = SIDE_MIN_ROWS
        while s // sr > n_steps:
            sr *= 2
        last = s // sr - 1
        row = pl.BlockSpec((sr, d), lambda *idx: (jnp.minimum(step_of(*idx), last), 0))
        consts = [pl.BlockSpec(c.shape, lambda *idx: (0, 0)) for c in const_args]
        return dict(body=body, args=[rows_arg, *const_args], in_specs=[row] + consts,
                    out_shape=[jax.ShapeDtypeStruct((s, d), dt) for dt in out_dtypes],
                    out_specs=[row] * len(out_dtypes))
    return build


def _host_call(body, grid, in_specs, out_specs, out_shape, scratch, args, sem, name, sides=()):
    n_in, n_out = len(in_specs), len(out_shape)
    if not sides:
        outs = pl.pallas_call(body, out_shape=tuple(out_shape), grid=grid, in_specs=in_specs,
                              out_specs=tuple(out_specs), scratch_shapes=scratch,
                              compiler_params=_params(*sem), name=name)(*args)
        return tuple(outs), []

    def step_of(*idx):
        step = idx[0]
        for axis in range(1, len(grid)):
            step = step * grid[axis] + idx[axis]
        return step

    jobs = [side(math.prod(grid), step_of) for side in sides]
    job_in = [spec for job in jobs for spec in job["in_specs"]]
    job_out = [spec for job in jobs for spec in job["out_specs"]]
    job_shape = [shape for job in jobs for shape in job["out_shape"]]
    job_args = [arg for job in jobs for arg in job["args"]]
    in_end, out_end = n_in + len(job_in), n_in + len(job_in) + n_out

    def fused(*refs):
        body(*refs[:n_in], *refs[in_end:out_end], *refs[out_end + len(job_out):])
        i, o = n_in, out_end
        for job in jobs:
            ni, no = len(job["in_specs"]), len(job["out_specs"])
            job["body"](*refs[i:i + ni], *refs[o:o + no])
            i, o = i + ni, o + no

    outs = pl.pallas_call(fused, out_shape=tuple(out_shape) + tuple(job_shape), grid=grid,
                          in_specs=list(in_specs) + job_in, out_specs=tuple(out_specs) + tuple(job_out),
                          scratch_shapes=scratch, compiler_params=_params(*("arbitrary",) * len(grid)),
                          name=name)(*args, *job_args)
    side_outs, o = [], n_out
    for job in jobs:
        side_outs.append(tuple(outs[o:o + len(job["out_specs"])]))
        o += len(job["out_specs"])
    return tuple(outs[:n_out]), side_outs


def _cast_kernel(x_ref, o_ref):
    o_ref[...] = x_ref[...].astype(o_ref.dtype)


def _side_cast(w, cols=None):
    return _side_job(_cast_kernel, w, [BF16], [], cols)


def _ada_kernel(ct_ref, w_ref, b_ref, o_ref, sb_ref):
    kdim, tn = w_ref.shape
    nt = tn // 128

    @pl.when(pl.program_id(0) == 0)
    def _():
        c = ct_ref[...]
        s = c * jax.nn.sigmoid(c)
        sb_ref[0] = jnp.broadcast_to(s[:, 0:1], (kdim, 128))
        sb_ref[1] = jnp.broadcast_to(s[:, 1:2], (kdim, 128))

    def body(kc, acc):
        r0 = pl.multiple_of(kc * 8, 8)
        s0 = sb_ref[0, pl.ds(r0, 8), :]
        s1 = sb_ref[1, pl.ds(r0, 8), :]
        out = []
        for t in range(nt):
            w = w_ref[pl.ds(r0, 8), t * 128:(t + 1) * 128]
            out.append(acc[2 * t] + w * s0)
            out.append(acc[2 * t + 1] + w * s1)
        return tuple(out)

    zero = jnp.zeros((8, 128), F32)
    acc = lax.fori_loop(0, kdim // 8, body, (zero,) * (2 * nt), unroll=8)
    for t in range(nt):
        cols = slice(t * 128, (t + 1) * 128)
        o_ref[0:1, cols] = jnp.sum(acc[2 * t], axis=0, keepdims=True) + b_ref[:, cols]
        o_ref[1:2, cols] = jnp.sum(acc[2 * t + 1], axis=0, keepdims=True) + b_ref[:, cols]


def _ada_mod(c2, w, b):
    kdim, n = w.shape
    tn = _tile(n, ADA_COL_TILE)
    return pl.pallas_call(
        _ada_kernel,
        out_shape=jax.ShapeDtypeStruct((2, n), F32),
        grid=(n // tn,),
        in_specs=[pl.BlockSpec((kdim, 2), lambda j: (0, 0)),
                  pl.BlockSpec((kdim, tn), lambda j: (0, j)),
                  pl.BlockSpec((1, tn), lambda j: (0, j))],
        out_specs=pl.BlockSpec((2, tn), lambda j: (0, j)),
        scratch_shapes=[pltpu.VMEM((2, kdim, 128), F32)],
        compiler_params=_params("arbitrary"),
        name="ada_mod",
    )(c2.T, w, b.reshape(1, n))


def _normalize(x):
    mu = jnp.mean(x, axis=-1, keepdims=True)
    xc = x - mu
    var = jnp.mean(xc * xc, axis=-1, keepdims=True)
    return xc * lax.rsqrt(var + LN_EPS)


LN_CHUNK = 16


def _row_chunks(n):
    step = min(n, LN_CHUNK)
    return [slice(r, r + step) for r in range(0, n, step)]


def _ln_mod_kernel(x_ref, mod_ref, o_ref):
    for rows in _row_chunks(x_ref.shape[0]):
        y = _normalize(x_ref[rows, :])
        o_ref[rows, :] = (y * (1.0 + mod_ref[1:2, :]) + mod_ref[0:1, :]).astype(o_ref.dtype)


def _ln_mod(x, mod3):
    s, d = x.shape
    tm = _tile(s, LN_ROW_TILE)
    return pl.pallas_call(
        _ln_mod_kernel,
        out_shape=jax.ShapeDtypeStruct((s, d), BF16),
        grid=(s // tm,),
        in_specs=[pl.BlockSpec((tm, d), lambda i: (i, 0)),
                  pl.BlockSpec((3, d), lambda i: (0, 0))],
        out_specs=pl.BlockSpec((tm, d), lambda i: (i, 0)),
        compiler_params=_params("parallel"),
        name="ln_mod",
    )(x, mod3)


def _proj_kernel(x_ref, w_ref, o_ref, wb_ref, *, sigmoid):
    @pl.when(pl.program_id(1) == 0)
    def _():
        wb_ref[...] = w_ref[...].astype(BF16)

    acc = jnp.dot(x_ref[...], wb_ref[...], preferred_element_type=F32)
    if sigmoid:
        acc = _sigmoid(acc)
    o_ref[...] = acc.astype(o_ref.dtype)


def _proj(h, w, col0, ncols, sigmoid, out_dtype, sides=()):
    s, kdim = h.shape
    tm, tn = _tile(s, ROW_TILE), _tile(math.gcd(col0, ncols), COL_TILE)
    j0 = col0 // tn
    (out,), side_out = _host_call(
        functools.partial(_proj_kernel, sigmoid=sigmoid),
        grid=(ncols // tn, s // tm),
        in_specs=[pl.BlockSpec((tm, kdim), lambda j, i: (i, 0)),
                  pl.BlockSpec((kdim, tn), lambda j, i: (0, j0 + j))],
        out_specs=[pl.BlockSpec((tm, tn), lambda j, i: (i, j))],
        out_shape=[jax.ShapeDtypeStruct((s, ncols), out_dtype)],
        scratch=[pltpu.VMEM((kdim, tn), BF16)], args=(h, w), sem=("parallel", "arbitrary"),
        name="in_proj", sides=sides)
    return out, side_out


def _qkv_kernel(x_ref, w_ref, cos_ref, sin_ref, o_ref, *scratch, dil, q_scale):
    j = pl.program_id(1)
    acc = jnp.dot(x_ref[...], w_ref[...], preferred_element_type=F32)
    tm = acc.shape[0]

    scale = jnp.where(j == 0, q_scale, 1.0)
    cos = jnp.where(j < 2, cos_ref[...] * scale, 1.0)
    sin = jnp.where(j < 2, sin_ref[...] * scale, 0.0)
    for h in range(HEADS_PER_GROUP):
        t = acc[:, _head_cols(h)]
        t = t * cos + pltpu.roll(t, HEAD_DIM // 2, 1) * sin
        if dil == 1:
            o_ref[0, :, _head_cols(h)] = t.astype(o_ref.dtype)
        else:
            scratch[0][h] = t

    if dil > 1:
        for r in range(dil):
            for h in range(HEADS_PER_GROUP):
                rows = scratch[0][h, pl.ds(r, tm // dil, stride=dil), :]
                o_ref[r, :, _head_cols(h)] = rows.astype(o_ref.dtype)


def _qkv_proj(h, w_qkv, gi, cos, sin):
    s, kdim = h.shape
    dil = DILATIONS[gi]
    tm = _tile(s, ROW_TILE)
    ncol = ATTN_WIDTH // GROUP_WIDTH
    scratch = [] if dil == 1 else [pltpu.VMEM((HEADS_PER_GROUP, tm, HEAD_DIM), F32)]
    (out,), _ = _host_call(
        functools.partial(_qkv_kernel, dil=dil, q_scale=HEAD_DIM ** -0.5),
        grid=(s // tm, 3),
        in_specs=[pl.BlockSpec((tm, kdim), lambda i, j: (i, 0)),
                  pl.BlockSpec((kdim, GROUP_WIDTH), lambda i, j: (0, ncol * j + gi)),
                  pl.BlockSpec((tm, HEAD_DIM), lambda i, j: (i, 0)),
                  pl.BlockSpec((tm, HEAD_DIM), lambda i, j: (i, 0))],
        out_specs=[pl.BlockSpec((None, dil, tm // dil, GROUP_WIDTH), lambda i, j: (j, 0, i, 0))],
        out_shape=[jax.ShapeDtypeStruct((3, dil, s // dil, GROUP_WIDTH), BF16)],
        scratch=scratch, args=(h, w_qkv, cos, sin), sem=("parallel", "arbitrary"),
        name=f"qkv_proj_{dil}")
    return out


def _rope_tables(s):
    half = HEAD_DIM // 2
    inv = ROPE_THETA ** (-jnp.arange(half, dtype=F32) / half)
    ang = jnp.arange(s).astype(F32)[:, None] * inv[None, :]
    cos, sin = jnp.cos(ang), jnp.sin(ang)
    return jnp.concatenate([cos, cos], axis=-1), jnp.concatenate([-sin, sin], axis=-1)


def _attn_kernel(q_ref, kp_ref, kc_ref, kn_ref, vp_ref, vc_ref, vn_ref, o_ref, l_ref, kw_ref, vw_ref,
                 *, tq, seq):
    r = ATTN_RADIUS
    kw_ref[0:r, :] = kp_ref[...]
    kw_ref[r:r + tq, :] = kc_ref[...]
    kw_ref[r + tq:, :] = kn_ref[...]
    vw_ref[0:r, :] = vp_ref[...]
    vw_ref[r:r + tq, :] = vc_ref[...]
    vw_ref[r + tq:, :] = vn_ref[...]

    base = pl.program_id(1) * tq
    nkeys = Q_SUB + 2 * r
    qi = lax.broadcasted_iota(jnp.int32, (Q_SUB, nkeys), 0)
    kj = lax.broadcasted_iota(jnp.int32, (Q_SUB, nkeys), 1)
    lane_head = lax.broadcasted_iota(jnp.int32, (Q_SUB, HEAD_DIM), 1) // LSE_LANES

    for sb in range(tq // Q_SUB):
        rows = slice(sb * Q_SUB, (sb + 1) * Q_SUB)
        first = base + sb * Q_SUB - r
        lo = jnp.maximum(qi, -first)
        hi = jnp.minimum(qi + 2 * r, seq - 1 - first)
        keep = (kj >= lo) & (kj <= hi)
        lse_tile = jnp.zeros((Q_SUB, HEAD_DIM), F32)
        for h in range(HEADS_PER_GROUP):
            cols = _head_cols(h)
            qs = q_ref[rows, cols]
            ks = kw_ref[sb * Q_SUB:sb * Q_SUB + nkeys, cols]
            vs = vw_ref[sb * Q_SUB:sb * Q_SUB + nkeys, cols]
            sc = lax.dot_general(qs, ks, (((1,), (1,)), ((), ())), preferred_element_type=F32)
            sc = jnp.where(keep, sc, NEG_INF)
            m = jnp.max(sc, axis=-1, keepdims=True)
            p = jnp.exp(sc - m)
            den = jnp.sum(p, axis=-1, keepdims=True)
            o = jnp.dot(p.astype(BF16), vs, preferred_element_type=F32) / den
            o_ref[rows, cols] = o.astype(o_ref.dtype)
            lse_tile = jnp.where(lane_head == h, m + jnp.log(den), lse_tile)
        l_ref[rows, :] = lse_tile


def _attn_group(qkv):
    _, dil, seq, _ = qkv.shape
    tq = _tile(seq, ATTN_Q_TILE)
    r = ATTN_RADIUS
    halo_per_tile = tq // r
    n_halo = seq // r

    def main(which):
        return pl.BlockSpec((None, None, tq, GROUP_WIDTH), lambda rr, lb: (which, rr, lb, 0))

    def before(which):
        return pl.BlockSpec((None, None, r, GROUP_WIDTH),
                            lambda rr, lb: (which, rr, jnp.maximum(lb * halo_per_tile - 1, 0), 0))

    def after(which):
        return pl.BlockSpec((None, None, r, GROUP_WIDTH),
                            lambda rr, lb: (which, rr, jnp.minimum((lb + 1) * halo_per_tile, n_halo - 1), 0))

    return pl.pallas_call(
        functools.partial(_attn_kernel, tq=tq, seq=seq),
        out_shape=(jax.ShapeDtypeStruct((dil, seq, GROUP_WIDTH), BF16),
                   jax.ShapeDtypeStruct((dil, seq, HEAD_DIM), F32)),
        grid=(dil, seq // tq),
        in_specs=[main(0), before(1), main(1), after(1), before(2), main(2), after(2)],
        out_specs=(pl.BlockSpec((None, tq, GROUP_WIDTH), lambda rr, lb: (rr, lb, 0)),
                   pl.BlockSpec((None, tq, HEAD_DIM), lambda rr, lb: (rr, lb, 0))),
        scratch_shapes=[pltpu.VMEM((tq + 2 * r, GROUP_WIDTH), BF16),
                        pltpu.VMEM((tq + 2 * r, GROUP_WIDTH), BF16)],
        compiler_params=_params("parallel", "arbitrary"),
        name=f"banded_attn_{dil}",
    )(qkv, qkv, qkv, qkv, qkv, qkv, qkv)


def _combine_kernel(*refs):
    ng = len(DILATIONS)
    o_refs, l_refs = refs[0:2 * ng:2], refs[1:2 * ng:2]
    out_ref, os_ref, ls_ref = refs[2 * ng:]
    t = out_ref.shape[0]
    for g, dil in enumerate(DILATIONS):
        if dil == 1:
            continue
        for r in range(dil):
            dst = pl.ds(r, t // dil, stride=dil)
            ls_ref[g, dst, :] = l_refs[g][r]
            for h in range(HEADS_PER_GROUP):
                os_ref[g, h, dst, :] = o_refs[g][r, :, _head_cols(h)].astype(F32)

    def lse_of(g):
        return l_refs[g][0] if DILATIONS[g] == 1 else ls_ref[g]

    top = functools.reduce(jnp.maximum, [lse_of(g) for g in range(ng)])
    e = [jnp.exp(lse_of(g) - top) for g in range(ng)]
    inv = 1.0 / functools.reduce(lambda a, b: a + b, e)
    w = [eg * inv for eg in e]
    for h in range(HEADS_PER_GROUP):
        acc = None
        for g, dil in enumerate(DILATIONS):
            og = o_refs[g][0, :, _head_cols(h)].astype(F32) if dil == 1 else os_ref[g, h]
            term = w[g][:, h * LSE_LANES:h * LSE_LANES + 1] * og
            acc = term if acc is None else acc + term
        out_ref[:, _head_cols(h)] = acc.astype(out_ref.dtype)


def _combine(outs):
    ng = len(DILATIONS)
    s = outs[0][0].shape[0] * outs[0][0].shape[1]
    t = _tile(s, COMBINE_TILE)
    args, in_specs = [], []
    for (o, l), dil in zip(outs, DILATIONS):
        args += [o, l]
        in_specs += [pl.BlockSpec((dil, t // dil, GROUP_WIDTH), lambda i: (0, i, 0)),
                     pl.BlockSpec((dil, t // dil, HEAD_DIM), lambda i: (0, i, 0))]
    return pl.pallas_call(
        _combine_kernel,
        out_shape=jax.ShapeDtypeStruct((s, GROUP_WIDTH), BF16),
        grid=(s // t,),
        in_specs=in_specs,
        out_specs=pl.BlockSpec((t, GROUP_WIDTH), lambda i: (i, 0)),
        scratch_shapes=[pltpu.VMEM((ng, HEADS_PER_GROUP, t, HEAD_DIM), F32),
                        pltpu.VMEM((ng, t, HEAD_DIM), F32)],
        compiler_params=_params("parallel"),
        name="attn_combine",
    )(*args)


def _dft_tables(s, cg):
    n1, n2 = DFT_ROWS, s // DFT_ROWS

    def cs(rows, cols, period):
        ang = 2.0 * np.pi * ((np.arange(rows)[:, None] * np.arange(cols)[None, :]) % period) / period
        return np.cos(ang), np.sin(ang)

    c1, s1 = cs(n1, n1, n1)
    c2, s2 = cs(n2, n2, n2)
    ct, st = cs(n1, n2, s)
    cc, sc = cs(cg, cg, cg)
    norm = 1.0 / math.sqrt(s * cg)
    as_bf16 = lambda a: jnp.asarray(a, F32).astype(BF16)
    return dict(
        w1=as_bf16(np.concatenate([c1, -s1], axis=0)),
        tw_cos=jnp.asarray(ct, F32), tw_sin=jnp.asarray(st, F32),
        w2_re=as_bf16(np.concatenate([c2, -s2], axis=0)),
        w2_im=as_bf16(np.concatenate([s2, c2], axis=0)),
        wc_re=as_bf16(cc * norm), wc_im=as_bf16(sc * norm))


LANES = 128


def _flatten_slabs(dst_ref, src_refs):
    for k, src in enumerate(src_refs):
        n = src.shape[0]
        dst_ref[k] = src[...].reshape(n * DFT_SLAB, LANES)


def _slab(ref, k, row, n):
    return ref[k, pl.ds(row, n, stride=DFT_SLAB), :]


def _lane_tiles(x, nt):
    return [x[:, t * LANES:(t + 1) * LANES] for t in range(nt)]


def _dft_stage1_kernel(*refs, nt):
    x_refs, (w1_ref, twc_ref, tws_ref, br_ref, bi_ref, xs_ref) = refs[:nt], refs[nt:]
    n1 = w1_ref.shape[1]
    _flatten_slabs(xs_ref, x_refs)
    lane = lax.broadcasted_iota(jnp.int32, twc_ref.shape, 1)
    for b in range(DFT_SLAB):
        n2 = pl.program_id(0) * DFT_SLAB + b
        xb = jnp.concatenate([_slab(xs_ref, k, b, n1).astype(BF16) for k in range(nt)], axis=1)
        a = jnp.dot(w1_ref[...], xb, preferred_element_type=F32)
        ar, ai = a[:n1], a[n1:]
        c = jnp.sum(jnp.where(lane == n2, twc_ref[...], 0.0), axis=1, keepdims=True)
        sn = jnp.sum(jnp.where(lane == n2, tws_ref[...], 0.0), axis=1, keepdims=True)
        br_ref[b] = ar * c + ai * sn
        bi_ref[b] = ai * c - ar * sn


def _dft_stage2_kernel(*refs, nt):
    br_refs, bi_refs = refs[:nt], refs[nt:2 * nt]
    w2r_ref, w2i_ref, wcr_ref, wci_ref, o_ref, bs_ref, scr_ref = refs[2 * nt:]
    n2 = w2r_ref.shape[1]
    cg = wcr_ref.shape[0]
    _flatten_slabs(bs_ref, br_refs + bi_refs)
    for kk in range(DFT_SLAB):
        br = jnp.concatenate([_slab(bs_ref, k, kk, n2).astype(BF16) for k in range(nt)], axis=1)
        bi = jnp.concatenate([_slab(bs_ref, nt + k, kk, n2).astype(BF16) for k in range(nt)], axis=1)
        z = (jnp.dot(w2r_ref[...], br, preferred_element_type=F32)
             + jnp.dot(w2i_ref[...], bi, preferred_element_type=F32))
        zr, zi = z[:n2].astype(BF16), z[n2:].astype(BF16)
        groups = []
        for g in range(nt * LANES // cg):
            cols = slice(g * cg, (g + 1) * cg)
            groups.append(jnp.dot(zr[:, cols], wcr_ref[...], preferred_element_type=F32)
                          + jnp.dot(zi[:, cols], wci_ref[...], preferred_element_type=F32))
        out = groups[0] if len(groups) == 1 else jnp.concatenate(groups, axis=1)
        for t, tile in enumerate(_lane_tiles(out, nt)):
            scr_ref[t, pl.ds(kk, n2, stride=DFT_SLAB), :] = tile
    for t in range(nt):
        o_ref[:, :, t * LANES:(t + 1) * LANES] = scr_ref[t].reshape(n2, DFT_SLAB, LANES)


def _fourier_mix(f):
    s, width = f.shape
    n1, n2 = DFT_ROWS, s // DFT_ROWS
    cg = width // FOURIER_GROUPS
    t = _dft_tables(s, cg)
    slab = DFT_SLAB
    full = lambda a: pl.BlockSpec(a.shape, lambda i, j: (0,) * a.ndim)

    def tile_specs(rows, nt, place):
        return [pl.BlockSpec((rows, slab, LANES), functools.partial(place, t=k)) for k in range(nt)]

    nt1 = min(4, width // LANES)
    br, bi = pl.pallas_call(
        functools.partial(_dft_stage1_kernel, nt=nt1),
        out_shape=(jax.ShapeDtypeStruct((n2, n1, width), F32),) * 2,
        grid=(n2 // slab, width // (nt1 * LANES)),
        in_specs=tile_specs(n1, nt1, lambda i, j, t: (0, i, j * nt1 + t))
        + [full(t["w1"]), full(t["tw_cos"]), full(t["tw_sin"])],
        out_specs=(pl.BlockSpec((slab, n1, nt1 * LANES), lambda i, j: (i, 0, j)),) * 2,
        scratch_shapes=[pltpu.VMEM((nt1, n1 * slab, LANES), F32)],
        compiler_params=_params("parallel", "parallel"),
        name="dft_stage1",
    )(*([f.reshape(n1, n2, width)] * nt1), t["w1"], t["tw_cos"], t["tw_sin"])

    nt2 = width // LANES
    stage2_in = tile_specs(n2, nt2, lambda i, j, t: (0, i, j * nt2 + t))
    out = pl.pallas_call(
        functools.partial(_dft_stage2_kernel, nt=nt2),
        out_shape=jax.ShapeDtypeStruct((n2, n1, width), F32),
        grid=(n1 // slab, width // (nt2 * LANES)),
        in_specs=stage2_in + stage2_in
        + [full(t["w2_re"]), full(t["w2_im"]), full(t["wc_re"]), full(t["wc_im"])],
        out_specs=pl.BlockSpec((n2, slab, nt2 * LANES), lambda i, j: (0, i, j)),
        scratch_shapes=[pltpu.VMEM((2 * nt2, n2 * slab, LANES), F32),
                        pltpu.VMEM((nt2, n2 * slab, LANES), F32)],
        compiler_params=_params("parallel", "parallel"),
        name="dft_stage2",
    )(*([br] * nt2), *([bi] * nt2), t["w2_re"], t["w2_im"], t["wc_re"], t["wc_im"])
    return out.reshape(s, width)


def _merge_kernel(a_ref, f_ref, wa_ref, wf_ref, ga_ref, gf_ref, o_ref):
    ab = jnp.dot(a_ref[...], wa_ref[...], preferred_element_type=F32)
    fb = jnp.dot(f_ref[...].astype(BF16), wf_ref[...], preferred_element_type=F32)
    o_ref[...] = (ga_ref[...].astype(F32) * ab + gf_ref[...].astype(F32) * fb).astype(o_ref.dtype)


def _merge(attn, four, wa, wf, ga, gf):
    s, ka = attn.shape
    kf = four.shape[1]
    n = wa.shape[1]
    tm, tn = _tile(s, ROW_TILE), _tile(n, MERGE_COL_TILE)
    return pl.pallas_call(
        _merge_kernel,
        out_shape=jax.ShapeDtypeStruct((s, n), BF16),
        grid=(s // tm, n // tn),
        in_specs=[pl.BlockSpec((tm, ka), lambda i, j: (i, 0)),
                  pl.BlockSpec((tm, kf), lambda i, j: (i, 0)),
                  pl.BlockSpec((ka, tn), lambda i, j: (0, j)),
                  pl.BlockSpec((kf, tn), lambda i, j: (0, j)),
                  pl.BlockSpec((tm, tn), lambda i, j: (i, j)),
                  pl.BlockSpec((tm, tn), lambda i, j: (i, j))],
        out_specs=pl.BlockSpec((tm, tn), lambda i, j: (i, j)),
        compiler_params=_params("parallel", "arbitrary"),
        name="branch_merge",
    )(attn, four, wa, wf, ga, gf)


def _proj_res_kernel(a_ref, w_ref, res_ref, gate_ref, o_ref, *, alpha):
    acc = jnp.dot(a_ref[...], w_ref[...], preferred_element_type=F32)
    o_ref[...] = alpha * res_ref[...] + gate_ref[...] * acc


def _proj_res(a, w, res, gate, alpha, tm_want, sides=()):
    s, kdim = a.shape
    n = w.shape[1]
    tm, tn = _tile(s, tm_want), _tile(n, COL_TILE)
    (out,), side_out = _host_call(
        functools.partial(_proj_res_kernel, alpha=alpha),
        grid=(s // tm, n // tn),
        in_specs=[pl.BlockSpec((tm, kdim), lambda i, j: (i, 0)),
                  pl.BlockSpec((kdim, tn), lambda i, j: (0, j)),
                  pl.BlockSpec((tm, tn), lambda i, j: (i, j)),
                  pl.BlockSpec((1, tn), lambda i, j: (0, j))],
        out_specs=[pl.BlockSpec((tm, tn), lambda i, j: (i, j))],
        out_shape=[jax.ShapeDtypeStruct((s, n), F32)],
        scratch=[], args=(a, w, res, gate), sem=("parallel", "arbitrary"), name="proj_res", sides=sides)
    return out, side_out


def _ln_out_kernel(t_ref, g_ref, b_ref, *refs):
    for rows in _row_chunks(t_ref.shape[0]):
        y = _normalize(t_ref[rows, :]) * g_ref[...] + b_ref[...]
        if len(refs) == 1:
            refs[0][rows, :] = y
        else:
            mod_ref, y_ref, h_ref = refs
            y_ref[rows, :] = y
            h_ref[rows, :] = (_normalize(y) * (1.0 + mod_ref[1:2, :]) + mod_ref[0:1, :]).astype(h_ref.dtype)


def _ln_out(t, g, b, mod3=None, rows=None):
    s, d = rows or t.shape[0], t.shape[1]
    tm = _tile(s, LN_ROW_TILE)
    row = pl.BlockSpec((tm, d), lambda i: (i, 0))
    vec = lambda rows: pl.BlockSpec((rows, d), lambda i: (0, 0))
    args, in_specs = [t, g, b], [row, vec(1), vec(1)]
    out_shape, out_specs = [jax.ShapeDtypeStruct((s, d), F32)], [row]
    if mod3 is not None:
        args.append(mod3)
        in_specs.append(vec(3))
        out_shape.append(jax.ShapeDtypeStruct((s, d), BF16))
        out_specs.append(row)
    return pl.pallas_call(
        _ln_out_kernel,
        out_shape=tuple(out_shape),
        grid=(s // tm,),
        in_specs=in_specs,
        out_specs=tuple(out_specs),
        compiler_params=_params("parallel"),
        name="ln_out",
    )(*args)


def _swiglu_kernel(x_ref, wg_ref, wu_ref, o_ref):
    x = x_ref[...]
    gpre = jnp.dot(x, wg_ref[...], preferred_element_type=F32)
    up = jnp.dot(x, wu_ref[...], preferred_element_type=F32)
    o_ref[...] = (gpre * _sigmoid(gpre) * up).astype(o_ref.dtype)


def _swiglu_ln_kernel(h0_ref, t_ref, g_ref, b_ref, mod_ref, wg_ref, wu_ref, o_ref, y_ref, hs_ref, *, chunks):
    i, j = pl.program_id(0), pl.program_id(1)
    cur = i % 2

    @pl.when((i == 0) & (j == 0))
    def _():
        hs_ref[0] = h0_ref[...]

    x = hs_ref[cur]
    gpre = jnp.dot(x, wg_ref[...], preferred_element_type=F32)
    up = jnp.dot(x, wu_ref[...], preferred_element_type=F32)
    o_ref[...] = (gpre * _sigmoid(gpre) * up).astype(o_ref.dtype)

    rows_per = t_ref.shape[0]
    base = pl.multiple_of(jnp.minimum(j, chunks - 1) * rows_per, rows_per)
    for rows in _row_chunks(rows_per):
        y = _normalize(t_ref[rows, :]) * g_ref[...] + b_ref[...]
        y_ref[rows, :] = y
        h2 = _normalize(y) * (1.0 + mod_ref[1:2, :]) + mod_ref[0:1, :]
        hs_ref[1 - cur, pl.ds(base + rows.start, rows.stop - rows.start), :] = h2.astype(hs_ref.dtype)


def _swiglu_ln(t, g, b, mod3, wg, wu, sides=()):
    s, kdim = t.shape
    n = wg.shape[1]
    tm, tn = _tile(s, ROW_TILE), _tile(n, FFN_COL_TILE)
    nb, nj = s // tm, n // tn
    rows_per = SIDE_MIN_ROWS
    while tm // rows_per > nj:
        rows_per *= 2
    chunks = tm // rows_per
    _, h0 = _ln_out(t, g, b, mod3, rows=tm)
    nxt = lambda i, j: (((i + 1) % nb) * chunks + jnp.minimum(j, chunks - 1), 0)
    vec = lambda r: pl.BlockSpec((r, kdim), lambda i, j: (0, 0))
    (out, y), side_out = _host_call(
        functools.partial(_swiglu_ln_kernel, chunks=chunks),
        grid=(nb, nj),
        in_specs=[pl.BlockSpec((tm, kdim), lambda i, j: (0, 0)),
                  pl.BlockSpec((rows_per, kdim), nxt), vec(1), vec(1), vec(3),
                  pl.BlockSpec((kdim, tn), lambda i, j: (0, j)),
                  pl.BlockSpec((kdim, tn), lambda i, j: (0, j))],
        out_specs=[pl.BlockSpec((tm, tn), lambda i, j: (i, j)), pl.BlockSpec((rows_per, kdim), nxt)],
        out_shape=[jax.ShapeDtypeStruct((s, n), BF16), jax.ShapeDtypeStruct((s, kdim), F32)],
        scratch=[pltpu.VMEM((2, tm, kdim), BF16)], args=(h0, t, g, b, mod3, wg, wu),
        sem=("arbitrary", "arbitrary"), name="swiglu_ln", sides=sides)
    return out, y, side_out


def _swiglu(h, wg, wu, sides=()):
    s, kdim = h.shape
    n = wg.shape[1]
    tm, tn = _tile(s, ROW_TILE), _tile(n, FFN_COL_TILE)
    (out,), side_out = _host_call(
        _swiglu_kernel,
        grid=(s // tm, n // tn),
        in_specs=[pl.BlockSpec((tm, kdim), lambda i, j: (i, 0)),
                  pl.BlockSpec((kdim, tn), lambda i, j: (0, j)),
                  pl.BlockSpec((kdim, tn), lambda i, j: (0, j))],
        out_specs=[pl.BlockSpec((tm, tn), lambda i, j: (i, j))],
        out_shape=[jax.ShapeDtypeStruct((s, n), BF16)],
        scratch=[], args=(h, wg, wu), sem=("parallel", "arbitrary"), name="swiglu", sides=sides)
    return out, side_out


def _gate_projs(h, w_in, d, sides):
    fw = d // FOURIER_WIDTH_DIVISOR
    f, out_f = _proj(h, w_in, 3 * ATTN_WIDTH, fw, False, F32, sides.get("f", ()))
    ga, out_ga = _proj(h, w_in, 3 * ATTN_WIDTH + fw, d, True, BF16, sides.get("ga", ()))
    gf, out_gf = _proj(h, w_in, 3 * ATTN_WIDTH + fw + d, d, True, BF16, sides.get("gf", ()))
    return (f, ga, gf), dict(f=out_f, ga=out_ga, gf=out_gf)


def _mixer(x, h, gates, gate_m, w_qkv, wts, alpha, rope, sides=()):
    f, ga, gf = gates
    s = x.shape[0]
    cos, sin = rope[0][:s], rope[1][:s]
    outs = [_attn_group(_qkv_proj(h, w_qkv, gi, cos, sin)) for gi in range(len(DILATIONS))]
    merged = _merge(_combine(outs), _fourier_mix(f), wts["w_attn_up"], wts["w_fourier_up"], ga, gf)
    return _proj_res(merged, wts["w_mix_out"], x, gate_m, alpha, ROW_TILE, sides)


def kernel(x_prompt, x_sample, c_prompt, c_sample, w_ada, b_ada, w_in, w_attn_up, w_fourier_up,
           w_mix_out, ln1_g, ln1_b, w_gate, w_up, w_down, ln2_g, ln2_b):
    depth = w_ada.shape[0]
    d = x_prompt.shape[-1]
    alpha = (2.0 * depth) ** 0.25
    xp, xs = x_prompt[0], x_sample[0]
    c2 = jnp.concatenate([c_prompt, c_sample], axis=0)
    row = lambda v: v.reshape(1, d)
    rope = _rope_tables(max(xp.shape[0], xs.shape[0]))
    for l in range(depth):
        mod = _ada_mod(c2, w_ada[l], b_ada[l]).reshape(2, N_MOD, d)
        mp, ms = mod[0], mod[1]
        g1, b1, g2, b2 = row(ln1_g[l]), row(ln1_b[l]), row(ln2_g[l]), row(ln2_b[l])

        hp = _ln_mod(xp, mp[0:3])
        gates_p, done = _gate_projs(hp, w_in[l], d, dict(
            f=[_side_cast(w_attn_up[l]), _side_cast(w_fourier_up[l])],
            ga=[_side_job(_ln_mod_kernel, xs, [BF16], [ms[0:3]]), _side_cast(w_in[l], 3 * ATTN_WIDTH)],
            gf=[_side_cast(w_gate[l]), _side_cast(w_mix_out[l])]))
        ((hs,), (w_qkv,)), ((wg,), (w_mix,)) = done["ga"], done["gf"]
        wts = dict(w_attn_up=done["f"][0][0], w_fourier_up=done["f"][1][0], w_mix_out=w_mix)
        t1p, ((wu,),) = _mixer(xp, hp, gates_p, mp[2:3], w_qkv, wts, alpha, rope, [_side_cast(w_up[l])])
        gates_s, _ = _gate_projs(hs, w_in[l], d, {})
        t1s, _ = _mixer(xs, hs, gates_s, ms[2:3], w_qkv, wts, alpha, rope)
        up, x1p, ((x1s, h2s), (wd,)) = _swiglu_ln(t1p, g1, b1, mp[3:6], wg, wu, [
            _side_job(_ln_out_kernel, t1s, [F32, BF16], [g1, b1, ms[3:6]]), _side_cast(w_down[l])])
        t2p, _ = _proj_res(up, wd, x1p, mp[5:6], alpha, DOWN_ROW_TILE)
        us, ((xp,),) = _swiglu(h2s, wg, wu, [_side_job(_ln_out_kernel, t2p, [F32], [g2, b2])])
        t2s, _ = _proj_res(us, wd, x1s, ms[5:6], alpha, DOWN_ROW_TILE)
        (xs,) = _ln_out(t2s, g2, b2)
    return (xp[None], xs[None])
```

```python
import functools
import math

import numpy as np
import jax
import jax.numpy as jnp
from jax import lax
from jax.experimental import pallas as pl
from jax.experimental.pallas import tpu as pltpu

HEAD_DIM = 128
HEADS_PER_GROUP = 8
GROUP_WIDTH = HEADS_PER_GROUP * HEAD_DIM
DILATIONS = (1, 4, 16)
ATTN_RADIUS = 64
ATTN_WIDTH = len(DILATIONS) * GROUP_WIDTH
FOURIER_WIDTH_DIVISOR = 4
FOURIER_GROUPS = 4
DFT_ROWS = 128
DFT_SLAB = 8
N_MOD = 6
ROPE_THETA = 10000.0
LN_EPS = 1e-5
NEG_INF = -1e30
Q_SUB = 128
LSE_LANES = HEAD_DIM // HEADS_PER_GROUP

VMEM_LIMIT_BYTES = 56 * 1024 * 1024

ROW_TILE = 1024
COL_TILE = 512
MERGE_COL_TILE = 1024
FFN_COL_TILE = 256
DOWN_ROW_TILE = 512
LN_ROW_TILE = 512
ADA_COL_TILE = 1024
ATTN_Q_TILE = 1024
COMBINE_TILE = 512

F32 = jnp.float32
BF16 = jnp.bfloat16


def _params(*sem):
    return pltpu.CompilerParams(dimension_semantics=sem, vmem_limit_bytes=VMEM_LIMIT_BYTES)


def _tile(n, want):
    t = min(n, want)
    while n % t:
        t //= 2
    return t


def _head_cols(h):
    return slice(h * HEAD_DIM, (h + 1) * HEAD_DIM)


def _sigmoid(x):
    return 0.5 * jnp.tanh(0.5 * x) + 0.5


SIDE_MIN_ROWS = 16


def _side_job(body, rows_arg, out_dtypes, const_args, cols=None):
    def build(n_steps, step_of):
        s, d = rows_arg.shape[0], cols or rows_arg.shape[1]
        sr = SIDE_MIN_ROWS
        while s // sr > n_steps:
            sr *= 2
        last = s // sr - 1
        row = pl.BlockSpec((sr, d), lambda *idx: (jnp.minimum(step_of(*idx), last), 0))
        consts = [pl.BlockSpec(c.shape, lambda *idx: (0, 0)) for c in const_args]
        return dict(body=body, args=[rows_arg, *const_args], in_specs=[row] + consts,
                    out_shape=[jax.ShapeDtypeStruct((s, d), dt) for dt in out_dtypes],
                    out_specs=[row] * len(out_dtypes))
    return build


def _host_call(body, grid, in_specs, out_specs, out_shape, scratch, args, sem, name, sides=()):
    n_in, n_out = len(in_specs), len(out_shape)
    if not sides:
        outs = pl.pallas_call(body, out_shape=tuple(out_shape), grid=grid, in_specs=in_specs,
                              out_specs=tuple(out_specs), scratch_shapes=scratch,
                              compiler_params=_params(*sem), name=name)(*args)
        return tuple(outs), []

    def step_of(*idx):
        step = idx[0]
        for axis in range(1, len(grid)):
            step = step * grid[axis] + idx[axis]
        return step

    jobs = [side(math.prod(grid), step_of) for side in sides]
    job_in = [spec for job in jobs for spec in job["in_specs"]]
    job_out = [spec for job in jobs for spec in job["out_specs"]]
    job_shape = [shape for job in jobs for shape in job["out_shape"]]
    job_args = [arg for job in jobs for arg in job["args"]]
    in_end, out_end = n_in + len(job_in), n_in + len(job_in) + n_out

    def fused(*refs):
        body(*refs[:n_in], *refs[in_end:out_end], *refs[out_end + len(job_out):])
        i, o = n_in, out_end
        for job in jobs:
            ni, no = len(job["in_specs"]), len(job["out_specs"])
            job["body"](*refs[i:i + ni], *refs[o:o + no])
            i, o = i + ni, o + no

    outs = pl.pallas_call(fused, out_shape=tuple(out_shape) + tuple(job_shape), grid=grid,
                          in_specs=list(in_specs) + job_in, out_specs=tuple(out_specs) + tuple(job_out),
                          scratch_shapes=scratch, compiler_params=_params(*("arbitrary",) * len(grid)),
                          name=name)(*args, *job_args)
    side_outs, o = [], n_out
    for job in jobs:
        side_outs.append(tuple(outs[o:o + len(job["out_specs"])]))
        o += len(job["out_specs"])
    return tuple(outs[:n_out]), side_outs


def _cast_kernel(x_ref, o_ref):
    o_ref[...] = x_ref[...].astype(o_ref.dtype)


def _side_cast(w, cols=None):
    return _side_job(_cast_kernel, w, [BF16], [], cols)


def _ada_kernel(ct_ref, w_ref, b_ref, o_ref, sb_ref):
    kdim, tn = w_ref.shape
    nt = tn // 128

    @pl.when(pl.program_id(0) == 0)
    def _():
        c = ct_ref[...]
        s = c * jax.nn.sigmoid(c)
        sb_ref[0] = jnp.broadcast_to(s[:, 0:1], (kdim, 128))
        sb_ref[1] = jnp.broadcast_to(s[:, 1:2], (kdim, 128))

    def body(kc, acc):
        r0 = pl.multiple_of(kc * 8, 8)
        s0 = sb_ref[0, pl.ds(r0, 8), :]
        s1 = sb_ref[1, pl.ds(r0, 8), :]
        out = []
        for t in range(nt):
            w = w_ref[pl.ds(r0, 8), t * 128:(t + 1) * 128]
            out.append(acc[2 * t] + w * s0)
            out.append(acc[2 * t + 1] + w * s1)
        return tuple(out)

    zero = jnp.zeros((8, 128), F32)
    acc = lax.fori_loop(0, kdim // 8, body, (zero,) * (2 * nt), unroll=8)
    for t in range(nt):
        cols = slice(t * 128, (t + 1) * 128)
        o_ref[0:1, cols] = jnp.sum(acc[2 * t], axis=0, keepdims=True) + b_ref[:, cols]
        o_ref[1:2, cols] = jnp.sum(acc[2 * t + 1], axis=0, keepdims=True) + b_ref[:, cols]


def _ada_mod(c2, w, b):
    kdim, n = w.shape
    tn = _tile(n, ADA_COL_TILE)
    return pl.pallas_call(
        _ada_kernel,
        out_shape=jax.ShapeDtypeStruct((2, n), F32),
        grid=(n // tn,),
        in_specs=[pl.BlockSpec((kdim, 2), lambda j: (0, 0)),
                  pl.BlockSpec((kdim, tn), lambda j: (0, j)),
                  pl.BlockSpec((1, tn), lambda j: (0, j))],
        out_specs=pl.BlockSpec((2, tn), lambda j: (0, j)),
        scratch_shapes=[pltpu.VMEM((2, kdim, 128), F32)],
        compiler_params=_params("arbitrary"),
        name="ada_mod",
    )(c2.T, w, b.reshape(1, n))


def _normalize(x):
    mu = jnp.mean(x, axis=-1, keepdims=True)
    xc = x - mu
    var = jnp.mean(xc * xc, axis=-1, keepdims=True)
    return xc * lax.rsqrt(var + LN_EPS)


LN_CHUNK = 16


def _row_chunks(n):
    step = min(n, LN_CHUNK)
    return [slice(r, r + step) for r in range(0, n, step)]


def _ln_mod_kernel(x_ref, mod_ref, o_ref):
    for rows in _row_chunks(x_ref.shape[0]):
        y = _normalize(x_ref[rows, :])
        o_ref[rows, :] = (y * (1.0 + mod_ref[1:2, :]) + mod_ref[0:1, :]).astype(o_ref.dtype)


def _ln_mod(x, mod3):
    s, d = x.shape
    tm = _tile(s, LN_ROW_TILE)
    return pl.pallas_call(
        _ln_mod_kernel,
        out_shape=jax.ShapeDtypeStruct((s, d), BF16),
        grid=(s // tm,),
        in_specs=[pl.BlockSpec((tm, d), lambda i: (i, 0)),
                  pl.BlockSpec((3, d), lambda i: (0, 0))],
        out_specs=pl.BlockSpec((tm, d), lambda i: (i, 0)),
        compiler_params=_params("parallel"),
        name="ln_mod",
    )(x, mod3)


def _proj_kernel(x_ref, w_ref, o_ref, wb_ref, *, sigmoid):
    @pl.when(pl.program_id(1) == 0)
    def _():
        wb_ref[...] = w_ref[...].astype(BF16)

    acc = jnp.dot(x_ref[...], wb_ref[...], preferred_element_type=F32)
    if sigmoid:
        acc = _sigmoid(acc)
    o_ref[...] = acc.astype(o_ref.dtype)


def _proj(h, w, col0, ncols, sigmoid, out_dtype, sides=()):
    s, kdim = h.shape
    tm, tn = _tile(s, ROW_TILE), _tile(math.gcd(col0, ncols), COL_TILE)
    j0 = col0 // tn
    (out,), side_out = _host_call(
        functools.partial(_proj_kernel, sigmoid=sigmoid),
        grid=(ncols // tn, s // tm),
        in_specs=[pl.BlockSpec((tm, kdim), lambda j, i: (i, 0)),
                  pl.BlockSpec((kdim, tn), lambda j, i: (0, j0 + j))],
        out_specs=[pl.BlockSpec((tm, tn), lambda j, i: (i, j))],
        out_shape=[jax.ShapeDtypeStruct((s, ncols), out_dtype)],
        scratch=[pltpu.VMEM((kdim, tn), BF16)], args=(h, w), sem=("parallel", "arbitrary"),
        name="in_proj", sides=sides)
    return out, side_out


def _qkv_kernel(x_ref, w_ref, cos_ref, sin_ref, o_ref, *scratch, dil, q_scale):
    j = pl.program_id(1)
    acc = jnp.dot(x_ref[...], w_ref[...], preferred_element_type=F32)
    tm = acc.shape[0]

    scale = jnp.where(j == 0, q_scale, 1.0)
    cos = jnp.where(j < 2, cos_ref[...] * scale, 1.0)
    sin = jnp.where(j < 2, sin_ref[...] * scale, 0.0)
    for h in range(HEADS_PER_GROUP):
        t = acc[:, _head_cols(h)]
        t = t * cos + pltpu.roll(t, HEAD_DIM // 2, 1) * sin
        if dil == 1:
            o_ref[0, :, _head_cols(h)] = t.astype(o_ref.dtype)
        else:
            scratch[0][h] = t

    if dil > 1:
        for r in range(dil):
            for h in range(HEADS_PER_GROUP):
                rows = scratch[0][h, pl.ds(r, tm // dil, stride=dil), :]
                o_ref[r, :, _head_cols(h)] = rows.astype(o_ref.dtype)


def _qkv_proj(h, w_qkv, gi, cos, sin):
    s, kdim = h.shape
    dil = DILATIONS[gi]
    tm = _tile(s, ROW_TILE)
    ncol = ATTN_WIDTH // GROUP_WIDTH
    scratch = [] if dil == 1 else [pltpu.VMEM((HEADS_PER_GROUP, tm, HEAD_DIM), F32)]
    (out,), _ = _host_call(
        functools.partial(_qkv_kernel, dil=dil, q_scale=HEAD_DIM ** -0.5),
        grid=(s // tm, 3),
        in_specs=[pl.BlockSpec((tm, kdim), lambda i, j: (i, 0)),
                  pl.BlockSpec((kdim, GROUP_WIDTH), lambda i, j: (0, ncol * j + gi)),
                  pl.BlockSpec((tm, HEAD_DIM), lambda i, j: (i, 0)),
                  pl.BlockSpec((tm, HEAD_DIM), lambda i, j: (i, 0))],
        out_specs=[pl.BlockSpec((None, dil, tm // dil, GROUP_WIDTH), lambda i, j: (j, 0, i, 0))],
        out_shape=[jax.ShapeDtypeStruct((3, dil, s // dil, GROUP_WIDTH), BF16)],
        scratch=scratch, args=(h, w_qkv, cos, sin), sem=("parallel", "arbitrary"),
        name=f"qkv_proj_{dil}")
    return out


def _rope_tables(s):
    half = HEAD_DIM // 2
    inv = ROPE_THETA ** (-jnp.arange(half, dtype=F32) / half)
    ang = jnp.arange(s).astype(F32)[:, None] * inv[None, :]
    cos, sin = jnp.cos(ang), jnp.sin(ang)
    return jnp.concatenate([cos, cos], axis=-1), jnp.concatenate([-sin, sin], axis=-1)


def _attn_kernel(q_ref, kp_ref, kc_ref, kn_ref, vp_ref, vc_ref, vn_ref, o_ref, l_ref, kw_ref, vw_ref,
                 *, tq, seq):
    r = ATTN_RADIUS
    kw_ref[0:r, :] = kp_ref[...]
    kw_ref[r:r + tq, :] = kc_ref[...]
    kw_ref[r + tq:, :] = kn_ref[...]
    vw_ref[0:r, :] = vp_ref[...]
    vw_ref[r:r + tq, :] = vc_ref[...]
    vw_ref[r + tq:, :] = vn_ref[...]

    base = pl.program_id(1) * tq
    nkeys = Q_SUB + 2 * r
    qi = lax.broadcasted_iota(jnp.int32, (Q_SUB, nkeys), 0)
    kj = lax.broadcasted_iota(jnp.int32, (Q_SUB, nkeys), 1)
    lane_head = lax.broadcasted_iota(jnp.int32, (Q_SUB, HEAD_DIM), 1) // LSE_LANES

    for sb in range(tq // Q_SUB):
        rows = slice(sb * Q_SUB, (sb + 1) * Q_SUB)
        first = base + sb * Q_SUB - r
        lo = jnp.maximum(qi, -first)
        hi = jnp.minimum(qi + 2 * r, seq - 1 - first)
        keep = (kj >= lo) & (kj <= hi)
        lse_tile = jnp.zeros((Q_SUB, HEAD_DIM), F32)
        for h in range(HEADS_PER_GROUP):
            cols = _head_cols(h)
            qs = q_ref[rows, cols]
            ks = kw_ref[sb * Q_SUB:sb * Q_SUB + nkeys, cols]
            vs = vw_ref[sb * Q_SUB:sb * Q_SUB + nkeys, cols]
            sc = lax.dot_general(qs, ks, (((1,), (1,)), ((), ())), preferred_element_type=F32)
            sc = jnp.where(keep, sc, NEG_INF)
            m = jnp.max(sc, axis=-1, keepdims=True)
            p = jnp.exp(sc - m)
            den = jnp.sum(p, axis=-1, keepdims=True)
            o = jnp.dot(p.astype(BF16), vs, preferred_element_type=F32) / den
            o_ref[rows, cols] = o.astype(o_ref.dtype)
            lse_tile = jnp.where(lane_head == h, m + jnp.log(den), lse_tile)
        l_ref[rows, :] = lse_tile


def _attn_group(qkv):
    _, dil, seq, _ = qkv.shape
    tq = _tile(seq, ATTN_Q_TILE)
    r = ATTN_RADIUS
    halo_per_tile = tq // r
    n_halo = seq // r

    def main(which):
        return pl.BlockSpec((None, None, tq, GROUP_WIDTH), lambda rr, lb: (which, rr, lb, 0))

    def before(which):
        return pl.BlockSpec((None, None, r, GROUP_WIDTH),
                            lambda rr, lb: (which, rr, jnp.maximum(lb * halo_per_tile - 1, 0), 0))

    def after(which):
        return pl.BlockSpec((None, None, r, GROUP_WIDTH),
                            lambda rr, lb: (which, rr, jnp.minimum((lb + 1) * halo_per_tile, n_halo - 1), 0))

    return pl.pallas_call(
        functools.partial(_attn_kernel, tq=tq, seq=seq),
        out_shape=(jax.ShapeDtypeStruct((dil, seq, GROUP_WIDTH), BF16),
                   jax.ShapeDtypeStruct((dil, seq, HEAD_DIM), F32)),
        grid=(dil, seq // tq),
        in_specs=[main(0), before(1), main(1), after(1), before(2), main(2), after(2)],
        out_specs=(pl.BlockSpec((None, tq, GROUP_WIDTH), lambda rr, lb: (rr, lb, 0)),
                   pl.BlockSpec((None, tq, HEAD_DIM), lambda rr, lb: (rr, lb, 0))),
        scratch_shapes=[pltpu.VMEM((tq + 2 * r, GROUP_WIDTH), BF16),
                        pltpu.VMEM((tq + 2 * r, GROUP_WIDTH), BF16)],
        compiler_params=_params("parallel", "arbitrary"),
        name=f"banded_attn_{dil}",
    )(qkv, qkv, qkv, qkv, qkv, qkv, qkv)


def _combine_kernel(*refs):
    ng = len(DILATIONS)
    o_refs, l_refs = refs[0:2 * ng:2], refs[1:2 * ng:2]
    out_ref, os_ref, ls_ref = refs[2 * ng:]
    t = out_ref.shape[0]
    for g, dil in enumerate(DILATIONS):
        if dil == 1:
            continue
        for r in range(dil):
            dst = pl.ds(r, t // dil, stride=dil)
            ls_ref[g, dst, :] = l_refs[g][r]
            for h in range(HEADS_PER_GROUP):
                os_ref[g, h, dst, :] = o_refs[g][r, :, _head_cols(h)].astype(F32)

    def lse_of(g):
        return l_refs[g][0] if DILATIONS[g] == 1 else ls_ref[g]

    top = functools.reduce(jnp.maximum, [lse_of(g) for g in range(ng)])
    e = [jnp.exp(lse_of(g) - top) for g in range(ng)]
    inv = 1.0 / functools.reduce(lambda a, b: a + b, e)
    w = [eg * inv for eg in e]
    for h in range(HEADS_PER_GROUP):
        acc = None
        for g, dil in enumerate(DILATIONS):
            og = o_refs[g][0, :, _head_cols(h)].astype(F32) if dil == 1 else os_ref[g, h]
            term = w[g][:, h * LSE_LANES:h * LSE_LANES + 1] * og
            acc = term if acc is None else acc + term
        out_ref[:, _head_cols(h)] = acc.astype(out_ref.dtype)


def _combine(outs):
    ng = len(DILATIONS)
    s = outs[0][0].shape[0] * outs[0][0].shape[1]
    t = _tile(s, COMBINE_TILE)
    args, in_specs = [], []
    for (o, l), dil in zip(outs, DILATIONS):
        args += [o, l]
        in_specs += [pl.BlockSpec((dil, t // dil, GROUP_WIDTH), lambda i: (0, i, 0)),
                     pl.BlockSpec((dil, t // dil, HEAD_DIM), lambda i: (0, i, 0))]
    return pl.pallas_call(
        _combine_kernel,
        out_shape=jax.ShapeDtypeStruct((s, GROUP_WIDTH), BF16),
        grid=(s // t,),
        in_specs=in_specs,
        out_specs=pl.BlockSpec((t, GROUP_WIDTH), lambda i: (i, 0)),
        scratch_shapes=[pltpu.VMEM((ng, HEADS_PER_GROUP, t, HEAD_DIM), F32),
                        pltpu.VMEM((ng, t, HEAD_DIM), F32)],
        compiler_params=_params("parallel"),
        name="attn_combine",
    )(*args)


def _dft_tables(s, cg):
    n1, n2 = DFT_ROWS, s // DFT_ROWS

    def cs(rows, cols, period):
        ang = 2.0 * np.pi * ((np.arange(rows)[:, None] * np.arange(cols)[None, :]) % period) / period
        return np.cos(ang), np.sin(ang)

    c1, s1 = cs(n1, n1, n1)
    c2, s2 = cs(n2, n2, n2)
    ct, st = cs(n1, n2, s)
    cc, sc = cs(cg, cg, cg)
    norm = 1.0 / math.sqrt(s * cg)
    as_bf16 = lambda a: jnp.asarray(a, F32).astype(BF16)
    return dict(
        w1=as_bf16(np.concatenate([c1, -s1], axis=0)),
        tw_cos=jnp.asarray(ct, F32), tw_sin=jnp.asarray(st, F32),
        w2_re=as_bf16(np.concatenate([c2, -s2], axis=0)),
        w2_im=as_bf16(np.concatenate([s2, c2], axis=0)),
        wc_re=as_bf16(cc * norm), wc_im=as_bf16(sc * norm))


LANES = 128


def _flatten_slabs(dst_ref, src_refs):
    for k, src in enumerate(src_refs):
        n = src.shape[0]
        dst_ref[k] = src[...].reshape(n * DFT_SLAB, LANES)


def _slab(ref, k, row, n):
    return ref[k, pl.ds(row, n, stride=DFT_SLAB), :]


def _lane_tiles(x, nt):
    return [x[:, t * LANES:(t + 1) * LANES] for t in range(nt)]


def _dft_stage1_kernel(*refs, nt):
    x_refs, (w1_ref, twc_ref, tws_ref, br_ref, bi_ref, xs_ref) = refs[:nt], refs[nt:]
    n1 = w1_ref.shape[1]
    _flatten_slabs(xs_ref, x_refs)
    lane = lax.broadcasted_iota(jnp.int32, twc_ref.shape, 1)
    for b in range(DFT_SLAB):
        n2 = pl.program_id(0) * DFT_SLAB + b
        xb = jnp.concatenate([_slab(xs_ref, k, b, n1).astype(BF16) for k in range(nt)], axis=1)
        a = jnp.dot(w1_ref[...], xb, preferred_element_type=F32)
        ar, ai = a[:n1], a[n1:]
        c = jnp.sum(jnp.where(lane == n2, twc_ref[...], 0.0), axis=1, keepdims=True)
        sn = jnp.sum(jnp.where(lane == n2, tws_ref[...], 0.0), axis=1, keepdims=True)
        br_ref[b] = ar * c + ai * sn
        bi_ref[b] = ai * c - ar * sn


def _dft_stage2_kernel(*refs, nt):
    br_refs, bi_refs = refs[:nt], refs[nt:2 * nt]
    w2r_ref, w2i_ref, wcr_ref, wci_ref, o_ref, bs_ref, scr_ref = refs[2 * nt:]
    n2 = w2r_ref.shape[1]
    cg = wcr_ref.shape[0]
    _flatten_slabs(bs_ref, br_refs + bi_refs)
    for kk in range(DFT_SLAB):
        br = jnp.concatenate([_slab(bs_ref, k, kk, n2).astype(BF16) for k in range(nt)], axis=1)
        bi = jnp.concatenate([_slab(bs_ref, nt + k, kk, n2).astype(BF16) for k in range(nt)], axis=1)
        z = (jnp.dot(w2r_ref[...], br, preferred_element_type=F32)
             + jnp.dot(w2i_ref[...], bi, preferred_element_type=F32))
        zr, zi = z[:n2].astype(BF16), z[n2:].astype(BF16)
        groups = []
        for g in range(nt * LANES // cg):
            cols = slice(g * cg, (g + 1) * cg)
            groups.append(jnp.dot(zr[:, cols], wcr_ref[...], preferred_element_type=F32)
                          + jnp.dot(zi[:, cols], wci_ref[...], preferred_element_type=F32))
        out = groups[0] if len(groups) == 1 else jnp.concatenate(groups, axis=1)
        for t, tile in enumerate(_lane_tiles(out, nt)):
            scr_ref[t, pl.ds(kk, n2, stride=DFT_SLAB), :] = tile
    for t in range(nt):
        o_ref[:, :, t * LANES:(t + 1) * LANES] = scr_ref[t].reshape(n2, DFT_SLAB, LANES)


def _fourier_mix(f):
    s, width = f.shape
    n1, n2 = DFT_ROWS, s // DFT_ROWS
    cg = width // FOURIER_GROUPS
    t = _dft_tables(s, cg)
    slab = DFT_SLAB
    full = lambda a: pl.BlockSpec(a.shape, lambda i, j: (0,) * a.ndim)

    def tile_specs(rows, nt, place):
        return [pl.BlockSpec((rows, slab, LANES), functools.partial(place, t=k)) for k in range(nt)]

    nt1 = min(4, width // LANES)
    br, bi = pl.pallas_call(
        functools.partial(_dft_stage1_kernel, nt=nt1),
        out_shape=(jax.ShapeDtypeStruct((n2, n1, width), F32),) * 2,
        grid=(n2 // slab, width // (nt1 * LANES)),
        in_specs=tile_specs(n1, nt1, lambda i, j, t: (0, i, j * nt1 + t))
        + [full(t["w1"]), full(t["tw_cos"]), full(t["tw_sin"])],
        out_specs=(pl.BlockSpec((slab, n1, nt1 * LANES), lambda i, j: (i, 0, j)),) * 2,
        scratch_shapes=[pltpu.VMEM((nt1, n1 * slab, LANES), F32)],
        compiler_params=_params("parallel", "parallel"),
        name="dft_stage1",
    )(*([f.reshape(n1, n2, width)] * nt1), t["w1"], t["tw_cos"], t["tw_sin"])

    nt2 = width // LANES
    stage2_in = tile_specs(n2, nt2, lambda i, j, t: (0, i, j * nt2 + t))
    out = pl.pallas_call(
        functools.partial(_dft_stage2_kernel, nt=nt2),
        out_shape=jax.ShapeDtypeStruct((n2, n1, width), F32),
        grid=(n1 // slab, width // (nt2 * LANES)),
        in_specs=stage2_in + stage2_in
        + [full(t["w2_re"]), full(t["w2_im"]), full(t["wc_re"]), full(t["wc_im"])],
        out_specs=pl.BlockSpec((n2, slab, nt2 * LANES), lambda i, j: (0, i, j)),
        scratch_shapes=[pltpu.VMEM((2 * nt2, n2 * slab, LANES), F32),
                        pltpu.VMEM((nt2, n2 * slab, LANES), F32)],
        compiler_params=_params("parallel", "parallel"),
        name="dft_stage2",
    )(*([br] * nt2), *([bi] * nt2), t["w2_re"], t["w2_im"], t["wc_re"], t["wc_im"])
    return out.reshape(s, width)


def _merge_kernel(a_ref, f_ref, wa_ref, wf_ref, ga_ref, gf_ref, o_ref):
    ab = jnp.dot(a_ref[...], wa_ref[...], preferred_element_type=F32)
    fb = jnp.dot(f_ref[...].astype(BF16), wf_ref[...], preferred_element_type=F32)
    o_ref[...] = (ga_ref[...].astype(F32) * ab + gf_ref[...].astype(F32) * fb).astype(o_ref.dtype)


def _merge(attn, four, wa, wf, ga, gf):
    s, ka = attn.shape
    kf = four.shape[1]
    n = wa.shape[1]
    tm, tn = _tile(s, ROW_TILE), _tile(n, MERGE_COL_TILE)
    return pl.pallas_call(
        _merge_kernel,
        out_shape=jax.ShapeDtypeStruct((s, n), BF16),
        grid=(s // tm, n // tn),
        in_specs=[pl.BlockSpec((tm, ka), lambda i, j: (i, 0)),
                  pl.BlockSpec((tm, kf), lambda i, j: (i, 0)),
                  pl.BlockSpec((ka, tn), lambda i, j: (0, j)),
                  pl.BlockSpec((kf, tn), lambda i, j: (0, j)),
                  pl.BlockSpec((tm, tn), lambda i, j: (i, j)),
                  pl.BlockSpec((tm, tn), lambda i, j: (i, j))],
        out_specs=pl.BlockSpec((tm, tn), lambda i, j: (i, j)),
        compiler_params=_params("parallel", "arbitrary"),
        name="branch_merge",
    )(attn, four, wa, wf, ga, gf)


def _proj_res_kernel(a_ref, w_ref, res_ref, gate_ref, o_ref, *, alpha):
    acc = jnp.dot(a_ref[...], w_ref[...], preferred_element_type=F32)
    o_ref[...] = alpha * res_ref[...] + gate_ref[...] * acc


def _proj_res(a, w, res, gate, alpha, tm_want, sides=(), weights_outer=False):
    s, kdim = a.shape
    n = w.shape[1]
    tm, tn = _tile(s, tm_want), _tile(n, COL_TILE)
    grid = (n // tn, s // tm) if weights_outer else (s // tm, n // tn)
    at = (lambda f: (lambda j, i: f(i, j))) if weights_outer else (lambda f: f)
    (out,), side_out = _host_call(
        functools.partial(_proj_res_kernel, alpha=alpha),
        grid=grid,
        in_specs=[pl.BlockSpec((tm, kdim), at(lambda i, j: (i, 0))),
                  pl.BlockSpec((kdim, tn), at(lambda i, j: (0, j))),
                  pl.BlockSpec((tm, tn), at(lambda i, j: (i, j))),
                  pl.BlockSpec((1, tn), at(lambda i, j: (0, j)))],
        out_specs=[pl.BlockSpec((tm, tn), at(lambda i, j: (i, j)))],
        out_shape=[jax.ShapeDtypeStruct((s, n), F32)],
        scratch=[], args=(a, w, res, gate), sem=("parallel", "arbitrary"), name="proj_res", sides=sides)
    return out, side_out


def _ln_out_kernel(t_ref, g_ref, b_ref, *refs):
    for rows in _row_chunks(t_ref.shape[0]):
        y = _normalize(t_ref[rows, :]) * g_ref[...] + b_ref[...]
        if len(refs) == 1:
            refs[0][rows, :] = y
        else:
            mod_ref, y_ref, h_ref = refs
            y_ref[rows, :] = y
            h_ref[rows, :] = (_normalize(y) * (1.0 + mod_ref[1:2, :]) + mod_ref[0:1, :]).astype(h_ref.dtype)


def _ln_out(t, g, b, mod3=None):
    s, d = t.shape
    tm = _tile(s, LN_ROW_TILE)
    row = pl.BlockSpec((tm, d), lambda i: (i, 0))
    vec = lambda rows: pl.BlockSpec((rows, d), lambda i: (0, 0))
    args, in_specs = [t, g, b], [row, vec(1), vec(1)]
    out_shape, out_specs = [jax.ShapeDtypeStruct((s, d), F32)], [row]
    if mod3 is not None:
        args.append(mod3)
        in_specs.append(vec(3))
        out_shape.append(jax.ShapeDtypeStruct((s, d), BF16))
        out_specs.append(row)
    return pl.pallas_call(
        _ln_out_kernel,
        out_shape=tuple(out_shape),
        grid=(s // tm,),
        in_specs=in_specs,
        out_specs=tuple(out_specs),
        compiler_params=_params("parallel"),
        name="ln_out",
    )(*args)


def _swiglu_kernel(x_ref, wg_ref, wu_ref, o_ref):
    x = x_ref[...]
    gpre = jnp.dot(x, wg_ref[...], preferred_element_type=F32)
    up = jnp.dot(x, wu_ref[...], preferred_element_type=F32)
    o_ref[...] = (gpre * _sigmoid(gpre) * up).astype(o_ref.dtype)


def _swiglu(h, wg, wu, sides=()):
    s, kdim = h.shape
    n = wg.shape[1]
    tm, tn = _tile(s, ROW_TILE), _tile(n, FFN_COL_TILE)
    (out,), side_out = _host_call(
        _swiglu_kernel,
        grid=(s // tm, n // tn),
        in_specs=[pl.BlockSpec((tm, kdim), lambda i, j: (i, 0)),
                  pl.BlockSpec((kdim, tn), lambda i, j: (0, j)),
                  pl.BlockSpec((kdim, tn), lambda i, j: (0, j))],
        out_specs=[pl.BlockSpec((tm, tn), lambda i, j: (i, j))],
        out_shape=[jax.ShapeDtypeStruct((s, n), BF16)],
        scratch=[], args=(h, wg, wu), sem=("parallel", "arbitrary"), name="swiglu", sides=sides)
    return out, side_out


def _gate_projs(h, w_in, d, sides):
    fw = d // FOURIER_WIDTH_DIVISOR
    f, out_f = _proj(h, w_in, 3 * ATTN_WIDTH, fw, False, F32, sides.get("f", ()))
    ga, out_ga = _proj(h, w_in, 3 * ATTN_WIDTH + fw, d, True, BF16, sides.get("ga", ()))
    gf, out_gf = _proj(h, w_in, 3 * ATTN_WIDTH + fw + d, d, True, BF16, sides.get("gf", ()))
    return (f, ga, gf), dict(f=out_f, ga=out_ga, gf=out_gf)


def _mixer(x, h, gates, gate_m, w_qkv, wts, alpha, rope, sides=()):
    f, ga, gf = gates
    s = x.shape[0]
    cos, sin = rope[0][:s], rope[1][:s]
    outs = [_attn_group(_qkv_proj(h, w_qkv, gi, cos, sin)) for gi in range(len(DILATIONS))]
    merged = _merge(_combine(outs), _fourier_mix(f), wts["w_attn_up"], wts["w_fourier_up"], ga, gf)
    return _proj_res(merged, wts["w_mix_out"], x, gate_m, alpha, ROW_TILE, sides)


def kernel(x_prompt, x_sample, c_prompt, c_sample, w_ada, b_ada, w_in, w_attn_up, w_fourier_up,
           w_mix_out, ln1_g, ln1_b, w_gate, w_up, w_down, ln2_g, ln2_b):
    depth = w_ada.shape[0]
    d = x_prompt.shape[-1]
    alpha = (2.0 * depth) ** 0.25
    xp, xs = x_prompt[0], x_sample[0]
    c2 = jnp.concatenate([c_prompt, c_sample], axis=0)
    row = lambda v: v.reshape(1, d)
    rope = _rope_tables(max(xp.shape[0], xs.shape[0]))
    for l in range(depth):
        mod = _ada_mod(c2, w_ada[l], b_ada[l]).reshape(2, N_MOD, d)
        mp, ms = mod[0], mod[1]
        g1, b1, g2, b2 = row(ln1_g[l]), row(ln1_b[l]), row(ln2_g[l]), row(ln2_b[l])

        hp = _ln_mod(xp, mp[0:3])
        gates_p, done = _gate_projs(hp, w_in[l], d, dict(
            f=[_side_cast(w_attn_up[l]), _side_cast(w_fourier_up[l])],
            ga=[_side_job(_ln_mod_kernel, xs, [BF16], [ms[0:3]]), _side_cast(w_in[l], 3 * ATTN_WIDTH)],
            gf=[_side_cast(w_gate[l]), _side_cast(w_mix_out[l])]))
        ((hs,), (w_qkv,)), ((wg,), (w_mix,)) = done["ga"], done["gf"]
        wts = dict(w_attn_up=done["f"][0][0], w_fourier_up=done["f"][1][0], w_mix_out=w_mix)
        t1p, ((wu,),) = _mixer(xp, hp, gates_p, mp[2:3], w_qkv, wts, alpha, rope, [_side_cast(w_up[l])])
        gates_s, _ = _gate_projs(hs, w_in[l], d, {})
        t1s, _ = _mixer(xs, hs, gates_s, ms[2:3], w_qkv, wts, alpha, rope)
        x1p, h2p = _ln_out(t1p, g1, b1, mp[3:6])
        up, ((x1s, h2s), (wd,)) = _swiglu(h2p, wg, wu, [
            _side_job(_ln_out_kernel, t1s, [F32, BF16], [g1, b1, ms[3:6]]), _side_cast(w_down[l])])
        t2p, _ = _proj_res(up, wd, x1p, mp[5:6], alpha, DOWN_ROW_TILE, weights_outer=True)
        us, ((xp,),) = _swiglu(h2s, wg, wu, [_side_job(_ln_out_kernel, t2p, [F32], [g2, b2])])
        t2s, _ = _proj_res(us, wd, x1s, ms[5:6], alpha, DOWN_ROW_TILE, weights_outer=True)
        (xs,) = _ln_out(t2s, g2, b2)
    return (xp[None], xs[None])
```
